```python
import math
import jax
import jax.numpy as jnp
from jax import lax
import numpy as np

D_MODEL = 1024
BATCH = 8
SEQ = 4096
DEPTH = 4

GRID_W = 64
PLE_DIM = 256
EPS = 1e-6
N_EVEN = (DEPTH + 1) // 2
N_ODD = DEPTH // 2

HY_W = D_MODEL
HY_GROUPS = 8
HY_BANDS = 16
HY_EMB = 1 + 2 * HY_BANDS
HY_HIDDEN = 64
HY_SHORT = 3
HY_FAST_DECAY = 0.3
HY_SLOW_DECAY = 1.5
HY_TARGET = 1e-2

GM_W = D_MODEL
GM_GROUPS = 8
GM_GROUP_CH = GM_W // GM_GROUPS
CHUNK = 128

EVEN_IN = 4 * HY_W + 3 * GM_W
EVEN_MIX = HY_W + GM_W

POOL_WINDOWS = (2, 4, 8, 16)
POOL_GROUPS = len(POOL_WINDOWS)
POOL_W = D_MODEL
POOL_GROUP_CH = POOL_W // POOL_GROUPS

NA_HEADS = 16
NA_HEAD_DIM = 64
NA_W = NA_HEADS * NA_HEAD_DIM
NA_KH_MAX = 8
NA_KW = 16

ODD_IN = 2 * POOL_W + 4 * NA_W
ODD_MIX = POOL_W + NA_W

kernel_name = "hybrid_hyena_gmlp_pool_natten_encoder"


def rmsnorm(x, g):
    xf = x.astype(jnp.float32)
    y = xf * lax.rsqrt(jnp.mean(xf * xf, axis=-1, keepdims=True) + EPS)
    return (y * g.astype(jnp.float32)).astype(x.dtype)


def centred_short_conv(x, w, b):
    xp = jnp.pad(x, ((0, 0), (1, 1), (0, 0)))
    return xp[:, :-2] * w[0] + xp[:, 1:-1] * w[1] + xp[:, 2:] * w[2] + b


def hyena_two_sided_filter(L, w0, b0, w1, b1, w2, b2, w_out, freq):
    f32 = jnp.float32
    t = jnp.linspace(0.0, 1.0, L, dtype=f32)[:, None]
    ang = 2.0 * math.pi * jnp.arange(L, dtype=f32)[:, None] / L
    bands = jnp.linspace(1e-4, HY_BANDS - 1, HY_BANDS, dtype=f32)[None, :]
    feats = jnp.concatenate([t, jnp.cos(bands * ang), -jnp.sin(bands * ang)], axis=-1)
    fr = freq.astype(f32)
    h = jnp.sin(fr * (feats @ w0.astype(f32) + b0.astype(f32)))
    h = jnp.sin(fr * (h @ w1.astype(f32) + b1.astype(f32)))
    h = jnp.sin(fr * (h @ w2.astype(f32) + b2.astype(f32)))
    k = (h @ w_out.astype(f32)).reshape(L, 2, HY_W)
    max_decay = math.log(HY_TARGET) / HY_FAST_DECAY
    min_decay = math.log(HY_TARGET) / HY_SLOW_DECAY
    deltas = jnp.linspace(min_decay, max_decay, HY_W, dtype=f32)
    k = k * jnp.exp(-t * jnp.abs(deltas))[:, None, :]
    k_fwd = k[:, 0]
    k_bwd = k[1:, 1][::-1]
    kc = jnp.concatenate([k_fwd, jnp.zeros((1, HY_W), f32), k_bwd], axis=0)
    return kc * lax.rsqrt(jnp.sum(kc * kc, axis=0, keepdims=True) + EPS)


def fft_long_conv(z, kc, d):
    L = z.shape[1]
    n = 2 * L
    zf = z.astype(jnp.float32)
    spec = jnp.fft.rfft(zf, n=n, axis=1) * jnp.fft.rfft(kc, n=n, axis=0)[None]
    y = jnp.fft.irfft(spec, n=n, axis=1)[:, :L]
    return (y + zf * d.astype(jnp.float32)).astype(z.dtype)


def even_mixer(hn, w_in, conv_w, conv_b, hy_w0, hy_b0, hy_w1, hy_b1, hy_w2, hy_b2,
               hy_wout, hy_freq, hy_d, gm_norm_g, gm_ws, gm_bs, w_out):
    B, L, _ = hn.shape
    proj = hn @ w_in
    hy_in, g_a, u_b, v_b, g_b = jnp.split(
        proj, [3 * HY_W, 4 * HY_W, 4 * HY_W + GM_W, 4 * HY_W + 2 * GM_W], axis=-1)
    hy_in = centred_short_conv(hy_in, conv_w, conv_b)
    x0, x1, v = jnp.split(hy_in, 3, axis=-1)
    kc = hyena_two_sided_filter(L, hy_w0, hy_b0, hy_w1, hy_b1, hy_w2, hy_b2, hy_wout, hy_freq)
    y_a = x0 * fft_long_conv(v * x1, kc, hy_d) * jax.nn.silu(g_a)
    v_c = rmsnorm(v_b, gm_norm_g).reshape(B, L // CHUNK, CHUNK, GM_GROUPS, GM_GROUP_CH)
    s = jnp.einsum('gpq,bnqgc->bnpgc', gm_ws, v_c) + gm_bs.T[None, None, :, :, None]
    y_b = u_b * s.reshape(B, L, GM_W) * jax.nn.silu(g_b)
    return jnp.concatenate([y_a, y_b], axis=-1) @ w_out


def multiscale_pool(xc):
    B, L, C = xc.shape
    xf = xc.astype(jnp.float32)
    cs = jnp.concatenate([jnp.zeros((B, 1, C), jnp.float32), jnp.cumsum(xf, axis=1)], axis=1)
    t = np.arange(L)
    outs = []
    for g, w in enumerate(POOL_WINDOWS):
        lo = np.clip(t - w // 2, 0, L)
        hi = np.clip(t + w // 2, 0, L)
        seg = cs[:, :, g * POOL_GROUP_CH:(g + 1) * POOL_GROUP_CH]
        cnt = jnp.asarray((hi - lo).astype(np.float32))[None, :, None]
        outs.append((jnp.take(seg, hi, axis=1) - jnp.take(seg, lo, axis=1)) / cnt)
    pooled = jnp.concatenate(outs, axis=-1)
    return (pooled - xf).astype(xc.dtype)


def neighbourhood_attention(q, k, v, rpb):
    B, L, H, Dh = q.shape
    rows = L // GRID_W
    kh = min(NA_KH_MAX, rows)
    kw = NA_KW
    qg = q.reshape(B, rows, GRID_W, H, Dh)
    kg = k.reshape(B, rows, GRID_W, H, Dh)
    vg = v.reshape(B, rows, GRID_W, H, Dh)
    cols = np.arange(GRID_W)
    col_start = np.clip(cols - kw // 2, 0, GRID_W - kw)
    col_idx = col_start[:, None] + np.arange(kw)[None, :]
    dc = col_idx - cols[:, None] + (NA_KW - 1)
    rpb_c = rpb[:, :, dc]
    scale = Dh ** -0.5

    def row_block(r):
        rs = jnp.clip(r - kh // 2, 0, rows - kh)
        kb = lax.dynamic_slice_in_dim(kg, rs, kh, axis=1)
        vb = lax.dynamic_slice_in_dim(vg, rs, kh, axis=1)
        k_win = kb[:, :, col_idx]
        v_win = vb[:, :, col_idx]
        q_row = lax.dynamic_index_in_dim(qg, r, axis=1, keepdims=False)
        s = jnp.einsum('bqhd,bjqkhd->bhqjk', q_row, k_win).astype(jnp.float32) * scale
        dr = rs + jnp.arange(kh) - r + (NA_KH_MAX - 1)
        bias = jnp.take(rpb_c, dr, axis=1).astype(jnp.float32)
        s = s + jnp.transpose(bias, (0, 2, 1, 3))[None]
        a = jax.nn.softmax(s.reshape(B, H, GRID_W, kh * kw), axis=-1).reshape(B, H, GRID_W, kh, kw)
        return jnp.einsum('bhqjk,bjqkhd->bqhd', a.astype(v.dtype), v_win)

    out = lax.map(row_block, jnp.arange(rows))
    return jnp.transpose(out, (1, 0, 2, 3, 4)).reshape(B, L, H * Dh)


def odd_mixer(hn, w_in, pool_w, pool_b, pool_scale, rpb, w_out):
    B, L, _ = hn.shape
    proj = hn @ w_in
    xc, g_c, q, k, v, g_d = jnp.split(
        proj, [POOL_W, 2 * POOL_W, 2 * POOL_W + NA_W, 2 * POOL_W + 2 * NA_W, 2 * POOL_W + 3 * NA_W], axis=-1)
    d = multiscale_pool(xc).reshape(B, L, POOL_GROUPS, POOL_GROUP_CH)
    y_c = jnp.einsum('bsgc,gcd->bsgd', d, pool_w).reshape(B, L, POOL_W) + pool_b
    y_c = y_c * pool_scale * jax.nn.silu(g_c)
    shp = (B, L, NA_HEADS, NA_HEAD_DIM)
    y_d = neighbourhood_attention(q.reshape(shp), k.reshape(shp), v.reshape(shp), rpb) * jax.nn.silu(g_d)
    return jnp.concatenate([y_c, y_d], axis=-1) @ w_out


def setup_inputs(seed: int = 0) -> dict:
    key = jax.random.key(seed)
    ks = jax.random.split(key, 32)
    f32 = jnp.float32

    def nrm(k, shape, scale):
        return jax.random.normal(k, shape, f32) * scale

    return {
        "x": nrm(ks[0], (BATCH, SEQ, D_MODEL), 1.0),
        "p": nrm(ks[1], (DEPTH, BATCH, SEQ, PLE_DIM), 1.0),
        "norm_g": 1.0 + nrm(ks[2], (DEPTH, D_MODEL), 0.1),
        "final_g": 1.0 + nrm(ks[3], (D_MODEL,), 0.1),
        "ev_w_in": nrm(ks[4], (N_EVEN, D_MODEL, EVEN_IN), D_MODEL ** -0.5),
        "ev_conv_w": nrm(ks[5], (N_EVEN, HY_SHORT, 3 * HY_W), HY_SHORT ** -0.5),
        "ev_conv_b": nrm(ks[6], (N_EVEN, 3 * HY_W), 0.01),
        "hy_w0": nrm(ks[7], (N_EVEN, HY_EMB, HY_HIDDEN), HY_EMB ** -0.5),
        "hy_b0": nrm(ks[8], (N_EVEN, HY_HIDDEN), 0.1),
        "hy_w1": nrm(ks[9], (N_EVEN, HY_HIDDEN, HY_HIDDEN), HY_HIDDEN ** -0.5),
        "hy_b1": nrm(ks[10], (N_EVEN, HY_HIDDEN), 0.1),
        "hy_w2": nrm(ks[11], (N_EVEN, HY_HIDDEN, HY_HIDDEN), HY_HIDDEN ** -0.5),
        "hy_b2": nrm(ks[12], (N_EVEN, HY_HIDDEN), 0.1),
        "hy_wout": nrm(ks[13], (N_EVEN, HY_HIDDEN, 2 * HY_W), HY_HIDDEN ** -0.5),
        "hy_freq": 1.0 + nrm(ks[14], (N_EVEN, HY_HIDDEN), 0.1),
        "hy_d": nrm(ks[15], (N_EVEN, HY_W), 1.0),
        "gm_norm_g": 1.0 + nrm(ks[16], (N_EVEN, GM_W), 0.1),
        "gm_ws": nrm(ks[17], (N_EVEN, GM_GROUPS, CHUNK, CHUNK), CHUNK ** -0.5),
        "gm_bs": 1.0 + nrm(ks[18], (N_EVEN, GM_GROUPS, CHUNK), 0.1),
        "ev_w_out": nrm(ks[19], (N_EVEN, EVEN_MIX, D_MODEL), EVEN_MIX ** -0.5),
        "od_w_in": nrm(ks[20], (N_ODD, D_MODEL, ODD_IN), D_MODEL ** -0.5),
        "pool_w": nrm(ks[21], (N_ODD, POOL_GROUPS, POOL_GROUP_CH, POOL_GROUP_CH), POOL_GROUP_CH ** -0.5),
        "pool_b": nrm(ks[22], (N_ODD, POOL_W), 0.01),
        "pool_scale": 1.0 + nrm(ks[23], (N_ODD, POOL_W), 0.1),
        "na_rpb": nrm(ks[24], (N_ODD, NA_HEADS, 2 * NA_KH_MAX - 1, 2 * NA_KW - 1), 0.1),
        "od_w_out": nrm(ks[25], (N_ODD, ODD_MIX, D_MODEL), ODD_MIX ** -0.5),
        "ple_up": nrm(ks[26], (DEPTH, PLE_DIM, D_MODEL), PLE_DIM ** -0.5),
        "ple_gate_w": nrm(ks[27], (DEPTH, D_MODEL, D_MODEL), D_MODEL ** -0.5),
        "ple_g": 1.0 + nrm(ks[28], (DEPTH, D_MODEL), 0.1),
    }


def reference(x, p, norm_g, final_g, ev_w_in, ev_conv_w, ev_conv_b, hy_w0, hy_b0, hy_w1, hy_b1,
              hy_w2, hy_b2, hy_wout, hy_freq, hy_d, gm_norm_g, gm_ws, gm_bs, ev_w_out,
              od_w_in, pool_w, pool_b, pool_scale, na_rpb, od_w_out, ple_up, ple_gate_w, ple_g):
    h = x
    for i in range(DEPTH):
        j = i // 2
        hn = rmsnorm(h, norm_g[i])
        if i % 2 == 0:
            mix = even_mixer(hn, ev_w_in[j], ev_conv_w[j], ev_conv_b[j], hy_w0[j], hy_b0[j],
                             hy_w1[j], hy_b1[j], hy_w2[j], hy_b2[j], hy_wout[j], hy_freq[j],
                             hy_d[j], gm_norm_g[j], gm_ws[j], gm_bs[j], ev_w_out[j])
        else:
            mix = odd_mixer(hn, od_w_in[j], pool_w[j], pool_b[j], pool_scale[j], na_rpb[j], od_w_out[j])
        h = h + mix
        gate = jax.nn.sigmoid(rmsnorm(h, ple_g[i]) @ ple_gate_w[i])
        h = h + (p[i] @ ple_up[i]) * gate
    return rmsnorm(h, final_g)
```

```python
import functools
import math

import jax
import jax.numpy as jnp
import numpy as np
from jax import lax
from jax.experimental import pallas as pl
from jax.experimental.pallas import tpu as pltpu

F32 = jnp.float32
BF16 = jnp.bfloat16

EPS = 1e-6
GRID_W = 64
PLE_DIM = 256
HY_BANDS = 16
HY_FAST_DECAY = 0.3
HY_SLOW_DECAY = 1.5
HY_TARGET = 1e-2
GM_GROUPS = 8
CHUNK = 128
POOL_WINDOWS = (2, 4, 8, 16)
NA_HEADS = 16
NA_HEAD_DIM = 64
NA_KH = 8
NA_KW = 16
FFT_N2 = 64
NEG_BIG = -1e30
VMEM_LIMIT = 56 * 1024 * 1024


def _cparams(*sem):
    return pltpu.CompilerParams(dimension_semantics=sem, vmem_limit_bytes=VMEM_LIMIT)


def _silu(x):
    return x * jax.nn.sigmoid(x)


def _rms(x, g):
    return x * lax.rsqrt(jnp.mean(x * x, axis=-1, keepdims=True) + EPS) * g


def _norm_matmul_kernel(h_ref, g_ref, w_ref, o_ref, hn_ref):
    @pl.when(pl.program_id(1) == 0)
    def _():
        hn_ref[...] = _rms(h_ref[...], g_ref[...]).astype(BF16)

    o_ref[...] = jnp.dot(hn_ref[...], w_ref[...], preferred_element_type=F32)


def norm_matmul(h, g, w):
    m, d = h.shape
    n = w.shape[1]
    tm = min(1024, m)
    tn = min(512, n)
    return pl.pallas_call(
        _norm_matmul_kernel,
        grid=(m // tm, n // tn),
        in_specs=[pl.BlockSpec((tm, d), lambda i, j: (i, 0)),
                  pl.BlockSpec((1, d), lambda i, j: (0, 0)),
                  pl.BlockSpec((d, tn), lambda i, j: (0, j))],
        out_specs=pl.BlockSpec((tm, tn), lambda i, j: (i, j)),
        out_shape=jax.ShapeDtypeStruct((m, n), F32),
        scratch_shapes=[pltpu.VMEM((tm, d), BF16)],
        compiler_params=_cparams("parallel", "arbitrary"),
        name="norm_matmul",
    )(h, g.reshape(1, d), w)


def _short_conv(x, w, b):
    n = x.shape[0]
    row = lax.broadcasted_iota(jnp.int32, x.shape, 0)
    xm = jnp.where(row == 0, 0.0, pltpu.roll(x, 1, 0))
    xp = jnp.where(row == n - 1, 0.0, pltpu.roll(x, n - 1, 0))
    return xm * w[0:1] + x * w[1:2] + xp * w[2:3] + b


def _hyena_prep_kernel(x0_ref, x1_ref, v_ref, ga_ref, w0_ref, w1_ref, w2_ref,
                       b0_ref, b1_ref, b2_ref, z_ref, xg_ref):
    x1 = _short_conv(x1_ref[...], w1_ref[...], b1_ref[...])
    v = _short_conv(v_ref[...], w2_ref[...], b2_ref[...])
    z_ref[...] = x1 * v
    x0 = _short_conv(x0_ref[...], w0_ref[...], b0_ref[...])
    xg_ref[...] = x0 * _silu(ga_ref[...])


def hyena_prep(proj, conv_w, conv_b, bsz, seq, c):
    ct = min(128, c)
    nc = c // ct
    blk = lambda off: pl.BlockSpec((seq, ct), lambda b, j: (b, j + off * nc))
    wblk = lambda off: pl.BlockSpec((3, ct), lambda b, j: (0, j + off * nc))
    bblk = lambda off: pl.BlockSpec((1, ct), lambda b, j: (0, j + off * nc))
    out = jax.ShapeDtypeStruct((bsz * seq, c), F32)
    return pl.pallas_call(
        _hyena_prep_kernel,
        grid=(bsz, nc),
        in_specs=[blk(0), blk(1), blk(2), blk(3), wblk(0), wblk(1), wblk(2), bblk(0), bblk(1), bblk(2)],
        out_specs=[pl.BlockSpec((seq, ct), lambda b, j: (b, j))] * 2,
        out_shape=[out, out],
        compiler_params=_cparams("parallel", "parallel"),
        name="hyena_prep",
    )(proj, proj, proj, proj, conv_w, conv_w, conv_w,
      conv_b.reshape(1, -1), conv_b.reshape(1, -1), conv_b.reshape(1, -1))


def _positional_features(seq):
    f32 = np.float32
    t = np.linspace(0.0, 1.0, seq, dtype=f32)[:, None]
    ang = (f32(2.0 * math.pi) * np.arange(seq, dtype=f32)[:, None] / f32(seq)).astype(f32)
    bands = np.linspace(1e-4, HY_BANDS - 1, HY_BANDS, dtype=f32)[None, :]
    ba = (bands * ang).astype(f32)
    feats = np.concatenate([t, np.cos(ba), -np.sin(ba)], axis=-1).astype(f32)
    feats2 = np.concatenate([feats, feats[:1], feats[1:][::-1]], axis=0)
    pad = np.zeros((2 * seq, 128 - feats2.shape[1]), f32)
    return np.concatenate([feats2, pad], axis=1)


def _filter_kernel(feat_ref, w0_ref, b0_ref, w1_ref, b1_ref, w2_ref, b2_ref, wo_ref, fr_ref, dl_ref,
                   kc_ref, ss_ref, *, seq, tr):
    hp = lax.Precision.HIGHEST
    i = pl.program_id(0)
    fr = fr_ref[...]
    h = jnp.sin(fr * (jnp.dot(feat_ref[...], w0_ref[...], precision=hp, preferred_element_type=F32) + b0_ref[...]))
    h = jnp.sin(fr * (jnp.dot(h, w1_ref[...], precision=hp, preferred_element_type=F32) + b1_ref[...]))
    h = jnp.sin(fr * (jnp.dot(h, w2_ref[...], precision=hp, preferred_element_type=F32) + b2_ref[...]))
    k2 = jnp.dot(h, wo_ref[...], precision=hp, preferred_element_type=F32)
    c = kc_ref.shape[1]
    s = i * tr + lax.broadcasted_iota(jnp.int32, (tr, 1), 0)
    k = jnp.where(s < seq, k2[:, :c], k2[:, c:])
    lag = jnp.where(s < seq, s, 2 * seq - s).astype(F32)
    t = lag * (1.0 / (seq - 1))
    k = jnp.where(s == seq, 0.0, k * jnp.exp(-t * dl_ref[...]))
    kc_ref[...] = k

    @pl.when(i == 0)
    def _():
        ss_ref[...] = jnp.zeros_like(ss_ref)

    ss_ref[...] += jnp.sum(k * k, axis=0, keepdims=True)


def hyena_filter(seq, c, w0, b0, w1, b1, w2, b2, wout, freq):
    feats = jnp.asarray(_positional_features(seq))
    hid = w0.shape[1]
    w0p = jnp.zeros((128, hid), F32).at[:w0.shape[0]].set(w0)
    max_decay = math.log(HY_TARGET) / HY_FAST_DECAY
    min_decay = math.log(HY_TARGET) / HY_SLOW_DECAY
    deltas = jnp.asarray(np.abs(np.linspace(min_decay, max_decay, c, dtype=np.float32)))[None, :]
    tr = min(512, 2 * seq)
    full = lambda a: pl.BlockSpec(a.shape, lambda i: (0,) * a.ndim)
    row = lambda a: a.reshape(1, -1)
    args = (feats, w0p, row(b0), w1, row(b1), w2, row(b2), wout, row(freq), deltas)
    return pl.pallas_call(
        functools.partial(_filter_kernel, seq=seq, tr=tr),
        grid=(2 * seq // tr,),
        in_specs=[pl.BlockSpec((tr, 128), lambda i: (i, 0))] + [full(a) for a in args[1:]],
        out_specs=[pl.BlockSpec((tr, c), lambda i: (i, 0)), pl.BlockSpec((1, c), lambda i: (0, 0))],
        out_shape=[jax.ShapeDtypeStruct((2 * seq, c), F32), jax.ShapeDtypeStruct((1, c), F32)],
        compiler_params=_cparams("arbitrary"),
        name="hyena_filter",
    )(*args)


def _dft_tables(seq):
    n = 2 * seq
    n2 = FFT_N2
    n1 = n // n2
    f1 = np.arange(n1)[:, None]
    s1 = np.arange(n1)[None, :]
    a = 2.0 * np.pi * (f1 * s1 % n1) / n1
    fa = np.empty((2 * n1, n1), np.float64)
    fa[0::2] = np.cos(a)
    fa[1::2] = -np.sin(a)
    ia = np.empty((n1 // 2, 2 * n1), np.float64)
    at = a.T[: n1 // 2]
    ia[:, 0::2] = np.cos(at) / n
    ia[:, 1::2] = -np.sin(at) / n
    f1b = np.arange(n1)[:, None, None]
    f2 = np.arange(n2)[None, :, None]
    s2 = np.arange(n2)[None, None, :]
    ph = 2.0 * np.pi * ((s2 * f2 * n1 + s2 * f1b) % n) / n
    gr, gi = np.cos(ph), -np.sin(ph)
    gf = np.concatenate([np.concatenate([gr, -gi], axis=2),
                         np.concatenate([gi, gr], axis=2)], axis=1)
    er, ei = np.transpose(gr, (0, 2, 1)), -np.transpose(gi, (0, 2, 1))
    gb = np.concatenate([np.concatenate([er, -ei], axis=2),
                         np.concatenate([ei, er], axis=2)], axis=1)
    return fa.astype(np.float32), ia.astype(np.float32), gf.astype(np.float32), gb.astype(np.float32)


def _lmul_kernel(w_ref, x_ref, o_ref, *, precise):
    if precise:
        o_ref[...] = jnp.dot(w_ref[...], x_ref[...], precision=lax.Precision.HIGHEST,
                             preferred_element_type=F32)
    else:
        o_ref[...] = jnp.dot(w_ref[...], x_ref[...].astype(BF16), preferred_element_type=F32)


def lmul(w, x, precise=False):
    bsz, k, n = x.shape
    m = w.shape[0]
    nt = min(2048, n)
    return pl.pallas_call(
        functools.partial(_lmul_kernel, precise=precise),
        grid=(bsz, n // nt),
        in_specs=[pl.BlockSpec((m, k), lambda b, j: (0, 0)),
                  pl.BlockSpec((None, k, nt), lambda b, j: (b, 0, j))],
        out_specs=pl.BlockSpec((None, m, nt), lambda b, j: (b, 0, j)),
        out_shape=jax.ShapeDtypeStruct((bsz, m, n), F32),
        compiler_params=_cparams("parallel", "parallel"),
        name="dft_stage_a",
    )(w, x)


def _bmm(g, x, precise):
    dn = (((2,), (1,)), ((0,), (0,)))
    if precise:
        return lax.dot_general(g, x, dn, precision=lax.Precision.HIGHEST, preferred_element_type=F32)
    return lax.dot_general(g, x.astype(BF16), dn, preferred_element_type=F32)


def _filter_spectrum_kernel(g_ref, x_ref, ss_ref, o_ref):
    o_ref[...] = _bmm(g_ref[...], x_ref[...], True) * lax.rsqrt(ss_ref[...] + EPS)


def filter_spectrum(gf, x, ss):
    n1, k2, c = x.shape
    ft = min(8, n1)
    ct = min(256, c)
    return pl.pallas_call(
        _filter_spectrum_kernel,
        grid=(n1 // ft, c // ct),
        in_specs=[pl.BlockSpec((ft, k2, k2), lambda i, j: (i, 0, 0)),
                  pl.BlockSpec((ft, k2, ct), lambda i, j: (i, 0, j)),
                  pl.BlockSpec((1, ct), lambda i, j: (0, j))],
        out_specs=pl.BlockSpec((ft, k2, ct), lambda i, j: (i, 0, j)),
        out_shape=jax.ShapeDtypeStruct((n1, k2, c), F32),
        compiler_params=_cparams("parallel", "parallel"),
        name="filter_spectrum",
    )(gf, x, ss)


def _spectral_kernel(gf_ref, gb_ref, h_ref, x_ref, o_ref):
    n2 = h_ref.shape[1] // 2
    spec = _bmm(gf_ref[...], x_ref[...], False)
    sr, si = spec[:, :n2], spec[:, n2:]
    hr, hi = h_ref[:, :n2], h_ref[:, n2:]
    prod = jnp.concatenate([sr * hr - si * hi, sr * hi + si * hr], axis=1)
    o_ref[...] = _bmm(gb_ref[...], prod, False)


def spectral_multiply(gf, gb, hspec, x):
    bsz, n1, k2, c = x.shape
    ft = min(8, n1)
    ct = min(512, c)
    gspec = pl.BlockSpec((ft, k2, k2), lambda i, j, b: (i, 0, 0))
    return pl.pallas_call(
        _spectral_kernel,
        grid=(n1 // ft, c // ct, bsz),
        in_specs=[gspec, gspec,
                  pl.BlockSpec((ft, k2, ct), lambda i, j, b: (i, 0, j)),
                  pl.BlockSpec((None, ft, k2, ct), lambda i, j, b: (b, i, 0, j))],
        out_specs=pl.BlockSpec((None, ft, k2, ct), lambda i, j, b: (b, i, 0, j)),
        out_shape=jax.ShapeDtypeStruct(x.shape, F32),
        compiler_params=_cparams("parallel", "parallel", "arbitrary"),
        name="spectral_multiply",
    )(gf, gb, hspec, x)


def _inverse_a_kernel(w_ref, x_ref, z_ref, xg_ref, d_ref, o_ref):
    y = jnp.dot(w_ref[...], x_ref[...].astype(BF16), preferred_element_type=F32)
    o_ref[...] = (y + z_ref[...] * d_ref[...]) * xg_ref[...]


def inverse_stage_a(w, x, z, xg, d):
    bsz, k, n = x.shape
    m = w.shape[0]
    c = d.shape[0]
    nt = min(2048, n)
    dt = jnp.tile(d.reshape(1, c), (1, nt // c)) if nt >= c else d.reshape(1, c)
    dspec = (pl.BlockSpec((1, nt), lambda b, j: (0, 0)) if nt >= c
             else pl.BlockSpec((1, nt), lambda b, j: (0, j % (c // nt))))
    blk = pl.BlockSpec((None, m, nt), lambda b, j: (b, 0, j))
    return pl.pallas_call(
        _inverse_a_kernel,
        grid=(bsz, n // nt),
        in_specs=[pl.BlockSpec((m, k), lambda b, j: (0, 0)),
                  pl.BlockSpec((None, k, nt), lambda b, j: (b, 0, j)),
                  blk, blk, dspec],
        out_specs=blk,
        out_shape=jax.ShapeDtypeStruct((bsz, m, n), F32),
        compiler_params=_cparams("parallel", "parallel"),
        name="inverse_stage_a",
    )(w, x, z, xg, dt)


def hyena_long_conv(z, xg, d, kc_raw, ss, bsz, seq, c):
    n2 = FFT_N2
    n1 = 2 * seq // n2
    fa, ia, gf, gb = _dft_tables(seq)
    kspec = lmul(jnp.asarray(fa), kc_raw.reshape(1, n1, n2 * c), precise=True)
    hspec = filter_spectrum(jnp.asarray(gf), kspec.reshape(n1, 2 * n2, c), ss)
    zs = z.reshape(bsz, n1 // 2, n2 * c)
    za = lmul(jnp.asarray(fa[:, : n1 // 2], dtype=BF16), zs)
    zb = spectral_multiply(jnp.asarray(gf, dtype=BF16), jnp.asarray(gb, dtype=BF16), hspec,
                           za.reshape(bsz, n1, 2 * n2, c))
    ya = inverse_stage_a(jnp.asarray(ia, dtype=BF16), zb.reshape(bsz, 2 * n1, n2 * c),
                         zs, xg.reshape(bsz, n1 // 2, n2 * c), d)
    return ya.reshape(bsz * seq, c)


def _gmlp_kernel(u_ref, v_ref, gb_ref, ng_ref, ws_ref, bias_ref, o_ref):
    vn = _rms(v_ref[...], ng_ref[...]).astype(BF16)
    tr, c = vn.shape
    gc = c // GM_GROUPS
    for n in range(tr // CHUNK):
        rows = slice(n * CHUNK, (n + 1) * CHUNK)
        for g in range(GM_GROUPS):
            cols = slice(g * gc, (g + 1) * gc)
            s = jnp.dot(ws_ref[g], vn[rows, cols], preferred_element_type=F32) + bias_ref[:, cols]
            o_ref[rows, cols] = u_ref[rows, cols] * s * _silu(gb_ref[rows, cols])


def gmlp_branch(proj, norm_g, ws, bs, c, col0):
    m = proj.shape[0]
    tr = min(512, m)
    off = col0 // c
    bias = jnp.repeat(bs.T, c // GM_GROUPS, axis=1)
    blk = lambda o: pl.BlockSpec((tr, c), lambda i: (i, off + o))
    return pl.pallas_call(
        _gmlp_kernel,
        grid=(m // tr,),
        in_specs=[blk(0), blk(1), blk(2),
                  pl.BlockSpec((1, c), lambda i: (0, 0)),
                  pl.BlockSpec(ws.shape, lambda i: (0, 0, 0)),
                  pl.BlockSpec(bias.shape, lambda i: (0, 0))],
        out_specs=pl.BlockSpec((tr, c), lambda i: (i, 0)),
        out_shape=jax.ShapeDtypeStruct((m, c), F32),
        compiler_params=_cparams("parallel"),
        name="gmlp_branch",
    )(proj, proj, proj, norm_g.reshape(1, c), ws.astype(BF16), bias)


def _out_ple_kernel(ya_ref, yb_ref, h_ref, p_ref, wa_ref, wb_ref, pg_ref, gw_ref, up_ref, fg_ref, o_ref,
                    *, final):
    mix = (jnp.dot(ya_ref[...].astype(BF16), wa_ref[...], preferred_element_type=F32)
           + jnp.dot(yb_ref[...].astype(BF16), wb_ref[...], preferred_element_type=F32))
    h1 = h_ref[...] + mix
    r = _rms(h1, pg_ref[...]).astype(BF16)
    gate = jax.nn.sigmoid(jnp.dot(r, gw_ref[...], preferred_element_type=F32))
    up = jnp.dot(p_ref[...].astype(BF16), up_ref[...], preferred_element_type=F32)
    h2 = h1 + up * gate
    if final:
        h2 = _rms(h2, fg_ref[...])
    o_ref[...] = h2


def out_ple(ya, yb, h, p, w_out, ple_g, gate_w, ple_up, final_g, final):
    m, d = h.shape
    ca = ya.shape[1]
    tm = min(512, m)
    rowblk = lambda a: pl.BlockSpec((tm, a.shape[1]), lambda i: (i, 0))
    full = lambda a: pl.BlockSpec(a.shape, lambda i: (0, 0))
    wa = w_out[:ca].astype(BF16)
    wb = w_out[ca:].astype(BF16)
    args = (ya, yb, h, p, wa, wb, ple_g.reshape(1, d), gate_w.astype(BF16), ple_up.astype(BF16),
            final_g.reshape(1, d))
    return pl.pallas_call(
        functools.partial(_out_ple_kernel, final=final),
        grid=(m // tm,),
        in_specs=[rowblk(a) for a in args[:4]] + [full(a) for a in args[4:]],
        out_specs=pl.BlockSpec((tm, d), lambda i: (i, 0)),
        out_shape=jax.ShapeDtypeStruct((m, d), F32),
        compiler_params=_cparams("parallel"),
        name="out_ple",
    )(*args)


POOL_HALO = 8


def _pool_kernel(x_ref, prev_ref, next_ref, gc_ref, w_ref, b_ref, sc_ref, o_ref, pad_ref, *, seq):
    tr, c = x_ref.shape
    i = pl.program_id(0)
    t0 = (i * tr) % seq
    zeros = jnp.zeros((POOL_HALO, c), F32)
    pad_ref[0:POOL_HALO] = jnp.where(t0 == 0, zeros, prev_ref[...])
    pad_ref[POOL_HALO:POOL_HALO + tr] = x_ref[...]
    pad_ref[POOL_HALO + tr:] = jnp.where(t0 + tr == seq, zeros, next_ref[...])
    t = t0 + lax.broadcasted_iota(jnp.int32, (tr, 1), 0)
    gcw = c // len(POOL_WINDOWS)
    for g, w in enumerate(POOL_WINDOWS):
        cols = slice(g * gcw, (g + 1) * gcw)
        acc = pad_ref[POOL_HALO - w // 2:POOL_HALO - w // 2 + tr, cols]
        for o in range(-w // 2 + 1, w // 2):
            acc = acc + pad_ref[POOL_HALO + o:POOL_HALO + o + tr, cols]
        cnt = (jnp.minimum(t + w // 2, seq) - jnp.maximum(t - w // 2, 0)).astype(F32)
        dlt = acc / cnt - x_ref[:, cols]
        y = jnp.dot(dlt.astype(BF16), w_ref[g], preferred_element_type=F32) + b_ref[:, cols]
        o_ref[:, cols] = y * sc_ref[:, cols] * _silu(gc_ref[:, cols])


def pool_branch(proj, pool_w, pool_b, pool_scale, seq, c):
    m = proj.shape[0]
    tr = min(512, seq)
    hb = tr // POOL_HALO
    nblk = m // POOL_HALO
    return pl.pallas_call(
        functools.partial(_pool_kernel, seq=seq),
        grid=(m // tr,),
        in_specs=[pl.BlockSpec((tr, c), lambda i: (i, 0)),
                  pl.BlockSpec((POOL_HALO, c), lambda i: (jnp.maximum(i * hb - 1, 0), 0)),
                  pl.BlockSpec((POOL_HALO, c), lambda i: (jnp.minimum((i + 1) * hb, nblk - 1), 0)),
                  pl.BlockSpec((tr, c), lambda i: (i, 1)),
                  pl.BlockSpec(pool_w.shape, lambda i: (0, 0, 0)),
                  pl.BlockSpec((1, c), lambda i: (0, 0)),
                  pl.BlockSpec((1, c), lambda i: (0, 0))],
        out_specs=pl.BlockSpec((tr, c), lambda i: (i, 0)),
        out_shape=jax.ShapeDtypeStruct((m, c), F32),
        scratch_shapes=[pltpu.VMEM((tr + 2 * POOL_HALO, c), F32)],
        compiler_params=_cparams("parallel"),
        name="pool_branch",
    )(proj, proj, proj, proj, pool_w.astype(BF16), pool_b.reshape(1, c), pool_scale.reshape(1, c))


NA_ROWS_PER_STEP = 8


def _natten_bias_table(rpb):
    o = np.arange(NA_KH)[:, None]
    j = np.arange(NA_KH)[None, :]
    dr = j - o + (NA_KH - 1)
    q = np.arange(GRID_W)[:, None]
    kc = np.arange(GRID_W)[None, :]
    start = np.clip(q - NA_KW // 2, 0, GRID_W - NA_KW)
    inside = (kc >= start) & (kc < start + NA_KW)
    dc = np.clip(kc - q + (NA_KW - 1), 0, 2 * NA_KW - 2)
    t = rpb[:, dr[:, :, None, None], dc[None, None, :, :]]
    t = jnp.where(jnp.asarray(inside)[None, None, None], t, NEG_BIG)
    t = jnp.transpose(t, (0, 1, 3, 2, 4))
    return t.reshape(rpb.shape[0], NA_KH, GRID_W, NA_KH * GRID_W).astype(F32)


def _natten_kernel(q_ref, k_ref, v_ref, gd_ref, t_ref, o_ref, *, rows):
    rg = pl.program_id(2)
    hd = NA_HEAD_DIM
    lane = lax.broadcasted_iota(jnp.int32, (GRID_W, 2 * hd), 1)
    first = lane < hd
    scale = hd ** -0.5
    nk = NA_KH * GRID_W

    def body(rr, carry):
        r = rg * NA_ROWS_PER_STEP + rr
        rs = jnp.clip(r - NA_KH // 2, 0, rows - NA_KH)
        off = r - rs
        k0 = pl.multiple_of(rs * GRID_W, GRID_W)
        q0 = pl.multiple_of(rr * GRID_W, GRID_W)
        kb = k_ref[pl.ds(k0, nk), :].astype(BF16)
        vb = v_ref[pl.ds(k0, nk), :].astype(BF16)
        q2 = q_ref[pl.ds(q0, GRID_W), :] * scale
        outs = []
        for hh in range(2):
            qm = jnp.where(first if hh == 0 else ~first, q2, 0.0).astype(BF16)
            s = lax.dot_general(qm, kb, (((1,), (1,)), ((), ())), preferred_element_type=F32)
            s = s + t_ref[hh, off]
            p = jnp.exp(s - jnp.max(s, axis=-1, keepdims=True))
            den = jnp.sum(p, axis=-1, keepdims=True)
            outs.append(jnp.dot(p.astype(BF16), vb, preferred_element_type=F32) / den)
        att = jnp.where(first, outs[0], outs[1])
        o_ref[pl.ds(q0, GRID_W), :] = att * _silu(gd_ref[pl.ds(q0, GRID_W), :])
        return carry

    lax.fori_loop(0, NA_ROWS_PER_STEP, body, 0)


def natten_branch(proj, rpb, bsz, seq, c, col0):
    rows = seq // GRID_W
    assert rows >= NA_KH and rows % NA_ROWS_PER_STEP == 0
    table = _natten_bias_table(rpb)
    pw = 2 * NA_HEAD_DIM
    npair = c // pw
    off = col0 // pw
    tq = NA_ROWS_PER_STEP * GRID_W
    nrg = rows // NA_ROWS_PER_STEP
    qblk = lambda o: pl.BlockSpec((tq, pw), lambda b, hp, rg: (b * nrg + rg, off + o * npair + hp))
    kvblk = lambda o: pl.BlockSpec((seq, pw), lambda b, hp, rg: (b, off + o * npair + hp))
    return pl.pallas_call(
        functools.partial(_natten_kernel, rows=rows),
        grid=(bsz, npair, nrg),
        in_specs=[qblk(0), kvblk(1), kvblk(2), qblk(3),
                  pl.BlockSpec((2, NA_KH, GRID_W, NA_KH * GRID_W), lambda b, hp, rg: (hp, 0, 0, 0))],
        out_specs=pl.BlockSpec((tq, pw), lambda b, hp, rg: (b * nrg + rg, hp)),
        out_shape=jax.ShapeDtypeStruct((bsz * seq, c), F32),
        compiler_params=_cparams("parallel", "parallel", "arbitrary"),
        name="natten_branch",
    )(proj, proj, proj, proj, table)


def even_layer_mix(h, norm_g, w_in, conv_w, conv_b, hy_w0, hy_b0, hy_w1, hy_b1, hy_w2, hy_b2, hy_wout,
                   hy_freq, hy_d, gm_norm_g, gm_ws, gm_bs, bsz, seq):
    c = h.shape[1]
    proj = norm_matmul(h, norm_g, w_in.astype(BF16))
    z, xg = hyena_prep(proj, conv_w, conv_b, bsz, seq, c)
    kc_raw, ss = hyena_filter(seq, c, hy_w0, hy_b0, hy_w1, hy_b1, hy_w2, hy_b2, hy_wout, hy_freq)
    ya = hyena_long_conv(z, xg, hy_d, kc_raw, ss, bsz, seq, c)
    yb = gmlp_branch(proj, gm_norm_g, gm_ws, gm_bs, c, 4 * c)
    return ya, yb


def odd_layer_mix(h, norm_g, w_in, pool_w, pool_b, pool_scale, rpb, bsz, seq):
    c = h.shape[1]
    proj = norm_matmul(h, norm_g, w_in.astype(BF16))
    yc = pool_branch(proj, pool_w, pool_b, pool_scale, seq, c)
    yd = natten_branch(proj, rpb, bsz, seq, c, 2 * c)
    return yc, yd


def kernel(x, p, norm_g, final_g, ev_w_in, ev_conv_w, ev_conv_b, hy_w0, hy_b0, hy_w1, hy_b1, hy_w2, hy_b2, hy_wout, hy_freq, hy_d, gm_norm_g, gm_ws, gm_bs, ev_w_out, od_w_in, pool_w, pool_b, pool_scale, na_rpb, od_w_out, ple_up, ple_gate_w, ple_g):
    bsz, seq, d = x.shape
    depth = p.shape[0]
    h = x.reshape(bsz * seq, d)
    for i in range(depth):
        j = i // 2
        if i % 2 == 0:
            ya, yb = even_layer_mix(h, norm_g[i], ev_w_in[j], ev_conv_w[j], ev_conv_b[j], hy_w0[j], hy_b0[j],
                                    hy_w1[j], hy_b1[j], hy_w2[j], hy_b2[j], hy_wout[j], hy_freq[j], hy_d[j],
                                    gm_norm_g[j], gm_ws[j], gm_bs[j], bsz, seq)
            w_out = ev_w_out[j]
        else:
            ya, yb = odd_layer_mix(h, norm_g[i], od_w_in[j], pool_w[j], pool_b[j], pool_scale[j], na_rpb[j],
                                   bsz, seq)
            w_out = od_w_out[j]
        h = out_ple(ya, yb, h, p[i].reshape(bsz * seq, -1), w_out, ple_g[i], ple_gate_w[i], ple_up[i],
                    final_g, final=(i == depth - 1))
    return h.reshape(bsz, seq, d)
```

```python
import functools
import math

import jax
import jax.numpy as jnp
import numpy as np
from jax import lax
from jax.experimental import pallas as pl
from jax.experimental.pallas import tpu as pltpu

F32 = jnp.float32
BF16 = jnp.bfloat16

EPS = 1e-6
GRID_W = 64
PLE_DIM = 256
HY_BANDS = 16
HY_FAST_DECAY = 0.3
HY_SLOW_DECAY = 1.5
HY_TARGET = 1e-2
GM_GROUPS = 8
CHUNK = 128
POOL_WINDOWS = (2, 4, 8, 16)
NA_HEADS = 16
NA_HEAD_DIM = 64
NA_KH = 8
NA_KW = 16
FFT_N2 = 64
NEG_BIG = -1e30
VMEM_LIMIT = 56 * 1024 * 1024


def _cparams(*sem):
    return pltpu.CompilerParams(dimension_semantics=sem, vmem_limit_bytes=VMEM_LIMIT)


def _silu(x):
    return x * jax.nn.sigmoid(x)


def _rms(x, g):
    return x * lax.rsqrt(jnp.mean(x * x, axis=-1, keepdims=True) + EPS) * g


def _norm_matmul_kernel(h_ref, g_ref, w_ref, o_ref, hn_ref):
    @pl.when(pl.program_id(1) == 0)
    def _():
        hn_ref[...] = _rms(h_ref[...], g_ref[...]).astype(BF16)

    o_ref[...] = jnp.dot(hn_ref[...], w_ref[...], preferred_element_type=F32)


def norm_matmul(h, g, w):
    m, d = h.shape
    n = w.shape[1]
    tm = min(1024, m)
    tn = min(512, n)
    return pl.pallas_call(
        _norm_matmul_kernel,
        grid=(m // tm, n // tn),
        in_specs=[pl.BlockSpec((tm, d), lambda i, j: (i, 0)),
                  pl.BlockSpec((1, d), lambda i, j: (0, 0)),
                  pl.BlockSpec((d, tn), lambda i, j: (0, j))],
        out_specs=pl.BlockSpec((tm, tn), lambda i, j: (i, j)),
        out_shape=jax.ShapeDtypeStruct((m, n), F32),
        scratch_shapes=[pltpu.VMEM((tm, d), BF16)],
        compiler_params=_cparams("parallel", "arbitrary"),
        name="norm_matmul",
    )(h, g.reshape(1, d), w)


def _short_conv(x, w, b):
    n = x.shape[0]
    row = lax.broadcasted_iota(jnp.int32, x.shape, 0)
    xm = jnp.where(row == 0, 0.0, pltpu.roll(x, 1, 0))
    xp = jnp.where(row == n - 1, 0.0, pltpu.roll(x, n - 1, 0))
    return xm * w[0:1] + x * w[1:2] + xp * w[2:3] + b


def _hyena_prep_kernel(x0_ref, x1_ref, v_ref, ga_ref, w0_ref, w1_ref, w2_ref,
                       b0_ref, b1_ref, b2_ref, z_ref, xg_ref):
    x1 = _short_conv(x1_ref[...], w1_ref[...], b1_ref[...])
    v = _short_conv(v_ref[...], w2_ref[...], b2_ref[...])
    z_ref[...] = x1 * v
    x0 = _short_conv(x0_ref[...], w0_ref[...], b0_ref[...])
    xg_ref[...] = x0 * _silu(ga_ref[...])


def hyena_prep(proj, conv_w, conv_b, bsz, seq, c):
    ct = min(128, c)
    nc = c // ct
    blk = lambda off: pl.BlockSpec((seq, ct), lambda b, j: (b, j + off * nc))
    wblk = lambda off: pl.BlockSpec((3, ct), lambda b, j: (0, j + off * nc))
    bblk = lambda off: pl.BlockSpec((1, ct), lambda b, j: (0, j + off * nc))
    out = jax.ShapeDtypeStruct((bsz * seq, c), F32)
    return pl.pallas_call(
        _hyena_prep_kernel,
        grid=(bsz, nc),
        in_specs=[blk(0), blk(1), blk(2), blk(3), wblk(0), wblk(1), wblk(2), bblk(0), bblk(1), bblk(2)],
        out_specs=[pl.BlockSpec((seq, ct), lambda b, j: (b, j))] * 2,
        out_shape=[out, out],
        compiler_params=_cparams("parallel", "parallel"),
        name="hyena_prep",
    )(proj, proj, proj, proj, conv_w, conv_w, conv_w,
      conv_b.reshape(1, -1), conv_b.reshape(1, -1), conv_b.reshape(1, -1))


def _positional_features(seq):
    f32 = np.float32
    t = np.linspace(0.0, 1.0, seq, dtype=f32)[:, None]
    ang = (f32(2.0 * math.pi) * np.arange(seq, dtype=f32)[:, None] / f32(seq)).astype(f32)
    bands = np.linspace(1e-4, HY_BANDS - 1, HY_BANDS, dtype=f32)[None, :]
    ba = (bands * ang).astype(f32)
    feats = np.concatenate([t, np.cos(ba), -np.sin(ba)], axis=-1).astype(f32)
    feats2 = np.concatenate([feats, feats[:1], feats[1:][::-1]], axis=0)
    pad = np.zeros((2 * seq, 128 - feats2.shape[1]), f32)
    return np.concatenate([feats2, pad], axis=1)


def _filter_kernel(feat_ref, w0_ref, b0_ref, w1_ref, b1_ref, w2_ref, b2_ref, wo_ref, fr_ref, dl_ref,
                   kc_ref, ss_ref, *, seq, tr):
    hp = lax.Precision.HIGHEST
    i = pl.program_id(0)
    fr = fr_ref[...]
    h = jnp.sin(fr * (jnp.dot(feat_ref[...], w0_ref[...], precision=hp, preferred_element_type=F32) + b0_ref[...]))
    h = jnp.sin(fr * (jnp.dot(h, w1_ref[...], precision=hp, preferred_element_type=F32) + b1_ref[...]))
    h = jnp.sin(fr * (jnp.dot(h, w2_ref[...], precision=hp, preferred_element_type=F32) + b2_ref[...]))
    k2 = jnp.dot(h, wo_ref[...], precision=hp, preferred_element_type=F32)
    c = kc_ref.shape[1]
    s = i * tr + lax.broadcasted_iota(jnp.int32, (tr, 1), 0)
    k = jnp.where(s < seq, k2[:, :c], k2[:, c:])
    lag = jnp.where(s < seq, s, 2 * seq - s).astype(F32)
    t = lag * (1.0 / (seq - 1))
    k = jnp.where(s == seq, 0.0, k * jnp.exp(-t * dl_ref[...]))
    kc_ref[...] = k

    @pl.when(i == 0)
    def _():
        ss_ref[...] = jnp.zeros_like(ss_ref)

    ss_ref[...] += jnp.sum(k * k, axis=0, keepdims=True)


def hyena_filter(seq, c, w0, b0, w1, b1, w2, b2, wout, freq):
    feats = jnp.asarray(_positional_features(seq))
    hid = w0.shape[1]
    w0p = jnp.zeros((128, hid), F32).at[:w0.shape[0]].set(w0)
    max_decay = math.log(HY_TARGET) / HY_FAST_DECAY
    min_decay = math.log(HY_TARGET) / HY_SLOW_DECAY
    deltas = jnp.asarray(np.abs(np.linspace(min_decay, max_decay, c, dtype=np.float32)))[None, :]
    tr = min(512, 2 * seq)
    full = lambda a: pl.BlockSpec(a.shape, lambda i: (0,) * a.ndim)
    row = lambda a: a.reshape(1, -1)
    args = (feats, w0p, row(b0), w1, row(b1), w2, row(b2), wout, row(freq), deltas)
    return pl.pallas_call(
        functools.partial(_filter_kernel, seq=seq, tr=tr),
        grid=(2 * seq // tr,),
        in_specs=[pl.BlockSpec((tr, 128), lambda i: (i, 0))] + [full(a) for a in args[1:]],
        out_specs=[pl.BlockSpec((tr, c), lambda i: (i, 0)), pl.BlockSpec((1, c), lambda i: (0, 0))],
        out_shape=[jax.ShapeDtypeStruct((2 * seq, c), F32), jax.ShapeDtypeStruct((1, c), F32)],
        compiler_params=_cparams("arbitrary"),
        name="hyena_filter",
    )(*args)


def _dft_tables(seq):
    n = 2 * seq
    n2 = FFT_N2
    n1 = n // n2
    f1 = np.arange(n1)[:, None]
    s1 = np.arange(n1)[None, :]
    a = 2.0 * np.pi * (f1 * s1 % n1) / n1
    fa = np.empty((2 * n1, n1), np.float64)
    fa[0::2] = np.cos(a)
    fa[1::2] = -np.sin(a)
    ia = np.empty((n1 // 2, 2 * n1), np.float64)
    at = a.T[: n1 // 2]
    ia[:, 0::2] = np.cos(at) / n
    ia[:, 1::2] = -np.sin(at) / n
    f1b = np.arange(n1)[:, None, None]
    f2 = np.arange(n2)[None, :, None]
    s2 = np.arange(n2)[None, None, :]
    ph = 2.0 * np.pi * ((s2 * f2 * n1 + s2 * f1b) % n) / n
    gr, gi = np.cos(ph), -np.sin(ph)
    gf = np.concatenate([np.concatenate([gr, -gi], axis=2),
                         np.concatenate([gi, gr], axis=2)], axis=1)
    er, ei = np.transpose(gr, (0, 2, 1)), -np.transpose(gi, (0, 2, 1))
    gb = np.concatenate([np.concatenate([er, -ei], axis=2),
                         np.concatenate([ei, er], axis=2)], axis=1)
    return fa.astype(np.float32), ia.astype(np.float32), gf.astype(np.float32), gb.astype(np.float32)


def _lmul_kernel(w_ref, x_ref, o_ref, *, precise):
    if precise:
        o_ref[...] = jnp.dot(w_ref[...], x_ref[...], precision=lax.Precision.HIGHEST,
                             preferred_element_type=F32)
    else:
        o_ref[...] = jnp.dot(w_ref[...], x_ref[...].astype(BF16),
                             preferred_element_type=F32).astype(o_ref.dtype)


def lmul(w, x, precise=False):
    bsz, k, n = x.shape
    m = w.shape[0]
    nt = min(2048, n)
    return pl.pallas_call(
        functools.partial(_lmul_kernel, precise=precise),
        grid=(bsz, n // nt),
        in_specs=[pl.BlockSpec((m, k), lambda b, j: (0, 0)),
                  pl.BlockSpec((None, k, nt), lambda b, j: (b, 0, j))],
        out_specs=pl.BlockSpec((None, m, nt), lambda b, j: (b, 0, j)),
        out_shape=jax.ShapeDtypeStruct((bsz, m, n), F32 if precise else BF16),
        compiler_params=_cparams("parallel", "parallel"),
        name="dft_stage_a",
    )(w, x)


def _bmm(g, x, precise):
    dn = (((2,), (1,)), ((0,), (0,)))
    if precise:
        return lax.dot_general(g, x, dn, precision=lax.Precision.HIGHEST, preferred_element_type=F32)
    return lax.dot_general(g, x.astype(BF16), dn, preferred_element_type=F32)


def _filter_spectrum_kernel(g_ref, x_ref, ss_ref, o_ref):
    o_ref[...] = _bmm(g_ref[...], x_ref[...], True) * lax.rsqrt(ss_ref[...] + EPS)


def filter_spectrum(gf, x, ss):
    n1, k2, c = x.shape
    ft = min(8, n1)
    ct = min(256, c)
    return pl.pallas_call(
        _filter_spectrum_kernel,
        grid=(n1 // ft, c // ct),
        in_specs=[pl.BlockSpec((ft, k2, k2), lambda i, j: (i, 0, 0)),
                  pl.BlockSpec((ft, k2, ct), lambda i, j: (i, 0, j)),
                  pl.BlockSpec((1, ct), lambda i, j: (0, j))],
        out_specs=pl.BlockSpec((ft, k2, ct), lambda i, j: (i, 0, j)),
        out_shape=jax.ShapeDtypeStruct((n1, k2, c), F32),
        compiler_params=_cparams("parallel", "parallel"),
        name="filter_spectrum",
    )(gf, x, ss)


def _spectral_kernel(gf_ref, gb_ref, h_ref, x_ref, o_ref):
    n2 = h_ref.shape[1] // 2
    spec = _bmm(gf_ref[...], x_ref[...], False)
    sr, si = spec[:, :n2], spec[:, n2:]
    hr, hi = h_ref[:, :n2], h_ref[:, n2:]
    prod = jnp.concatenate([sr * hr - si * hi, sr * hi + si * hr], axis=1)
    o_ref[...] = _bmm(gb_ref[...], prod, False).astype(o_ref.dtype)


def spectral_multiply(gf, gb, hspec, x):
    bsz, n1, k2, c = x.shape
    ft = min(8, n1)
    ct = min(512, c)
    gspec = pl.BlockSpec((ft, k2, k2), lambda i, j, b: (i, 0, 0))
    return pl.pallas_call(
        _spectral_kernel,
        grid=(n1 // ft, c // ct, bsz),
        in_specs=[gspec, gspec,
                  pl.BlockSpec((ft, k2, ct), lambda i, j, b: (i, 0, j)),
                  pl.BlockSpec((None, ft, k2, ct), lambda i, j, b: (b, i, 0, j))],
        out_specs=pl.BlockSpec((None, ft, k2, ct), lambda i, j, b: (b, i, 0, j)),
        out_shape=jax.ShapeDtypeStruct(x.shape, BF16),
        compiler_params=_cparams("parallel", "parallel", "arbitrary"),
        name="spectral_multiply",
    )(gf, gb, hspec, x)


def _inverse_a_kernel(w_ref, x_ref, z_ref, xg_ref, d_ref, o_ref):
    y = jnp.dot(w_ref[...], x_ref[...].astype(BF16), preferred_element_type=F32)
    o_ref[...] = (y + z_ref[...] * d_ref[...]) * xg_ref[...]


def inverse_stage_a(w, x, z, xg, d):
    bsz, k, n = x.shape
    m = w.shape[0]
    c = d.shape[0]
    nt = min(2048, n)
    dt = jnp.tile(d.reshape(1, c), (1, nt // c)) if nt >= c else d.reshape(1, c)
    dspec = (pl.BlockSpec((1, nt), lambda b, j: (0, 0)) if nt >= c
             else pl.BlockSpec((1, nt), lambda b, j: (0, j % (c // nt))))
    blk = pl.BlockSpec((None, m, nt), lambda b, j: (b, 0, j))
    return pl.pallas_call(
        _inverse_a_kernel,
        grid=(bsz, n // nt),
        in_specs=[pl.BlockSpec((m, k), lambda b, j: (0, 0)),
                  pl.BlockSpec((None, k, nt), lambda b, j: (b, 0, j)),
                  blk, blk, dspec],
        out_specs=blk,
        out_shape=jax.ShapeDtypeStruct((bsz, m, n), F32),
        compiler_params=_cparams("parallel", "parallel"),
        name="inverse_stage_a",
    )(w, x, z, xg, dt)


def hyena_long_conv(z, xg, d, kc_raw, ss, bsz, seq, c):
    n2 = FFT_N2
    n1 = 2 * seq // n2
    fa, ia, gf, gb = _dft_tables(seq)
    kspec = lmul(jnp.asarray(fa), kc_raw.reshape(1, n1, n2 * c), precise=True)
    hspec = filter_spectrum(jnp.asarray(gf), kspec.reshape(n1, 2 * n2, c), ss)
    zs = z.reshape(bsz, n1 // 2, n2 * c)
    za = lmul(jnp.asarray(fa[:, : n1 // 2], dtype=BF16), zs)
    zb = spectral_multiply(jnp.asarray(gf, dtype=BF16), jnp.asarray(gb, dtype=BF16), hspec,
                           za.reshape(bsz, n1, 2 * n2, c))
    ya = inverse_stage_a(jnp.asarray(ia, dtype=BF16), zb.reshape(bsz, 2 * n1, n2 * c),
                         zs, xg.reshape(bsz, n1 // 2, n2 * c), d)
    return ya.reshape(bsz * seq, c)


def _gmlp_kernel(u_ref, v_ref, gb_ref, ng_ref, ws_ref, bias_ref, o_ref):
    vn = _rms(v_ref[...], ng_ref[...]).astype(BF16)
    tr, c = vn.shape
    gc = c // GM_GROUPS
    for n in range(tr // CHUNK):
        rows = slice(n * CHUNK, (n + 1) * CHUNK)
        for g in range(GM_GROUPS):
            cols = slice(g * gc, (g + 1) * gc)
            s = jnp.dot(ws_ref[g], vn[rows, cols], preferred_element_type=F32) + bias_ref[:, cols]
            o_ref[rows, cols] = u_ref[rows, cols] * s * _silu(gb_ref[rows, cols])


def gmlp_branch(proj, norm_g, ws, bs, c, col0):
    m = proj.shape[0]
    tr = min(512, m)
    off = col0 // c
    bias = jnp.repeat(bs.T, c // GM_GROUPS, axis=1)
    blk = lambda o: pl.BlockSpec((tr, c), lambda i: (i, off + o))
    return pl.pallas_call(
        _gmlp_kernel,
        grid=(m // tr,),
        in_specs=[blk(0), blk(1), blk(2),
                  pl.BlockSpec((1, c), lambda i: (0, 0)),
                  pl.BlockSpec(ws.shape, lambda i: (0, 0, 0)),
                  pl.BlockSpec(bias.shape, lambda i: (0, 0))],
        out_specs=pl.BlockSpec((tr, c), lambda i: (i, 0)),
        out_shape=jax.ShapeDtypeStruct((m, c), F32),
        compiler_params=_cparams("parallel"),
        name="gmlp_branch",
    )(proj, proj, proj, norm_g.reshape(1, c), ws.astype(BF16), bias)


def _out_ple_kernel(ya_ref, yb_ref, h_ref, p_ref, wa_ref, wb_ref, pg_ref, gw_ref, up_ref, fg_ref, o_ref,
                    *, final):
    mix = (jnp.dot(ya_ref[...].astype(BF16), wa_ref[...], preferred_element_type=F32)
           + jnp.dot(yb_ref[...].astype(BF16), wb_ref[...], preferred_element_type=F32))
    h1 = h_ref[...] + mix
    r = _rms(h1, pg_ref[...]).astype(BF16)
    gate = jax.nn.sigmoid(jnp.dot(r, gw_ref[...], preferred_element_type=F32))
    up = jnp.dot(p_ref[...].astype(BF16), up_ref[...], preferred_element_type=F32)
    h2 = h1 + up * gate
    if final:
        h2 = _rms(h2, fg_ref[...])
    o_ref[...] = h2


def out_ple(ya, yb, h, p, w_out, ple_g, gate_w, ple_up, final_g, final):
    m, d = h.shape
    ca = ya.shape[1]
    tm = min(512, m)
    rowblk = lambda a: pl.BlockSpec((tm, a.shape[1]), lambda i: (i, 0))
    full = lambda a: pl.BlockSpec(a.shape, lambda i: (0, 0))
    wa = w_out[:ca].astype(BF16)
    wb = w_out[ca:].astype(BF16)
    args = (ya, yb, h, p, wa, wb, ple_g.reshape(1, d), gate_w.astype(BF16), ple_up.astype(BF16),
            final_g.reshape(1, d))
    return pl.pallas_call(
        functools.partial(_out_ple_kernel, final=final),
        grid=(m // tm,),
        in_specs=[rowblk(a) for a in args[:4]] + [full(a) for a in args[4:]],
        out_specs=pl.BlockSpec((tm, d), lambda i: (i, 0)),
        out_shape=jax.ShapeDtypeStruct((m, d), F32),
        compiler_params=_cparams("parallel"),
        name="out_ple",
    )(*args)


POOL_HALO = 8


def _pool_kernel(x_ref, prev_ref, next_ref, gc_ref, w_ref, b_ref, sc_ref, o_ref, pad_ref, *, seq):
    tr, c = x_ref.shape
    i = pl.program_id(0)
    t0 = (i * tr) % seq
    zeros = jnp.zeros((POOL_HALO, c), F32)
    pad_ref[0:POOL_HALO] = jnp.where(t0 == 0, zeros, prev_ref[...])
    pad_ref[POOL_HALO:POOL_HALO + tr] = x_ref[...]
    pad_ref[POOL_HALO + tr:] = jnp.where(t0 + tr == seq, zeros, next_ref[...])
    t = t0 + lax.broadcasted_iota(jnp.int32, (tr, 1), 0)
    gcw = c // len(POOL_WINDOWS)
    for g, w in enumerate(POOL_WINDOWS):
        cols = slice(g * gcw, (g + 1) * gcw)
        acc = pad_ref[POOL_HALO - w // 2:POOL_HALO - w // 2 + tr, cols]
        for o in range(-w // 2 + 1, w // 2):
            acc = acc + pad_ref[POOL_HALO + o:POOL_HALO + o + tr, cols]
        cnt = (jnp.minimum(t + w // 2, seq) - jnp.maximum(t - w // 2, 0)).astype(F32)
        dlt = acc / cnt - x_ref[:, cols]
        y = jnp.dot(dlt.astype(BF16), w_ref[g], preferred_element_type=F32) + b_ref[:, cols]
        o_ref[:, cols] = y * sc_ref[:, cols] * _silu(gc_ref[:, cols])


def pool_branch(proj, pool_w, pool_b, pool_scale, seq, c):
    m = proj.shape[0]
    tr = min(512, seq)
    hb = tr // POOL_HALO
    nblk = m // POOL_HALO
    return pl.pallas_call(
        functools.partial(_pool_kernel, seq=seq),
        grid=(m // tr,),
        in_specs=[pl.BlockSpec((tr, c), lambda i: (i, 0)),
                  pl.BlockSpec((POOL_HALO, c), lambda i: (jnp.maximum(i * hb - 1, 0), 0)),
                  pl.BlockSpec((POOL_HALO, c), lambda i: (jnp.minimum((i + 1) * hb, nblk - 1), 0)),
                  pl.BlockSpec((tr, c), lambda i: (i, 1)),
                  pl.BlockSpec(pool_w.shape, lambda i: (0, 0, 0)),
                  pl.BlockSpec((1, c), lambda i: (0, 0)),
                  pl.BlockSpec((1, c), lambda i: (0, 0))],
        out_specs=pl.BlockSpec((tr, c), lambda i: (i, 0)),
        out_shape=jax.ShapeDtypeStruct((m, c), F32),
        scratch_shapes=[pltpu.VMEM((tr + 2 * POOL_HALO, c), F32)],
        compiler_params=_cparams("parallel"),
        name="pool_branch",
    )(proj, proj, proj, proj, pool_w.astype(BF16), pool_b.reshape(1, c), pool_scale.reshape(1, c))


NA_ROWS_PER_STEP = 16


def _natten_bias_table(rpb):
    nh = rpb.shape[0]
    q = np.arange(GRID_W)[:, None]
    kc = np.arange(GRID_W)[None, :]
    start = np.clip(q - NA_KW // 2, 0, GRID_W - NA_KW)
    inside = (kc >= start) & (kc < start + NA_KW)
    dc = kc - q + (NA_KW - 1)
    onehot = ((dc[None] == np.arange(2 * NA_KW - 1)[:, None, None]) & inside[None]).astype(np.float32)
    m = jnp.einsum('hrd,dqk->hrqk', rpb.astype(F32), jnp.asarray(onehot), precision=lax.Precision.HIGHEST)
    m = m + jnp.asarray(np.where(inside, 0.0, NEG_BIG).astype(np.float32))
    t = jnp.stack([m[:, NA_KH - 1 - o:2 * NA_KH - 1 - o] for o in range(NA_KH)], axis=1)
    t = t.reshape(nh // 2, 2, NA_KH, NA_KH, GRID_W, GRID_W)
    t = jnp.transpose(t, (0, 2, 1, 4, 3, 5))
    return t.reshape(nh // 2, NA_KH, 2 * GRID_W, NA_KH * GRID_W)


def _natten_kernel(q_ref, k_ref, v_ref, gd_ref, t_ref, o_ref, kb_ref, vb_ref, *, rows):
    rg = pl.program_id(2)
    hd = NA_HEAD_DIM
    lane = lax.broadcasted_iota(jnp.int32, (GRID_W, 2 * hd), 1)
    first = lane < hd
    scale = hd ** -0.5
    nk = NA_KH * GRID_W

    @pl.when(rg == 0)
    def _():
        kb_ref[...] = k_ref[...].astype(BF16)
        vb_ref[...] = v_ref[...].astype(BF16)

    for rr in range(NA_ROWS_PER_STEP):
        r = rg * NA_ROWS_PER_STEP + rr
        rs = jnp.clip(r - NA_KH // 2, 0, rows - NA_KH)
        k0 = pl.multiple_of(rs * GRID_W, GRID_W)
        qrows = slice(rr * GRID_W, (rr + 1) * GRID_W)
        q2 = q_ref[qrows, :] * scale
        qs = jnp.concatenate([jnp.where(first, q2, 0.0), jnp.where(first, 0.0, q2)], axis=0).astype(BF16)
        s = lax.dot_general(qs, kb_ref[pl.ds(k0, nk), :], (((1,), (1,)), ((), ())),
                            preferred_element_type=F32)
        s = s + t_ref[r - rs]
        p = jnp.exp(s - jnp.max(s, axis=-1, keepdims=True))
        den = jnp.sum(p, axis=-1, keepdims=True)
        o2 = jnp.dot(p.astype(BF16), vb_ref[pl.ds(k0, nk), :], preferred_element_type=F32) / den
        att = jnp.where(first, o2[:GRID_W], o2[GRID_W:])
        o_ref[qrows, :] = att * _silu(gd_ref[qrows, :])


def natten_branch(proj, rpb, bsz, seq, c, col0):
    rows = seq // GRID_W
    assert rows >= NA_KH and rows % NA_ROWS_PER_STEP == 0
    table = _natten_bias_table(rpb)
    pw = 2 * NA_HEAD_DIM
    npair = c // pw
    off = col0 // pw
    tq = NA_ROWS_PER_STEP * GRID_W
    nrg = rows // NA_ROWS_PER_STEP
    qblk = lambda o: pl.BlockSpec((tq, pw), lambda b, hp, rg: (b * nrg + rg, off + o * npair + hp))
    kvblk = lambda o: pl.BlockSpec((seq, pw), lambda b, hp, rg: (b, off + o * npair + hp))
    return pl.pallas_call(
        functools.partial(_natten_kernel, rows=rows),
        grid=(bsz, npair, nrg),
        in_specs=[qblk(0), kvblk(1), kvblk(2), qblk(3),
                  pl.BlockSpec((None, NA_KH, 2 * GRID_W, NA_KH * GRID_W), lambda b, hp, rg: (hp, 0, 0, 0))],
        out_specs=pl.BlockSpec((tq, pw), lambda b, hp, rg: (b * nrg + rg, hp)),
        out_shape=jax.ShapeDtypeStruct((bsz * seq, c), F32),
        scratch_shapes=[pltpu.VMEM((seq, pw), BF16), pltpu.VMEM((seq, pw), BF16)],
        compiler_params=_cparams("parallel", "parallel", "arbitrary"),
        name="natten_branch",
    )(proj, proj, proj, proj, table)


def even_layer_mix(h, norm_g, w_in, conv_w, conv_b, hy_w0, hy_b0, hy_w1, hy_b1, hy_w2, hy_b2, hy_wout,
                   hy_freq, hy_d, gm_norm_g, gm_ws, gm_bs, bsz, seq):
    c = h.shape[1]
    proj = norm_matmul(h, norm_g, w_in.astype(BF16))
    z, xg = hyena_prep(proj, conv_w, conv_b, bsz, seq, c)
    kc_raw, ss = hyena_filter(seq, c, hy_w0, hy_b0, hy_w1, hy_b1, hy_w2, hy_b2, hy_wout, hy_freq)
    ya = hyena_long_conv(z, xg, hy_d, kc_raw, ss, bsz, seq, c)
    yb = gmlp_branch(proj, gm_norm_g, gm_ws, gm_bs, c, 4 * c)
    return ya, yb


def odd_layer_mix(h, norm_g, w_in, pool_w, pool_b, pool_scale, rpb, bsz, seq):
    c = h.shape[1]
    proj = norm_matmul(h, norm_g, w_in.astype(BF16))
    yc = pool_branch(proj, pool_w, pool_b, pool_scale, seq, c)
    yd = natten_branch(proj, rpb, bsz, seq, c, 2 * c)
    return yc, yd


def kernel(x, p, norm_g, final_g, ev_w_in, ev_conv_w, ev_conv_b, hy_w0, hy_b0, hy_w1, hy_b1, hy_w2, hy_b2, hy_wout, hy_freq, hy_d, gm_norm_g, gm_ws, gm_bs, ev_w_out, od_w_in, pool_w, pool_b, pool_scale, na_rpb, od_w_out, ple_up, ple_gate_w, ple_g):
    bsz, seq, d = x.shape
    depth = p.shape[0]
    h = x.reshape(bsz * seq, d)
    for i in range(depth):
        j = i // 2
        if i % 2 == 0:
            ya, yb = even_layer_mix(h, norm_g[i], ev_w_in[j], ev_conv_w[j], ev_conv_b[j], hy_w0[j], hy_b0[j],
                                    hy_w1[j], hy_b1[j], hy_w2[j], hy_b2[j], hy_wout[j], hy_freq[j], hy_d[j],
                                    gm_norm_g[j], gm_ws[j], gm_bs[j], bsz, seq)
            w_out = ev_w_out[j]
        else:
            ya, yb = odd_layer_mix(h, norm_g[i], od_w_in[j], pool_w[j], pool_b[j], pool_scale[j], na_rpb[j],
                                   bsz, seq)
            w_out = od_w_out[j]
        h = out_ple(ya, yb, h, p[i].reshape(bsz * seq, -1), w_out, ple_g[i], ple_gate_w[i], ple_up[i],
                    final_g, final=(i == depth - 1))
    return h.reshape(bsz, seq, d)
```

```python
import functools
import math

import jax
import jax.numpy as jnp
import numpy as np
from jax import lax
from jax.experimental import pallas as pl
from jax.experimental.pallas import tpu as pltpu

F32 = jnp.float32
BF16 = jnp.bfloat16

EPS = 1e-6
GRID_W = 64
PLE_DIM = 256
HY_BANDS = 16
HY_FAST_DECAY = 0.3
HY_SLOW_DECAY = 1.5
HY_TARGET = 1e-2
GM_GROUPS = 8
CHUNK = 128
POOL_WINDOWS = (2, 4, 8, 16)
NA_HEADS = 16
NA_HEAD_DIM = 64
NA_KH = 8
NA_KW = 16
FFT_N2 = 64
NEG_BIG = -1e30
VMEM_LIMIT = 56 * 1024 * 1024


def _cparams(*sem):
    return pltpu.CompilerParams(dimension_semantics=sem, vmem_limit_bytes=VMEM_LIMIT)


def _silu(x):
    return x * jax.nn.sigmoid(x)


def _rms(x, g):
    return x * lax.rsqrt(jnp.mean(x * x, axis=-1, keepdims=True) + EPS) * g


def _resident(shape):
    return pl.BlockSpec(shape, lambda *_: (0,) * len(shape), pipeline_mode=pl.Buffered(1))


def _norm_matmul_kernel(h_ref, g_ref, w_ref, o_ref, *, tn):
    hn = _rms(h_ref[...], g_ref[...]).astype(BF16)
    for j in range(o_ref.shape[1] // tn):
        cols = slice(j * tn, (j + 1) * tn)
        o_ref[:, cols] = jnp.dot(hn, w_ref[:, cols], preferred_element_type=F32).astype(o_ref.dtype)


def norm_matmul(h, g, w):
    m, d = h.shape
    n = w.shape[1]
    tm = min(512, m)
    tn = min(512, n)
    return pl.pallas_call(
        functools.partial(_norm_matmul_kernel, tn=tn),
        grid=(m // tm,),
        in_specs=[pl.BlockSpec((tm, d), lambda i: (i, 0)),
                  _resident((1, d)),
                  _resident((d, n))],
        out_specs=pl.BlockSpec((tm, n), lambda i: (i, 0)),
        out_shape=jax.ShapeDtypeStruct((m, n), BF16),
        compiler_params=_cparams("parallel"),
        name="norm_matmul",
    )(h, g.reshape(1, d), w)


def _short_conv(x, w, b):
    n = x.shape[0]
    row = lax.broadcasted_iota(jnp.int32, x.shape, 0)
    xm = jnp.where(row == 0, 0.0, pltpu.roll(x, 1, 0))
    xp = jnp.where(row == n - 1, 0.0, pltpu.roll(x, n - 1, 0))
    return xm * w[0:1] + x * w[1:2] + xp * w[2:3] + b


def _hyena_prep_kernel(x0_ref, x1_ref, v_ref, ga_ref, w0_ref, w1_ref, w2_ref,
                       b0_ref, b1_ref, b2_ref, z_ref, xg_ref):
    x1 = _short_conv(x1_ref[...].astype(F32), w1_ref[...], b1_ref[...])
    v = _short_conv(v_ref[...].astype(F32), w2_ref[...], b2_ref[...])
    z_ref[...] = x1 * v
    x0 = _short_conv(x0_ref[...].astype(F32), w0_ref[...], b0_ref[...])
    xg_ref[...] = x0 * _silu(ga_ref[...].astype(F32))


def hyena_prep(proj, conv_w, conv_b, bsz, seq, c):
    ct = min(128, c)
    nc = c // ct
    blk = lambda off: pl.BlockSpec((seq, ct), lambda b, j: (b, j + off * nc))
    wblk = lambda off: pl.BlockSpec((3, ct), lambda b, j: (0, j + off * nc))
    bblk = lambda off: pl.BlockSpec((1, ct), lambda b, j: (0, j + off * nc))
    out = jax.ShapeDtypeStruct((bsz * seq, c), F32)
    return pl.pallas_call(
        _hyena_prep_kernel,
        grid=(bsz, nc),
        in_specs=[blk(0), blk(1), blk(2), blk(3), wblk(0), wblk(1), wblk(2), bblk(0), bblk(1), bblk(2)],
        out_specs=[pl.BlockSpec((seq, ct), lambda b, j: (b, j))] * 2,
        out_shape=[out, out],
        compiler_params=_cparams("parallel", "parallel"),
        name="hyena_prep",
    )(proj, proj, proj, proj, conv_w, conv_w, conv_w,
      conv_b.reshape(1, -1), conv_b.reshape(1, -1), conv_b.reshape(1, -1))


def _positional_features(seq):
    f32 = np.float32
    t = np.linspace(0.0, 1.0, seq, dtype=f32)[:, None]
    ang = (f32(2.0 * math.pi) * np.arange(seq, dtype=f32)[:, None] / f32(seq)).astype(f32)
    bands = np.linspace(1e-4, HY_BANDS - 1, HY_BANDS, dtype=f32)[None, :]
    ba = (bands * ang).astype(f32)
    feats = np.concatenate([t, np.cos(ba), -np.sin(ba)], axis=-1).astype(f32)
    feats2 = np.concatenate([feats, feats[:1], feats[1:][::-1]], axis=0)
    pad = np.zeros((2 * seq, 128 - feats2.shape[1]), f32)
    return np.concatenate([feats2, pad], axis=1)


def _filter_kernel(feat_ref, w0_ref, b0_ref, w1_ref, b1_ref, w2_ref, b2_ref, wo_ref, fr_ref, dl_ref,
                   kc_ref, ss_ref, *, seq, tr):
    hp = lax.Precision.HIGHEST
    i = pl.program_id(0)
    fr = fr_ref[...]
    h = jnp.sin(fr * (jnp.dot(feat_ref[...], w0_ref[...], precision=hp, preferred_element_type=F32) + b0_ref[...]))
    h = jnp.sin(fr * (jnp.dot(h, w1_ref[...], precision=hp, preferred_element_type=F32) + b1_ref[...]))
    h = jnp.sin(fr * (jnp.dot(h, w2_ref[...], precision=hp, preferred_element_type=F32) + b2_ref[...]))
    k2 = jnp.dot(h, wo_ref[...], precision=hp, preferred_element_type=F32)
    c = kc_ref.shape[1]
    s = i * tr + lax.broadcasted_iota(jnp.int32, (tr, 1), 0)
    k = jnp.where(s < seq, k2[:, :c], k2[:, c:])
    lag = jnp.where(s < seq, s, 2 * seq - s).astype(F32)
    t = lag * (1.0 / (seq - 1))
    k = jnp.where(s == seq, 0.0, k * jnp.exp(-t * dl_ref[...]))
    kc_ref[...] = k

    @pl.when(i == 0)
    def _():
        ss_ref[...] = jnp.zeros_like(ss_ref)

    ss_ref[...] += jnp.sum(k * k, axis=0, keepdims=True)


def hyena_filter(seq, c, w0, b0, w1, b1, w2, b2, wout, freq):
    feats = jnp.asarray(_positional_features(seq))
    hid = w0.shape[1]
    w0p = jnp.zeros((128, hid), F32).at[:w0.shape[0]].set(w0)
    max_decay = math.log(HY_TARGET) / HY_FAST_DECAY
    min_decay = math.log(HY_TARGET) / HY_SLOW_DECAY
    deltas = jnp.asarray(np.abs(np.linspace(min_decay, max_decay, c, dtype=np.float32)))[None, :]
    tr = min(512, 2 * seq)
    full = lambda a: pl.BlockSpec(a.shape, lambda i: (0,) * a.ndim)
    row = lambda a: a.reshape(1, -1)
    args = (feats, w0p, row(b0), w1, row(b1), w2, row(b2), wout, row(freq), deltas)
    return pl.pallas_call(
        functools.partial(_filter_kernel, seq=seq, tr=tr),
        grid=(2 * seq // tr,),
        in_specs=[pl.BlockSpec((tr, 128), lambda i: (i, 0))] + [full(a) for a in args[1:]],
        out_specs=[pl.BlockSpec((tr, c), lambda i: (i, 0)), pl.BlockSpec((1, c), lambda i: (0, 0))],
        out_shape=[jax.ShapeDtypeStruct((2 * seq, c), F32), jax.ShapeDtypeStruct((1, c), F32)],
        compiler_params=_cparams("arbitrary"),
        name="hyena_filter",
    )(*args)


def _dft_tables(seq):
    n = 2 * seq
    n2 = FFT_N2
    n1 = n // n2
    f1 = np.arange(n1)[:, None]
    s1 = np.arange(n1)[None, :]
    a = 2.0 * np.pi * (f1 * s1 % n1) / n1
    fa = np.empty((2 * n1, n1), np.float64)
    fa[0::2] = np.cos(a)
    fa[1::2] = -np.sin(a)
    ia = np.empty((n1 // 2, 2 * n1), np.float64)
    at = a.T[: n1 // 2]
    ia[:, 0::2] = np.cos(at) / n
    ia[:, 1::2] = -np.sin(at) / n
    f1b = np.arange(n1)[:, None, None]
    f2 = np.arange(n2)[None, :, None]
    s2 = np.arange(n2)[None, None, :]
    ph = 2.0 * np.pi * ((s2 * f2 * n1 + s2 * f1b) % n) / n
    gr, gi = np.cos(ph), -np.sin(ph)
    gf = np.concatenate([np.concatenate([gr, -gi], axis=2),
                         np.concatenate([gi, gr], axis=2)], axis=1)
    er, ei = np.transpose(gr, (0, 2, 1)), -np.transpose(gi, (0, 2, 1))
    gb = np.concatenate([np.concatenate([er, -ei], axis=2),
                         np.concatenate([ei, er], axis=2)], axis=1)
    return fa.astype(np.float32), ia.astype(np.float32), gf.astype(np.float32), gb.astype(np.float32)


def _filter_stage_a_kernel(w_ref, x_ref, o_ref):
    o_ref[...] = jnp.dot(w_ref[...], x_ref[...], precision=lax.Precision.HIGHEST, preferred_element_type=F32)


def filter_stage_a(w, x):
    k, n = x.shape
    m = w.shape[0]
    nt = min(2048, n)
    return pl.pallas_call(
        _filter_stage_a_kernel,
        grid=(n // nt,),
        in_specs=[pl.BlockSpec((m, k), lambda j: (0, 0)),
                  pl.BlockSpec((k, nt), lambda j: (0, j))],
        out_specs=pl.BlockSpec((m, nt), lambda j: (0, j)),
        out_shape=jax.ShapeDtypeStruct((m, n), F32),
        compiler_params=_cparams("parallel"),
        name="filter_stage_a",
    )(w, x)


def _bmm(g, x, precise):
    dn = (((2,), (1,)), ((0,), (0,)))
    if precise:
        return lax.dot_general(g, x, dn, precision=lax.Precision.HIGHEST, preferred_element_type=F32)
    return lax.dot_general(g, x.astype(BF16), dn, preferred_element_type=F32)


def _filter_spectrum_kernel(g_ref, x_ref, ss_ref, o_ref):
    o_ref[...] = _bmm(g_ref[...], x_ref[...], True) * lax.rsqrt(ss_ref[...] + EPS)


def filter_spectrum(gf, x, ss):
    n1, k2, c = x.shape
    ft = min(8, n1)
    ct = min(256, c)
    return pl.pallas_call(
        _filter_spectrum_kernel,
        grid=(n1 // ft, c // ct),
        in_specs=[pl.BlockSpec((ft, k2, k2), lambda i, j: (i, 0, 0)),
                  pl.BlockSpec((ft, k2, ct), lambda i, j: (i, 0, j)),
                  pl.BlockSpec((1, ct), lambda i, j: (0, j))],
        out_specs=pl.BlockSpec((ft, k2, ct), lambda i, j: (i, 0, j)),
        out_shape=jax.ShapeDtypeStruct((n1, k2, c), F32),
        compiler_params=_cparams("parallel", "parallel"),
        name="filter_spectrum",
    )(gf, x, ss)


def _spectral_kernel(gf_ref, gb_ref, h_ref, x_ref, o_ref):
    n2 = h_ref.shape[1] // 2
    spec = _bmm(gf_ref[...], x_ref[...], False)
    sr, si = spec[:, :n2], spec[:, n2:]
    hr, hi = h_ref[:, :n2], h_ref[:, n2:]
    prod = jnp.concatenate([sr * hr - si * hi, sr * hi + si * hr], axis=1)
    o_ref[...] = _bmm(gb_ref[...], prod, False).astype(o_ref.dtype)


def spectral_multiply(gf, gb, hspec, x):
    bsz, n1, k2, c = x.shape
    ft = min(8, n1)
    ct = min(512, c)
    gspec = pl.BlockSpec((ft, k2, k2), lambda i, j, b: (i, 0, 0))
    return pl.pallas_call(
        _spectral_kernel,
        grid=(n1 // ft, c // ct, bsz),
        in_specs=[gspec, gspec,
                  pl.BlockSpec((ft, k2, ct), lambda i, j, b: (i, 0, j)),
                  pl.BlockSpec((None, ft, k2, ct), lambda i, j, b: (b, i, 0, j))],
        out_specs=pl.BlockSpec((None, ft, k2, ct), lambda i, j, b: (b, i, 0, j)),
        out_shape=jax.ShapeDtypeStruct(x.shape, BF16),
        compiler_params=_cparams("parallel", "parallel", "arbitrary"),
        name="spectral_multiply",
    )(gf, gb, hspec, x)


KRON = 8


def _kron_forward_kernel(fk_ref, z_ref, o_ref):
    n1h, n2, ct = z_ref.shape
    n1 = o_ref.shape[0]
    for oc in range(n2 // KRON):
        x8 = z_ref[:, oc * KRON:(oc + 1) * KRON, :].reshape(n1h * KRON, ct).astype(BF16)
        y = jnp.dot(fk_ref[...], x8, preferred_element_type=F32)
        o_ref[:, oc * 2 * KRON:(oc + 1) * 2 * KRON, :] = y.reshape(n1, 2 * KRON, ct).astype(o_ref.dtype)


def kron_forward(fk, z4):
    bsz, n1h, n2, c = z4.shape
    ct = min(256, c)
    return pl.pallas_call(
        _kron_forward_kernel,
        grid=(bsz, c // ct),
        in_specs=[_resident(fk.shape),
                  pl.BlockSpec((None, n1h, n2, ct), lambda b, j: (b, 0, 0, j))],
        out_specs=pl.BlockSpec((None, 2 * n1h, 2 * n2, ct), lambda b, j: (b, 0, 0, j)),
        out_shape=jax.ShapeDtypeStruct((bsz, 2 * n1h, 2 * n2, c), BF16),
        compiler_params=_cparams("parallel", "parallel"),
        name="dft_kron_forward",
    )(fk, z4)


def _kron_inverse_kernel(ik_ref, x_ref, z_ref, xg_ref, d_ref, o_ref):
    n1, _, ct = x_ref.shape
    n1h, n2, _ = z_ref.shape
    for oc in range(n2 // KRON):
        x8 = x_ref[:, oc * 2 * KRON:(oc + 1) * 2 * KRON, :].reshape(n1 * 2 * KRON, ct)
        y = jnp.dot(ik_ref[...], x8, preferred_element_type=F32).reshape(n1h, KRON, ct)
        sl = slice(oc * KRON, (oc + 1) * KRON)
        o_ref[:, sl, :] = ((y + z_ref[:, sl, :] * d_ref[...]) * xg_ref[:, sl, :]).astype(o_ref.dtype)


def kron_inverse(ik, x, z4, xg4, d):
    bsz, n1h, n2, c = z4.shape
    ct = min(256, c)
    blk = pl.BlockSpec((None, n1h, n2, ct), lambda b, j: (b, 0, 0, j))
    return pl.pallas_call(
        _kron_inverse_kernel,
        grid=(bsz, c // ct),
        in_specs=[_resident(ik.shape),
                  pl.BlockSpec((None, 2 * n1h, 2 * n2, ct), lambda b, j: (b, 0, 0, j)),
                  blk, blk,
                  pl.BlockSpec((1, ct), lambda b, j: (0, j))],
        out_specs=blk,
        out_shape=jax.ShapeDtypeStruct((bsz, n1h, n2, c), BF16),
        compiler_params=_cparams("parallel", "parallel"),
        name="dft_kron_inverse",
    )(ik, x, z4, xg4, d.reshape(1, c))


def hyena_long_conv(z, xg, d, kc_raw, ss, bsz, seq, c):
    n2 = FFT_N2
    n1 = 2 * seq // n2
    fa, ia, gf, gb = _dft_tables(seq)
    kspec = filter_stage_a(jnp.asarray(fa), kc_raw.reshape(n1, n2 * c))
    hspec = filter_spectrum(jnp.asarray(gf), kspec.reshape(n1, 2 * n2, c), ss)
    k = np.arange(2 * n2)
    perm = (k // (2 * KRON)) * KRON + (k % KRON) + ((k // KRON) % 2) * n2
    eye = np.eye(KRON, dtype=np.float32)
    fk = jnp.asarray(np.kron(fa[:, : n1 // 2], eye), dtype=BF16)
    ik = jnp.asarray(np.kron(ia, eye), dtype=BF16)
    z4 = z.reshape(bsz, n1 // 2, n2, c)
    za = kron_forward(fk, z4)
    zb = spectral_multiply(jnp.asarray(gf[:, :, perm], dtype=BF16), jnp.asarray(gb[:, perm, :], dtype=BF16),
                           hspec, za)
    ya = kron_inverse(ik, zb, z4, xg.reshape(bsz, n1 // 2, n2, c), d)
    return ya.reshape(bsz * seq, c)


def _gmlp_kernel(u_ref, v_ref, gb_ref, ng_ref, ws_ref, bias_ref, o_ref):
    vn = _rms(v_ref[...].astype(F32), ng_ref[...]).astype(BF16)
    tr, c = vn.shape
    gc = c // GM_GROUPS
    for n in range(tr // CHUNK):
        rows = slice(n * CHUNK, (n + 1) * CHUNK)
        for g in range(GM_GROUPS):
            cols = slice(g * gc, (g + 1) * gc)
            s = jnp.dot(ws_ref[g], vn[rows, cols], preferred_element_type=F32) + bias_ref[:, cols]
            y = u_ref[rows, cols].astype(F32) * s * _silu(gb_ref[rows, cols].astype(F32))
            o_ref[rows, cols] = y.astype(o_ref.dtype)


def gmlp_branch(proj, norm_g, ws, bs, c, col0):
    m = proj.shape[0]
    tr = min(512, m)
    off = col0 // c
    bias = jnp.repeat(bs.T, c // GM_GROUPS, axis=1)
    blk = lambda o: pl.BlockSpec((tr, c), lambda i: (i, off + o))
    return pl.pallas_call(
        _gmlp_kernel,
        grid=(m // tr,),
        in_specs=[blk(0), blk(1), blk(2),
                  pl.BlockSpec((1, c), lambda i: (0, 0)),
                  pl.BlockSpec(ws.shape, lambda i: (0, 0, 0)),
                  pl.BlockSpec(bias.shape, lambda i: (0, 0))],
        out_specs=pl.BlockSpec((tr, c), lambda i: (i, 0)),
        out_shape=jax.ShapeDtypeStruct((m, c), BF16),
        compiler_params=_cparams("parallel"),
        name="gmlp_branch",
    )(proj, proj, proj, norm_g.reshape(1, c), ws.astype(BF16), bias)


def _out_ple_kernel(ya_ref, yb_ref, h_ref, p_ref, wa_ref, wb_ref, pg_ref, gw_ref, up_ref, fg_ref, o_ref,
                    *, final):
    mix = (jnp.dot(ya_ref[...].astype(BF16), wa_ref[...], preferred_element_type=F32)
           + jnp.dot(yb_ref[...].astype(BF16), wb_ref[...], preferred_element_type=F32))
    h1 = h_ref[...] + mix
    r = _rms(h1, pg_ref[...]).astype(BF16)
    gate = jax.nn.sigmoid(jnp.dot(r, gw_ref[...], preferred_element_type=F32))
    up = jnp.dot(p_ref[...].astype(BF16), up_ref[...], preferred_element_type=F32)
    h2 = h1 + up * gate
    if final:
        h2 = _rms(h2, fg_ref[...])
    o_ref[...] = h2


def out_ple(ya, yb, h, p, w_out, ple_g, gate_w, ple_up, final_g, final):
    m, d = h.shape
    ca = ya.shape[1]
    tm = min(512, m)
    rowblk = lambda a: pl.BlockSpec((tm, a.shape[1]), lambda i: (i, 0))
    full = lambda a: pl.BlockSpec(a.shape, lambda i: (0, 0))
    wa = w_out[:ca].astype(BF16)
    wb = w_out[ca:].astype(BF16)
    args = (ya, yb, h, p, wa, wb, ple_g.reshape(1, d), gate_w.astype(BF16), ple_up.astype(BF16),
            final_g.reshape(1, d))
    return pl.pallas_call(
        functools.partial(_out_ple_kernel, final=final),
        grid=(m // tm,),
        in_specs=[rowblk(a) for a in args[:4]] + [full(a) for a in args[4:]],
        out_specs=pl.BlockSpec((tm, d), lambda i: (i, 0)),
        out_shape=jax.ShapeDtypeStruct((m, d), F32),
        compiler_params=_cparams("parallel"),
        name="out_ple",
    )(*args)


POOL_HALO = 16


def _pool_kernel(x_ref, prev_ref, next_ref, gc_ref, w_ref, b_ref, sc_ref, o_ref, pad_ref, *, seq):
    tr, c = x_ref.shape
    i = pl.program_id(0)
    t0 = (i * tr) % seq
    zeros = jnp.zeros((POOL_HALO, c), F32)
    pad_ref[0:POOL_HALO] = jnp.where(t0 == 0, zeros, prev_ref[...].astype(F32))
    pad_ref[POOL_HALO:POOL_HALO + tr] = x_ref[...].astype(F32)
    pad_ref[POOL_HALO + tr:] = jnp.where(t0 + tr == seq, zeros, next_ref[...].astype(F32))
    t = t0 + lax.broadcasted_iota(jnp.int32, (tr, 1), 0)
    gcw = c // len(POOL_WINDOWS)
    for g, w in enumerate(POOL_WINDOWS):
        cols = slice(g * gcw, (g + 1) * gcw)
        acc = pad_ref[POOL_HALO - w // 2:POOL_HALO - w // 2 + tr, cols]
        for o in range(-w // 2 + 1, w // 2):
            acc = acc + pad_ref[POOL_HALO + o:POOL_HALO + o + tr, cols]
        cnt = (jnp.minimum(t + w // 2, seq) - jnp.maximum(t - w // 2, 0)).astype(F32)
        dlt = acc / cnt - pad_ref[POOL_HALO:POOL_HALO + tr, cols]
        y = jnp.dot(dlt.astype(BF16), w_ref[g], preferred_element_type=F32) + b_ref[:, cols]
        o_ref[:, cols] = (y * sc_ref[:, cols] * _silu(gc_ref[:, cols].astype(F32))).astype(o_ref.dtype)


def pool_branch(proj, pool_w, pool_b, pool_scale, seq, c):
    m = proj.shape[0]
    tr = min(512, seq)
    hb = tr // POOL_HALO
    nblk = m // POOL_HALO
    return pl.pallas_call(
        functools.partial(_pool_kernel, seq=seq),
        grid=(m // tr,),
        in_specs=[pl.BlockSpec((tr, c), lambda i: (i, 0)),
                  pl.BlockSpec((POOL_HALO, c), lambda i: (jnp.maximum(i * hb - 1, 0), 0)),
                  pl.BlockSpec((POOL_HALO, c), lambda i: (jnp.minimum((i + 1) * hb, nblk - 1), 0)),
                  pl.BlockSpec((tr, c), lambda i: (i, 1)),
                  pl.BlockSpec(pool_w.shape, lambda i: (0, 0, 0)),
                  pl.BlockSpec((1, c), lambda i: (0, 0)),
                  pl.BlockSpec((1, c), lambda i: (0, 0))],
        out_specs=pl.BlockSpec((tr, c), lambda i: (i, 0)),
        out_shape=jax.ShapeDtypeStruct((m, c), BF16),
        scratch_shapes=[pltpu.VMEM((tr + 2 * POOL_HALO, c), F32)],
        compiler_params=_cparams("parallel"),
        name="pool_branch",
    )(proj, proj, proj, proj, pool_w.astype(BF16), pool_b.reshape(1, c), pool_scale.reshape(1, c))


NA_ROWS_PER_STEP = 16


def _natten_bias_table(rpb):
    nh = rpb.shape[0]
    q = np.arange(GRID_W)[:, None]
    kc = np.arange(GRID_W)[None, :]
    start = np.clip(q - NA_KW // 2, 0, GRID_W - NA_KW)
    inside = (kc >= start) & (kc < start + NA_KW)
    dc = kc - q + (NA_KW - 1)
    onehot = ((dc[None] == np.arange(2 * NA_KW - 1)[:, None, None]) & inside[None]).astype(np.float32)
    m = jnp.einsum('hrd,dqk->hrqk', rpb.astype(F32), jnp.asarray(onehot), precision=lax.Precision.HIGHEST)
    m = m + jnp.asarray(np.where(inside, 0.0, NEG_BIG).astype(np.float32))
    t = jnp.stack([m[:, NA_KH - 1 - o:2 * NA_KH - 1 - o] for o in range(NA_KH)], axis=1)
    t = t.reshape(nh // 2, 2, NA_KH, NA_KH, GRID_W, GRID_W)
    t = jnp.transpose(t, (0, 2, 1, 4, 3, 5))
    return t.reshape(nh // 2, NA_KH, 2 * GRID_W, NA_KH * GRID_W)


def _natten_kernel(q_ref, k_ref, v_ref, gd_ref, t_ref, o_ref, *, rows):
    rg = pl.program_id(2)
    hd = NA_HEAD_DIM
    lane = lax.broadcasted_iota(jnp.int32, (GRID_W, 2 * hd), 1)
    first = lane < hd
    scale = hd ** -0.5
    nk = NA_KH * GRID_W

    for rr in range(NA_ROWS_PER_STEP):
        r = rg * NA_ROWS_PER_STEP + rr
        rs = jnp.clip(r - NA_KH // 2, 0, rows - NA_KH)
        k0 = pl.multiple_of(rs * GRID_W, GRID_W)
        qrows = slice(rr * GRID_W, (rr + 1) * GRID_W)
        q2 = q_ref[qrows, :].astype(F32) * scale
        qs = jnp.concatenate([jnp.where(first, q2, 0.0), jnp.where(first, 0.0, q2)], axis=0).astype(BF16)
        s = lax.dot_general(qs, k_ref[pl.ds(k0, nk), :], (((1,), (1,)), ((), ())),
                            preferred_element_type=F32)
        s = s + t_ref[r - rs]
        p = jnp.exp(s - jnp.max(s, axis=-1, keepdims=True))
        den = jnp.sum(p, axis=-1, keepdims=True)
        o2 = jnp.dot(p.astype(BF16), v_ref[pl.ds(k0, nk), :], preferred_element_type=F32) / den
        att = jnp.where(first, o2[:GRID_W], o2[GRID_W:])
        o_ref[qrows, :] = (att * _silu(gd_ref[qrows, :].astype(F32))).astype(o_ref.dtype)


def natten_branch(proj, rpb, bsz, seq, c, col0):
    rows = seq // GRID_W
    assert rows >= NA_KH and rows % NA_ROWS_PER_STEP == 0
    table = _natten_bias_table(rpb)
    pw = 2 * NA_HEAD_DIM
    npair = c // pw
    off = col0 // pw
    tq = NA_ROWS_PER_STEP * GRID_W
    nrg = rows // NA_ROWS_PER_STEP
    qblk = lambda o: pl.BlockSpec((tq, pw), lambda b, hp, rg: (b * nrg + rg, off + o * npair + hp))
    kvblk = lambda o: pl.BlockSpec((seq, pw), lambda b, hp, rg: (b, off + o * npair + hp))
    return pl.pallas_call(
        functools.partial(_natten_kernel, rows=rows),
        grid=(bsz, npair, nrg),
        in_specs=[qblk(0), kvblk(1), kvblk(2), qblk(3),
                  pl.BlockSpec((None, NA_KH, 2 * GRID_W, NA_KH * GRID_W), lambda b, hp, rg: (hp, 0, 0, 0))],
        out_specs=pl.BlockSpec((tq, pw), lambda b, hp, rg: (b * nrg + rg, hp)),
        out_shape=jax.ShapeDtypeStruct((bsz * seq, c), BF16),
        compiler_params=_cparams("parallel", "parallel", "parallel"),
        name="natten_branch",
    )(proj, proj, proj, proj, table)


def even_layer_mix(h, norm_g, w_in, conv_w, conv_b, hy_w0, hy_b0, hy_w1, hy_b1, hy_w2, hy_b2, hy_wout,
                   hy_freq, hy_d, gm_norm_g, gm_ws, gm_bs, bsz, seq):
    c = h.shape[1]
    proj = norm_matmul(h, norm_g, w_in.astype(BF16))
    z, xg = hyena_prep(proj, conv_w, conv_b, bsz, seq, c)
    kc_raw, ss = hyena_filter(seq, c, hy_w0, hy_b0, hy_w1, hy_b1, hy_w2, hy_b2, hy_wout, hy_freq)
    ya = hyena_long_conv(z, xg, hy_d, kc_raw, ss, bsz, seq, c)
    yb = gmlp_branch(proj, gm_norm_g, gm_ws, gm_bs, c, 4 * c)
    return ya, yb


def odd_layer_mix(h, norm_g, w_in, pool_w, pool_b, pool_scale, rpb, bsz, seq):
    c = h.shape[1]
    proj = norm_matmul(h, norm_g, w_in.astype(BF16))
    yc = pool_branch(proj, pool_w, pool_b, pool_scale, seq, c)
    yd = natten_branch(proj, rpb, bsz, seq, c, 2 * c)
    return yc, yd


def kernel(x, p, norm_g, final_g, ev_w_in, ev_conv_w, ev_conv_b, hy_w0, hy_b0, hy_w1, hy_b1, hy_w2, hy_b2, hy_wout, hy_freq, hy_d, gm_norm_g, gm_ws, gm_bs, ev_w_out, od_w_in, pool_w, pool_b, pool_scale, na_rpb, od_w_out, ple_up, ple_gate_w, ple_g):
    bsz, seq, d = x.shape
    depth = p.shape[0]
    h = x.reshape(bsz * seq, d)
    for i in range(depth):
        j = i // 2
        if i % 2 == 0:
            ya, yb = even_layer_mix(h, norm_g[i], ev_w_in[j], ev_conv_w[j], ev_conv_b[j], hy_w0[j], hy_b0[j],
                                    hy_w1[j], hy_b1[j], hy_w2[j], hy_b2[j], hy_wout[j], hy_freq[j], hy_d[j],
                                    gm_norm_g[j], gm_ws[j], gm_bs[j], bsz, seq)
            w_out = ev_w_out[j]
        else:
            ya, yb = odd_layer_mix(h, norm_g[i], od_w_in[j], pool_w[j], pool_b[j], pool_scale[j], na_rpb[j],
                                   bsz, seq)
            w_out = od_w_out[j]
        h = out_ple(ya, yb, h, p[i].reshape(bsz * seq, -1), w_out, ple_g[i], ple_gate_w[i], ple_up[i],
                    final_g, final=(i == depth - 1))
    return h.reshape(bsz, seq, d)
```

```python
import functools
import math

import jax
import jax.numpy as jnp
import numpy as np
from jax import lax
from jax.experimental import pallas as pl
from jax.experimental.pallas import tpu as pltpu

F32 = jnp.float32
BF16 = jnp.bfloat16

EPS = 1e-6
GRID_W = 64
PLE_DIM = 256
HY_BANDS = 16
HY_FAST_DECAY = 0.3
HY_SLOW_DECAY = 1.5
HY_TARGET = 1e-2
GM_GROUPS = 8
CHUNK = 128
POOL_WINDOWS = (2, 4, 8, 16)
NA_HEADS = 16
NA_HEAD_DIM = 64
NA_KH = 8
NA_KW = 16
FFT_N2 = 64
NEG_BIG = -1e30
VMEM_LIMIT = 56 * 1024 * 1024


def _cparams(*sem):
    return pltpu.CompilerParams(dimension_semantics=sem, vmem_limit_bytes=VMEM_LIMIT)


def _silu(x):
    return x * jax.nn.sigmoid(x)


def _rms(x, g):
    return x * lax.rsqrt(jnp.mean(x * x, axis=-1, keepdims=True) + EPS) * g


def _resident(shape):
    return pl.BlockSpec(shape, lambda *_: (0,) * len(shape), pipeline_mode=pl.Buffered(1))


def _norm_matmul_kernel(h_ref, g_ref, w_ref, o_ref, *, tn):
    hn = _rms(h_ref[...], g_ref[...]).astype(BF16)
    for j in range(o_ref.shape[1] // tn):
        cols = slice(j * tn, (j + 1) * tn)
        o_ref[:, cols] = jnp.dot(hn, w_ref[:, cols], preferred_element_type=F32).astype(o_ref.dtype)


def norm_matmul(h, g, w):
    m, d = h.shape
    n = w.shape[1]
    tm = min(512, m)
    tn = min(512, n)
    return pl.pallas_call(
        functools.partial(_norm_matmul_kernel, tn=tn),
        grid=(m // tm,),
        in_specs=[pl.BlockSpec((tm, d), lambda i: (i, 0)),
                  _resident((1, d)),
                  _resident((d, n))],
        out_specs=pl.BlockSpec((tm, n), lambda i: (i, 0)),
        out_shape=jax.ShapeDtypeStruct((m, n), BF16),
        compiler_params=_cparams("parallel"),
        name="norm_matmul",
    )(h, g.reshape(1, d), w)


def _short_conv(x, w, b):
    n = x.shape[0]
    row = lax.broadcasted_iota(jnp.int32, x.shape, 0)
    xm = jnp.where(row == 0, 0.0, pltpu.roll(x, 1, 0))
    xp = jnp.where(row == n - 1, 0.0, pltpu.roll(x, n - 1, 0))
    return xm * w[0:1] + x * w[1:2] + xp * w[2:3] + b


def _short_conv_block(x_ref, w_ref, b_ref):
    n1h, n2, ct = x_ref.shape
    return _short_conv(x_ref[...].astype(F32).reshape(n1h * n2, ct), w_ref[...], b_ref[...])


def _positional_features(seq):
    f32 = np.float32
    t = np.linspace(0.0, 1.0, seq, dtype=f32)[:, None]
    ang = (f32(2.0 * math.pi) * np.arange(seq, dtype=f32)[:, None] / f32(seq)).astype(f32)
    bands = np.linspace(1e-4, HY_BANDS - 1, HY_BANDS, dtype=f32)[None, :]
    ba = (bands * ang).astype(f32)
    feats = np.concatenate([t, np.cos(ba), -np.sin(ba)], axis=-1).astype(f32)
    feats2 = np.concatenate([feats, feats[:1], feats[1:][::-1]], axis=0)
    pad = np.zeros((2 * seq, 128 - feats2.shape[1]), f32)
    return np.concatenate([feats2, pad], axis=1)


def _filter_kernel(feat_ref, w0_ref, b0_ref, w1_ref, b1_ref, w2_ref, b2_ref, wo_ref, fr_ref, dl_ref,
                   kc_ref, ss_ref, *, seq, tr):
    hp = lax.Precision.HIGHEST
    i = pl.program_id(0)
    fr = fr_ref[...]
    h = jnp.sin(fr * (jnp.dot(feat_ref[...], w0_ref[...], precision=hp, preferred_element_type=F32) + b0_ref[...]))
    h = jnp.sin(fr * (jnp.dot(h, w1_ref[...], precision=hp, preferred_element_type=F32) + b1_ref[...]))
    h = jnp.sin(fr * (jnp.dot(h, w2_ref[...], precision=hp, preferred_element_type=F32) + b2_ref[...]))
    k2 = jnp.dot(h, wo_ref[...], precision=hp, preferred_element_type=F32)
    c = kc_ref.shape[1]
    s = i * tr + lax.broadcasted_iota(jnp.int32, (tr, 1), 0)
    k = jnp.where(s < seq, k2[:, :c], k2[:, c:])
    lag = jnp.where(s < seq, s, 2 * seq - s).astype(F32)
    t = lag * (1.0 / (seq - 1))
    k = jnp.where(s == seq, 0.0, k * jnp.exp(-t * dl_ref[...]))
    kc_ref[...] = k

    @pl.when(i == 0)
    def _():
        ss_ref[...] = jnp.zeros_like(ss_ref)

    ss_ref[...] += jnp.sum(k * k, axis=0, keepdims=True)


def hyena_filter(seq, c, w0, b0, w1, b1, w2, b2, wout, freq):
    feats = jnp.asarray(_positional_features(seq))
    hid = w0.shape[1]
    w0p = jnp.zeros((128, hid), F32).at[:w0.shape[0]].set(w0)
    max_decay = math.log(HY_TARGET) / HY_FAST_DECAY
    min_decay = math.log(HY_TARGET) / HY_SLOW_DECAY
    deltas = jnp.asarray(np.abs(np.linspace(min_decay, max_decay, c, dtype=np.float32)))[None, :]
    tr = min(512, 2 * seq)
    full = lambda a: pl.BlockSpec(a.shape, lambda i: (0,) * a.ndim)
    row = lambda a: a.reshape(1, -1)
    args = (feats, w0p, row(b0), w1, row(b1), w2, row(b2), wout, row(freq), deltas)
    return pl.pallas_call(
        functools.partial(_filter_kernel, seq=seq, tr=tr),
        grid=(2 * seq // tr,),
        in_specs=[pl.BlockSpec((tr, 128), lambda i: (i, 0))] + [full(a) for a in args[1:]],
        out_specs=[pl.BlockSpec((tr, c), lambda i: (i, 0)), pl.BlockSpec((1, c), lambda i: (0, 0))],
        out_shape=[jax.ShapeDtypeStruct((2 * seq, c), F32), jax.ShapeDtypeStruct((1, c), F32)],
        compiler_params=_cparams("arbitrary"),
        name="hyena_filter",
    )(*args)


def _dft_tables(seq):
    n = 2 * seq
    n2 = FFT_N2
    n1 = n // n2
    f1 = np.arange(n1)[:, None]
    s1 = np.arange(n1)[None, :]
    a = 2.0 * np.pi * (f1 * s1 % n1) / n1
    fa = np.empty((2 * n1, n1), np.float64)
    fa[0::2] = np.cos(a)
    fa[1::2] = -np.sin(a)
    ia = np.empty((n1 // 2, 2 * n1), np.float64)
    at = a.T[: n1 // 2]
    ia[:, 0::2] = np.cos(at) / n
    ia[:, 1::2] = -np.sin(at) / n
    f1b = np.arange(n1)[:, None, None]
    f2 = np.arange(n2)[None, :, None]
    s2 = np.arange(n2)[None, None, :]
    ph = 2.0 * np.pi * ((s2 * f2 * n1 + s2 * f1b) % n) / n
    gr, gi = np.cos(ph), -np.sin(ph)
    gf = np.concatenate([np.concatenate([gr, -gi], axis=2),
                         np.concatenate([gi, gr], axis=2)], axis=1)
    er, ei = np.transpose(gr, (0, 2, 1)), -np.transpose(gi, (0, 2, 1))
    gb = np.concatenate([np.concatenate([er, -ei], axis=2),
                         np.concatenate([ei, er], axis=2)], axis=1)
    return fa.astype(np.float32), ia.astype(np.float32), gf.astype(np.float32), gb.astype(np.float32)


def _filter_stage_a_kernel(w_ref, x_ref, o_ref):
    o_ref[...] = jnp.dot(w_ref[...], x_ref[...], precision=lax.Precision.HIGHEST, preferred_element_type=F32)


def filter_stage_a(w, x):
    k, n = x.shape
    m = w.shape[0]
    nt = min(2048, n)
    return pl.pallas_call(
        _filter_stage_a_kernel,
        grid=(n // nt,),
        in_specs=[pl.BlockSpec((m, k), lambda j: (0, 0)),
                  pl.BlockSpec((k, nt), lambda j: (0, j))],
        out_specs=pl.BlockSpec((m, nt), lambda j: (0, j)),
        out_shape=jax.ShapeDtypeStruct((m, n), F32),
        compiler_params=_cparams("parallel"),
        name="filter_stage_a",
    )(w, x)


def _bmm(g, x, precise):
    dn = (((2,), (1,)), ((0,), (0,)))
    if precise:
        return lax.dot_general(g, x, dn, precision=lax.Precision.HIGHEST, preferred_element_type=F32)
    return lax.dot_general(g, x.astype(BF16), dn, preferred_element_type=F32)


def _filter_spectrum_kernel(g_ref, x_ref, ss_ref, d_ref, o_ref):
    spec = _bmm(g_ref[...], x_ref[...], True) * lax.rsqrt(ss_ref[...] + EPS)
    is_real = lax.broadcasted_iota(jnp.int32, spec.shape, 1) < spec.shape[1] // 2
    o_ref[...] = spec + jnp.where(is_real, d_ref[...], 0.0)


def filter_spectrum(gf, x, ss, d):
    n1, k2, c = x.shape
    ft = min(8, n1)
    ct = min(256, c)
    return pl.pallas_call(
        _filter_spectrum_kernel,
        grid=(n1 // ft, c // ct),
        in_specs=[pl.BlockSpec((ft, k2, k2), lambda i, j: (i, 0, 0)),
                  pl.BlockSpec((ft, k2, ct), lambda i, j: (i, 0, j)),
                  pl.BlockSpec((1, ct), lambda i, j: (0, j)),
                  pl.BlockSpec((1, ct), lambda i, j: (0, j))],
        out_specs=pl.BlockSpec((ft, k2, ct), lambda i, j: (i, 0, j)),
        out_shape=jax.ShapeDtypeStruct((n1, k2, c), F32),
        compiler_params=_cparams("parallel", "parallel"),
        name="filter_spectrum",
    )(gf, x, ss, d.reshape(1, c))


def _spectral_kernel(gf_ref, gb_ref, h_ref, x_ref, o_ref):
    n2 = h_ref.shape[1] // 2
    spec = _bmm(gf_ref[...], x_ref[...], False)
    sr, si = spec[:, :n2], spec[:, n2:]
    hr, hi = h_ref[:, :n2], h_ref[:, n2:]
    prod = jnp.concatenate([sr * hr - si * hi, sr * hi + si * hr], axis=1)
    o_ref[...] = _bmm(gb_ref[...], prod, False).astype(o_ref.dtype)


def spectral_multiply(gf, gb, hspec, x):
    bsz, n1, k2, c = x.shape
    ft = min(8, n1)
    ct = min(1024, c)
    gspec = pl.BlockSpec((ft, k2, k2), lambda i, j, b: (i, 0, 0))
    return pl.pallas_call(
        _spectral_kernel,
        grid=(n1 // ft, c // ct, bsz),
        in_specs=[gspec, gspec,
                  pl.BlockSpec((ft, k2, ct), lambda i, j, b: (i, 0, j)),
                  pl.BlockSpec((None, ft, k2, ct), lambda i, j, b: (b, i, 0, j))],
        out_specs=pl.BlockSpec((None, ft, k2, ct), lambda i, j, b: (b, i, 0, j)),
        out_shape=jax.ShapeDtypeStruct(x.shape, BF16),
        compiler_params=_cparams("parallel", "parallel", "arbitrary"),
        name="spectral_multiply",
    )(gf, gb, hspec, x)


KRON = 8


def _seq_block(ct, nc, group):
    return lambda n1h, n2: pl.BlockSpec((None, n1h, n2, ct), lambda b, j: (b, 0, 0, group * nc + j))


def _conv_specs(ct, nc, group):
    return [pl.BlockSpec((3, ct), lambda b, j: (0, group * nc + j)),
            pl.BlockSpec((1, ct), lambda b, j: (0, group * nc + j))]


def _kron_forward_kernel(fk_ref, x1_ref, v_ref, w1_ref, b1_ref, wv_ref, bv_ref, o_ref, z_ref):
    n1h, n2, ct = x1_ref.shape
    n1 = o_ref.shape[0]
    z = _short_conv_block(x1_ref, w1_ref, b1_ref) * _short_conv_block(v_ref, wv_ref, bv_ref)
    z_ref[...] = z.reshape(n1h, n2, ct)
    for oc in range(n2 // KRON):
        x8 = z_ref[:, oc * KRON:(oc + 1) * KRON, :].reshape(n1h * KRON, ct).astype(BF16)
        y = jnp.dot(fk_ref[...], x8, preferred_element_type=F32)
        o_ref[:, oc * 2 * KRON:(oc + 1) * 2 * KRON, :] = y.reshape(n1, 2 * KRON, ct).astype(o_ref.dtype)


def kron_forward(fk, proj4, conv_w, conv_b, c):
    bsz, n1h, n2, _ = proj4.shape
    ct = min(256, c)
    nc = c // ct
    return pl.pallas_call(
        _kron_forward_kernel,
        grid=(bsz, nc),
        in_specs=[_resident(fk.shape), _seq_block(ct, nc, 1)(n1h, n2), _seq_block(ct, nc, 2)(n1h, n2)]
        + _conv_specs(ct, nc, 1) + _conv_specs(ct, nc, 2),
        out_specs=pl.BlockSpec((None, 2 * n1h, 2 * n2, ct), lambda b, j: (b, 0, 0, j)),
        out_shape=jax.ShapeDtypeStruct((bsz, 2 * n1h, 2 * n2, c), BF16),
        scratch_shapes=[pltpu.VMEM((n1h, n2, ct), F32)],
        compiler_params=_cparams("parallel", "parallel"),
        name="dft_kron_forward",
    )(fk, proj4, proj4, conv_w, conv_b, conv_w, conv_b)


def _kron_inverse_kernel(ik_ref, x_ref, x0_ref, ga_ref, w0_ref, b0_ref, o_ref, xg_ref):
    n1, _, ct = x_ref.shape
    n1h, n2, _ = x0_ref.shape
    gate = _silu(ga_ref[...].astype(F32).reshape(n1h * n2, ct))
    xg_ref[...] = (_short_conv_block(x0_ref, w0_ref, b0_ref) * gate).reshape(n1h, n2, ct)
    for oc in range(n2 // KRON):
        x8 = x_ref[:, oc * 2 * KRON:(oc + 1) * 2 * KRON, :].reshape(n1 * 2 * KRON, ct)
        y = jnp.dot(ik_ref[...], x8, preferred_element_type=F32).reshape(n1h, KRON, ct)
        sl = slice(oc * KRON, (oc + 1) * KRON)
        o_ref[:, sl, :] = (y * xg_ref[:, sl, :]).astype(o_ref.dtype)


def kron_inverse(ik, x, proj4, conv_w, conv_b, c):
    bsz, n1h, n2, _ = proj4.shape
    ct = min(256, c)
    nc = c // ct
    return pl.pallas_call(
        _kron_inverse_kernel,
        grid=(bsz, nc),
        in_specs=[_resident(ik.shape),
                  pl.BlockSpec((None, 2 * n1h, 2 * n2, ct), lambda b, j: (b, 0, 0, j)),
                  _seq_block(ct, nc, 0)(n1h, n2), _seq_block(ct, nc, 3)(n1h, n2)] + _conv_specs(ct, nc, 0),
        out_specs=pl.BlockSpec((None, n1h, n2, ct), lambda b, j: (b, 0, 0, j)),
        out_shape=jax.ShapeDtypeStruct((bsz, n1h, n2, c), BF16),
        scratch_shapes=[pltpu.VMEM((n1h, n2, ct), F32)],
        compiler_params=_cparams("parallel", "parallel"),
        name="dft_kron_inverse",
    )(ik, x, proj4, proj4, conv_w, conv_b)


def hyena_branch(proj, conv_w, conv_b, d, kc_raw, ss, bsz, seq, c):
    n2 = FFT_N2
    n1 = 2 * seq // n2
    fa, ia, gf, gb = _dft_tables(seq)
    kspec = filter_stage_a(jnp.asarray(fa), kc_raw.reshape(n1, n2 * c))
    hspec = filter_spectrum(jnp.asarray(gf), kspec.reshape(n1, 2 * n2, c), ss, d)
    k = np.arange(2 * n2)
    perm = (k // (2 * KRON)) * KRON + (k % KRON) + ((k // KRON) % 2) * n2
    eye = np.eye(KRON, dtype=np.float32)
    fk = jnp.asarray(np.kron(fa[:, : n1 // 2], eye), dtype=BF16)
    ik = jnp.asarray(np.kron(ia, eye), dtype=BF16)
    proj4 = proj.reshape(bsz, n1 // 2, n2, proj.shape[1])
    cb = conv_b.reshape(1, -1)
    za = kron_forward(fk, proj4, conv_w, cb, c)
    zb = spectral_multiply(jnp.asarray(gf[:, :, perm], dtype=BF16), jnp.asarray(gb[:, perm, :], dtype=BF16),
                           hspec, za)
    ya = kron_inverse(ik, zb, proj4, conv_w, cb, c)
    return ya.reshape(bsz * seq, c)


def _gmlp_kernel(u_ref, v_ref, gb_ref, ng_ref, ws_ref, bias_ref, o_ref):
    vn = _rms(v_ref[...].astype(F32), ng_ref[...]).astype(BF16)
    tr, c = vn.shape
    gc = c // GM_GROUPS
    for n in range(tr // CHUNK):
        rows = slice(n * CHUNK, (n + 1) * CHUNK)
        for g in range(GM_GROUPS):
            cols = slice(g * gc, (g + 1) * gc)
            s = jnp.dot(ws_ref[g], vn[rows, cols], preferred_element_type=F32) + bias_ref[:, cols]
            y = u_ref[rows, cols].astype(F32) * s * _silu(gb_ref[rows, cols].astype(F32))
            o_ref[rows, cols] = y.astype(o_ref.dtype)


def gmlp_branch(proj, norm_g, ws, bs, c, col0):
    m = proj.shape[0]
    tr = min(512, m)
    off = col0 // c
    bias = jnp.repeat(bs.T, c // GM_GROUPS, axis=1)
    blk = lambda o: pl.BlockSpec((tr, c), lambda i: (i, off + o))
    return pl.pallas_call(
        _gmlp_kernel,
        grid=(m // tr,),
        in_specs=[blk(0), blk(1), blk(2),
                  pl.BlockSpec((1, c), lambda i: (0, 0)),
                  pl.BlockSpec(ws.shape, lambda i: (0, 0, 0)),
                  pl.BlockSpec(bias.shape, lambda i: (0, 0))],
        out_specs=pl.BlockSpec((tr, c), lambda i: (i, 0)),
        out_shape=jax.ShapeDtypeStruct((m, c), BF16),
        compiler_params=_cparams("parallel"),
        name="gmlp_branch",
    )(proj, proj, proj, norm_g.reshape(1, c), ws.astype(BF16), bias)


def _out_ple_kernel(ya_ref, yb_ref, h_ref, p_ref, wa_ref, wb_ref, pg_ref, gw_ref, up_ref, fg_ref, o_ref,
                    *, final):
    mix = (jnp.dot(ya_ref[...].astype(BF16), wa_ref[...], preferred_element_type=F32)
           + jnp.dot(yb_ref[...].astype(BF16), wb_ref[...], preferred_element_type=F32))
    h1 = h_ref[...] + mix
    r = _rms(h1, pg_ref[...]).astype(BF16)
    gate = jax.nn.sigmoid(jnp.dot(r, gw_ref[...], preferred_element_type=F32))
    up = jnp.dot(p_ref[...].astype(BF16), up_ref[...], preferred_element_type=F32)
    h2 = h1 + up * gate
    if final:
        h2 = _rms(h2, fg_ref[...])
    o_ref[...] = h2


def out_ple(ya, yb, h, p, w_out, ple_g, gate_w, ple_up, final_g, final):
    m, d = h.shape
    ca = ya.shape[1]
    tm = min(512, m)
    rowblk = lambda a: pl.BlockSpec((tm, a.shape[1]), lambda i: (i, 0))
    full = lambda a: pl.BlockSpec(a.shape, lambda i: (0, 0))
    wa = w_out[:ca].astype(BF16)
    wb = w_out[ca:].astype(BF16)
    args = (ya, yb, h, p, wa, wb, ple_g.reshape(1, d), gate_w.astype(BF16), ple_up.astype(BF16),
            final_g.reshape(1, d))
    return pl.pallas_call(
        functools.partial(_out_ple_kernel, final=final),
        grid=(m // tm,),
        in_specs=[rowblk(a) for a in args[:4]] + [full(a) for a in args[4:]],
        out_specs=pl.BlockSpec((tm, d), lambda i: (i, 0)),
        out_shape=jax.ShapeDtypeStruct((m, d), F32),
        compiler_params=_cparams("parallel"),
        name="out_ple",
    )(*args)


POOL_HALO = 16


def _pool_kernel(x_ref, prev_ref, next_ref, gc_ref, w_ref, b_ref, sc_ref, o_ref, pad_ref, *, seq):
    tr, c = x_ref.shape
    i = pl.program_id(0)
    t0 = (i * tr) % seq
    zeros = jnp.zeros((POOL_HALO, c), F32)
    pad_ref[0:POOL_HALO] = jnp.where(t0 == 0, zeros, prev_ref[...].astype(F32))
    pad_ref[POOL_HALO:POOL_HALO + tr] = x_ref[...].astype(F32)
    pad_ref[POOL_HALO + tr:] = jnp.where(t0 + tr == seq, zeros, next_ref[...].astype(F32))
    t = t0 + lax.broadcasted_iota(jnp.int32, (tr, 1), 0)
    gcw = c // len(POOL_WINDOWS)
    for g, w in enumerate(POOL_WINDOWS):
        cols = slice(g * gcw, (g + 1) * gcw)
        acc = pad_ref[POOL_HALO - w // 2:POOL_HALO - w // 2 + tr, cols]
        for o in range(-w // 2 + 1, w // 2):
            acc = acc + pad_ref[POOL_HALO + o:POOL_HALO + o + tr, cols]
        cnt = (jnp.minimum(t + w // 2, seq) - jnp.maximum(t - w // 2, 0)).astype(F32)
        dlt = acc / cnt - pad_ref[POOL_HALO:POOL_HALO + tr, cols]
        y = jnp.dot(dlt.astype(BF16), w_ref[g], preferred_element_type=F32) + b_ref[:, cols]
        o_ref[:, cols] = (y * sc_ref[:, cols] * _silu(gc_ref[:, cols].astype(F32))).astype(o_ref.dtype)


def pool_branch(proj, pool_w, pool_b, pool_scale, seq, c):
    m = proj.shape[0]
    tr = min(512, seq)
    hb = tr // POOL_HALO
    nblk = m // POOL_HALO
    return pl.pallas_call(
        functools.partial(_pool_kernel, seq=seq),
        grid=(m // tr,),
        in_specs=[pl.BlockSpec((tr, c), lambda i: (i, 0)),
                  pl.BlockSpec((POOL_HALO, c), lambda i: (jnp.maximum(i * hb - 1, 0), 0)),
                  pl.BlockSpec((POOL_HALO, c), lambda i: (jnp.minimum((i + 1) * hb, nblk - 1), 0)),
                  pl.BlockSpec((tr, c), lambda i: (i, 1)),
                  pl.BlockSpec(pool_w.shape, lambda i: (0, 0, 0)),
                  pl.BlockSpec((1, c), lambda i: (0, 0)),
                  pl.BlockSpec((1, c), lambda i: (0, 0))],
        out_specs=pl.BlockSpec((tr, c), lambda i: (i, 0)),
        out_shape=jax.ShapeDtypeStruct((m, c), BF16),
        scratch_shapes=[pltpu.VMEM((tr + 2 * POOL_HALO, c), F32)],
        compiler_params=_cparams("parallel"),
        name="pool_branch",
    )(proj, proj, proj, proj, pool_w.astype(BF16), pool_b.reshape(1, c), pool_scale.reshape(1, c))


NA_ROWS_PER_STEP = 16
NA_LOOKAHEAD = 3
LOG2E = math.log2(math.e)


def _natten_bias_table(rpb):
    nh = rpb.shape[0]
    q = np.arange(GRID_W)[:, None]
    kc = np.arange(GRID_W)[None, :]
    start = np.clip(q - NA_KW // 2, 0, GRID_W - NA_KW)
    inside = (kc >= start) & (kc < start + NA_KW)
    dc = kc - q + (NA_KW - 1)
    onehot = ((dc[None] == np.arange(2 * NA_KW - 1)[:, None, None]) & inside[None]).astype(np.float32)
    m = jnp.einsum('hrd,dqk->hrqk', rpb.astype(F32), jnp.asarray(onehot), precision=lax.Precision.HIGHEST)
    m = m * LOG2E + jnp.asarray(np.where(inside, 0.0, NEG_BIG).astype(np.float32))
    t = jnp.stack([m[:, NA_KH - 1 - o:2 * NA_KH - 1 - o] for o in range(NA_KH)], axis=1)
    t = t.reshape(nh // 2, 2, NA_KH, NA_KH, GRID_W, GRID_W)
    t = jnp.transpose(t, (0, 2, 1, 4, 3, 5))
    return t.reshape(nh // 2, NA_KH, 2 * GRID_W, NA_KH * GRID_W)


def _natten_kernel(q_ref, k_ref, v_ref, gd_ref, t_ref, o_ref, *, rows):
    rg = pl.program_id(2)
    hd = NA_HEAD_DIM
    lane = lax.broadcasted_iota(jnp.int32, (GRID_W, 2 * hd), 1)
    first = lane < hd
    scale = hd ** -0.5 * LOG2E
    nk = NA_KH * GRID_W

    def key_start(rr):
        r = rg * NA_ROWS_PER_STEP + rr
        rs = jnp.clip(r - NA_KH // 2, 0, rows - NA_KH)
        return r - rs, pl.multiple_of(rs * GRID_W, GRID_W)

    def scores(rr):
        off, k0 = key_start(rr)
        q2 = q_ref[rr * GRID_W:(rr + 1) * GRID_W, :].astype(F32) * scale
        qs = jnp.concatenate([jnp.where(first, q2, 0.0), jnp.where(first, 0.0, q2)], axis=0).astype(BF16)
        s = lax.dot_general(qs, k_ref[pl.ds(k0, nk), :], (((1,), (1,)), ((), ())),
                            preferred_element_type=F32)
        return s + t_ref[off]

    pending = [scores(rr) for rr in range(NA_LOOKAHEAD)]
    for rr in range(NA_ROWS_PER_STEP):
        if rr + NA_LOOKAHEAD < NA_ROWS_PER_STEP:
            pending.append(scores(rr + NA_LOOKAHEAD))
        s = pending.pop(0)
        _, k0 = key_start(rr)
        qrows = slice(rr * GRID_W, (rr + 1) * GRID_W)
        p = jnp.exp2(s - jnp.max(s, axis=-1, keepdims=True))
        den = jnp.sum(p, axis=-1, keepdims=True)
        o2 = jnp.dot(p.astype(BF16), v_ref[pl.ds(k0, nk), :], preferred_element_type=F32) / den
        att = jnp.where(first, o2[:GRID_W], o2[GRID_W:])
        o_ref[qrows, :] = (att * _silu(gd_ref[qrows, :].astype(F32))).astype(o_ref.dtype)


def natten_branch(proj, rpb, bsz, seq, c, col0):
    rows = seq // GRID_W
    assert rows >= NA_KH and rows % NA_ROWS_PER_STEP == 0
    table = _natten_bias_table(rpb)
    pw = 2 * NA_HEAD_DIM
    npair = c // pw
    off = col0 // pw
    tq = NA_ROWS_PER_STEP * GRID_W
    nrg = rows // NA_ROWS_PER_STEP
    qblk = lambda o: pl.BlockSpec((tq, pw), lambda b, hp, rg: (b * nrg + rg, off + o * npair + hp))
    kvblk = lambda o: pl.BlockSpec((seq, pw), lambda b, hp, rg: (b, off + o * npair + hp))
    return pl.pallas_call(
        functools.partial(_natten_kernel, rows=rows),
        grid=(bsz, npair, nrg),
        in_specs=[qblk(0), kvblk(1), kvblk(2), qblk(3),
                  pl.BlockSpec((None, NA_KH, 2 * GRID_W, NA_KH * GRID_W), lambda b, hp, rg: (hp, 0, 0, 0))],
        out_specs=pl.BlockSpec((tq, pw), lambda b, hp, rg: (b * nrg + rg, hp)),
        out_shape=jax.ShapeDtypeStruct((bsz * seq, c), BF16),
        compiler_params=_cparams("parallel", "parallel", "parallel"),
        name="natten_branch",
    )(proj, proj, proj, proj, table)


def even_layer_mix(h, norm_g, w_in, conv_w, conv_b, hy_w0, hy_b0, hy_w1, hy_b1, hy_w2, hy_b2, hy_wout,
                   hy_freq, hy_d, gm_norm_g, gm_ws, gm_bs, bsz, seq):
    c = h.shape[1]
    proj = norm_matmul(h, norm_g, w_in.astype(BF16))
    kc_raw, ss = hyena_filter(seq, c, hy_w0, hy_b0, hy_w1, hy_b1, hy_w2, hy_b2, hy_wout, hy_freq)
    ya = hyena_branch(proj, conv_w, conv_b, hy_d, kc_raw, ss, bsz, seq, c)
    yb = gmlp_branch(proj, gm_norm_g, gm_ws, gm_bs, c, 4 * c)
    return ya, yb


def odd_layer_mix(h, norm_g, w_in, pool_w, pool_b, pool_scale, rpb, bsz, seq):
    c = h.shape[1]
    proj = norm_matmul(h, norm_g, w_in.astype(BF16))
    yc = pool_branch(proj, pool_w, pool_b, pool_scale, seq, c)
    yd = natten_branch(proj, rpb, bsz, seq, c, 2 * c)
    return yc, yd


def kernel(x, p, norm_g, final_g, ev_w_in, ev_conv_w, ev_conv_b, hy_w0, hy_b0, hy_w1, hy_b1, hy_w2, hy_b2, hy_wout, hy_freq, hy_d, gm_norm_g, gm_ws, gm_bs, ev_w_out, od_w_in, pool_w, pool_b, pool_scale, na_rpb, od_w_out, ple_up, ple_gate_w, ple_g):
    bsz, seq, d = x.shape
    depth = p.shape[0]
    h = x.reshape(bsz * seq, d)
    for i in range(depth):
        j = i // 2
        if i % 2 == 0:
            ya, yb = even_layer_mix(h, norm_g[i], ev_w_in[j], ev_conv_w[j], ev_conv_b[j], hy_w0[j], hy_b0[j],
                                    hy_w1[j], hy_b1[j], hy_w2[j], hy_b2[j], hy_wout[j], hy_freq[j], hy_d[j],
                                    gm_norm_g[j], gm_ws[j], gm_bs[j], bsz, seq)
            w_out = ev_w_out[j]
        else:
            ya, yb = odd_layer_mix(h, norm_g[i], od_w_in[j], pool_w[j], pool_b[j], pool_scale[j], na_rpb[j],
                                   bsz, seq)
            w_out = od_w_out[j]
        h = out_ple(ya, yb, h, p[i].reshape(bsz * seq, -1), w_out, ple_g[i], ple_gate_w[i], ple_up[i],
                    final_g, final=(i == depth - 1))
    return h.reshape(bsz, seq, d)
```

```python
import functools
import math

import jax
import jax.numpy as jnp
import numpy as np
from jax import lax
from jax.experimental import pallas as pl
from jax.experimental.pallas import tpu as pltpu

F32 = jnp.float32
BF16 = jnp.bfloat16

EPS = 1e-6
GRID_W = 64
PLE_DIM = 256
HY_BANDS = 16
HY_FAST_DECAY = 0.3
HY_SLOW_DECAY = 1.5
HY_TARGET = 1e-2
GM_GROUPS = 8
CHUNK = 128
POOL_WINDOWS = (2, 4, 8, 16)
NA_HEADS = 16
NA_HEAD_DIM = 64
NA_KH = 8
NA_KW = 16
FFT_N2 = 64
NEG_BIG = -1e30
VMEM_LIMIT = 56 * 1024 * 1024


def _cparams(*sem):
    return pltpu.CompilerParams(dimension_semantics=sem, vmem_limit_bytes=VMEM_LIMIT)


def _silu(x):
    return x * jax.nn.sigmoid(x)


def _rms(x, g):
    return x * lax.rsqrt(jnp.mean(x * x, axis=-1, keepdims=True) + EPS) * g


def _resident(shape):
    return pl.BlockSpec(shape, lambda *_: (0,) * len(shape), pipeline_mode=pl.Buffered(1))


def _norm_matmul_kernel(h_ref, g_ref, w_ref, o_ref, *, tn):
    hn = _rms(h_ref[...], g_ref[...]).astype(BF16)
    for j in range(o_ref.shape[1] // tn):
        cols = slice(j * tn, (j + 1) * tn)
        o_ref[:, cols] = jnp.dot(hn, w_ref[:, cols], preferred_element_type=F32).astype(o_ref.dtype)


def norm_matmul(h, g, w):
    m, d = h.shape
    n = w.shape[1]
    tm = min(512, m)
    tn = min(512, n)
    return pl.pallas_call(
        functools.partial(_norm_matmul_kernel, tn=tn),
        grid=(m // tm,),
        in_specs=[pl.BlockSpec((tm, d), lambda i: (i, 0)),
                  _resident((1, d)),
                  _resident((d, n))],
        out_specs=pl.BlockSpec((tm, n), lambda i: (i, 0)),
        out_shape=jax.ShapeDtypeStruct((m, n), BF16),
        compiler_params=_cparams("parallel"),
        name="norm_matmul",
    )(h, g.reshape(1, d), w)


def _short_conv(x, w, b):
    n = x.shape[0]
    row = lax.broadcasted_iota(jnp.int32, x.shape, 0)
    xm = jnp.where(row == 0, 0.0, pltpu.roll(x, 1, 0))
    xp = jnp.where(row == n - 1, 0.0, pltpu.roll(x, n - 1, 0))
    return xm * w[0:1] + x * w[1:2] + xp * w[2:3] + b


def _short_conv_block(x_ref, w_ref, b_ref):
    n1h, n2, ct = x_ref.shape
    return _short_conv(x_ref[...].astype(F32).reshape(n1h * n2, ct), w_ref[...], b_ref[...])


def _positional_features(seq):
    f32 = np.float32
    t = np.linspace(0.0, 1.0, seq, dtype=f32)[:, None]
    ang = (f32(2.0 * math.pi) * np.arange(seq, dtype=f32)[:, None] / f32(seq)).astype(f32)
    bands = np.linspace(1e-4, HY_BANDS - 1, HY_BANDS, dtype=f32)[None, :]
    ba = (bands * ang).astype(f32)
    feats = np.concatenate([t, np.cos(ba), -np.sin(ba)], axis=-1).astype(f32)
    feats2 = np.concatenate([feats, feats[:1], feats[1:][::-1]], axis=0)
    pad = np.zeros((2 * seq, 128 - feats2.shape[1]), f32)
    return np.concatenate([feats2, pad], axis=1)


def _filter_kernel(feat_ref, w0_ref, b0_ref, w1_ref, b1_ref, w2_ref, b2_ref, wo_ref, fr_ref, dl_ref,
                   kc_ref, ss_ref, *, seq, tr):
    hp = lax.Precision.HIGHEST
    i = pl.program_id(0)
    fr = fr_ref[...]
    h = jnp.sin(fr * (jnp.dot(feat_ref[...], w0_ref[...], precision=hp, preferred_element_type=F32) + b0_ref[...]))
    h = jnp.sin(fr * (jnp.dot(h, w1_ref[...], precision=hp, preferred_element_type=F32) + b1_ref[...]))
    h = jnp.sin(fr * (jnp.dot(h, w2_ref[...], precision=hp, preferred_element_type=F32) + b2_ref[...]))
    k2 = jnp.dot(h, wo_ref[...], precision=hp, preferred_element_type=F32)
    c = kc_ref.shape[1]
    s = i * tr + lax.broadcasted_iota(jnp.int32, (tr, 1), 0)
    k = jnp.where(s < seq, k2[:, :c], k2[:, c:])
    lag = jnp.where(s < seq, s, 2 * seq - s).astype(F32)
    t = lag * (1.0 / (seq - 1))
    k = jnp.where(s == seq, 0.0, k * jnp.exp(-t * dl_ref[...]))
    kc_ref[...] = k

    @pl.when(i == 0)
    def _():
        ss_ref[...] = jnp.zeros_like(ss_ref)

    ss_ref[...] += jnp.sum(k * k, axis=0, keepdims=True)


def hyena_filter(seq, c, w0, b0, w1, b1, w2, b2, wout, freq):
    feats = jnp.asarray(_positional_features(seq))
    hid = w0.shape[1]
    w0p = jnp.zeros((128, hid), F32).at[:w0.shape[0]].set(w0)
    max_decay = math.log(HY_TARGET) / HY_FAST_DECAY
    min_decay = math.log(HY_TARGET) / HY_SLOW_DECAY
    deltas = jnp.asarray(np.abs(np.linspace(min_decay, max_decay, c, dtype=np.float32)))[None, :]
    tr = min(512, 2 * seq)
    full = lambda a: pl.BlockSpec(a.shape, lambda i: (0,) * a.ndim)
    row = lambda a: a.reshape(1, -1)
    args = (feats, w0p, row(b0), w1, row(b1), w2, row(b2), wout, row(freq), deltas)
    return pl.pallas_call(
        functools.partial(_filter_kernel, seq=seq, tr=tr),
        grid=(2 * seq // tr,),
        in_specs=[pl.BlockSpec((tr, 128), lambda i: (i, 0))] + [full(a) for a in args[1:]],
        out_specs=[pl.BlockSpec((tr, c), lambda i: (i, 0)), pl.BlockSpec((1, c), lambda i: (0, 0))],
        out_shape=[jax.ShapeDtypeStruct((2 * seq, c), F32), jax.ShapeDtypeStruct((1, c), F32)],
        compiler_params=_cparams("arbitrary"),
        name="hyena_filter",
    )(*args)


def _dft_tables(seq):
    n = 2 * seq
    n2 = FFT_N2
    n1 = n // n2
    f1 = np.arange(n1)[:, None]
    s1 = np.arange(n1)[None, :]
    a = 2.0 * np.pi * (f1 * s1 % n1) / n1
    fa = np.empty((2 * n1, n1), np.float64)
    fa[0::2] = np.cos(a)
    fa[1::2] = -np.sin(a)
    ia = np.empty((n1 // 2, 2 * n1), np.float64)
    at = a.T[: n1 // 2]
    ia[:, 0::2] = np.cos(at) / n
    ia[:, 1::2] = -np.sin(at) / n
    f1b = np.arange(n1)[:, None, None]
    f2 = np.arange(n2)[None, :, None]
    s2 = np.arange(n2)[None, None, :]
    ph = 2.0 * np.pi * ((s2 * f2 * n1 + s2 * f1b) % n) / n
    gr, gi = np.cos(ph), -np.sin(ph)
    gf = np.concatenate([np.concatenate([gr, -gi], axis=2),
                         np.concatenate([gi, gr], axis=2)], axis=1)
    er, ei = np.transpose(gr, (0, 2, 1)), -np.transpose(gi, (0, 2, 1))
    gb = np.concatenate([np.concatenate([er, -ei], axis=2),
                         np.concatenate([ei, er], axis=2)], axis=1)
    return fa.astype(np.float32), ia.astype(np.float32), gf.astype(np.float32), gb.astype(np.float32)


def _filter_stage_a_kernel(w_ref, x_ref, o_ref):
    o_ref[...] = jnp.dot(w_ref[...], x_ref[...], precision=lax.Precision.HIGHEST, preferred_element_type=F32)


def filter_stage_a(w, x):
    k, n = x.shape
    m = w.shape[0]
    nt = min(2048, n)
    return pl.pallas_call(
        _filter_stage_a_kernel,
        grid=(n // nt,),
        in_specs=[pl.BlockSpec((m, k), lambda j: (0, 0)),
                  pl.BlockSpec((k, nt), lambda j: (0, j))],
        out_specs=pl.BlockSpec((m, nt), lambda j: (0, j)),
        out_shape=jax.ShapeDtypeStruct((m, n), F32),
        compiler_params=_cparams("parallel"),
        name="filter_stage_a",
    )(w, x)


def _bmm(g, x, precise):
    dn = (((2,), (1,)), ((0,), (0,)))
    if precise:
        return lax.dot_general(g, x, dn, precision=lax.Precision.HIGHEST, preferred_element_type=F32)
    return lax.dot_general(g, x.astype(BF16), dn, preferred_element_type=F32)


def _filter_spectrum_kernel(g_ref, x_ref, ss_ref, d_ref, o_ref):
    spec = _bmm(g_ref[...], x_ref[...], True) * lax.rsqrt(ss_ref[...] + EPS)
    is_real = lax.broadcasted_iota(jnp.int32, spec.shape, 1) < spec.shape[1] // 2
    o_ref[...] = spec + jnp.where(is_real, d_ref[...], 0.0)


def _largest_divisor(n, cap):
    return max(d for d in range(1, cap + 1) if n % d == 0)


def filter_spectrum(gf, x, ss, d):
    nf = gf.shape[0]
    _, k2, c = x.shape
    ft = _largest_divisor(nf, 8)
    ct = min(256, c)
    return pl.pallas_call(
        _filter_spectrum_kernel,
        grid=(nf // ft, c // ct),
        in_specs=[pl.BlockSpec((ft, k2, k2), lambda i, j: (i, 0, 0)),
                  pl.BlockSpec((ft, k2, ct), lambda i, j: (i, 0, j)),
                  pl.BlockSpec((1, ct), lambda i, j: (0, j)),
                  pl.BlockSpec((1, ct), lambda i, j: (0, j))],
        out_specs=pl.BlockSpec((ft, k2, ct), lambda i, j: (i, 0, j)),
        out_shape=jax.ShapeDtypeStruct((nf, k2, c), F32),
        compiler_params=_cparams("parallel", "parallel"),
        name="filter_spectrum",
    )(gf, x, ss, d.reshape(1, c))


KRON = 8
STAGE_B_CHUNK = 16


def _hyena_kernel(fk_ref, ik_ref, gf_ref, gb_ref, h_ref, x0_ref, x1_ref, v_ref, ga_ref,
                  w0_ref, b0_ref, w1_ref, b1_ref, wv_ref, bv_ref, o_ref, seq_ref, za_ref, zb_ref):
    n1h, n2, ct = x1_ref.shape
    nf = za_ref.shape[0]
    z = _short_conv_block(x1_ref, w1_ref, b1_ref) * _short_conv_block(v_ref, wv_ref, bv_ref)
    seq_ref[...] = z.reshape(n1h, n2, ct)
    for oc in range(n2 // KRON):
        x8 = seq_ref[:, oc * KRON:(oc + 1) * KRON, :].reshape(n1h * KRON, ct).astype(BF16)
        y = jnp.dot(fk_ref[...], x8, preferred_element_type=F32)
        za_ref[:, oc * 2 * KRON:(oc + 1) * 2 * KRON, :] = y.reshape(nf, 2 * KRON, ct).astype(BF16)
    ft = _largest_divisor(nf, STAGE_B_CHUNK)
    for f0 in range(0, nf, ft):
        fs = slice(f0, f0 + ft)
        spec = _bmm(gf_ref[fs], za_ref[fs], False)
        sr, si = spec[:, :n2], spec[:, n2:]
        hr, hi = h_ref[fs, :n2], h_ref[fs, n2:]
        prod = jnp.concatenate([sr * hr - si * hi, sr * hi + si * hr], axis=1)
        zb_ref[fs] = _bmm(gb_ref[fs], prod, False).astype(BF16)
    gate = _silu(ga_ref[...].astype(F32).reshape(n1h * n2, ct))
    seq_ref[...] = (_short_conv_block(x0_ref, w0_ref, b0_ref) * gate).reshape(n1h, n2, ct)
    for oc in range(n2 // KRON):
        x8 = zb_ref[:, oc * 2 * KRON:(oc + 1) * 2 * KRON, :].reshape(nf * 2 * KRON, ct)
        y = jnp.dot(ik_ref[...], x8, preferred_element_type=F32).reshape(n1h, KRON, ct)
        sl = slice(oc * KRON, (oc + 1) * KRON)
        o_ref[:, sl, :] = (y * seq_ref[:, sl, :]).astype(o_ref.dtype)


def hyena_branch(proj, conv_w, conv_b, d, kc_raw, ss, bsz, seq, c):
    n2 = FFT_N2
    n1 = 2 * seq // n2
    n1h = n1 // 2
    nf = n1h + 1
    fa, ia, gf, gb = _dft_tables(seq)
    kspec = filter_stage_a(jnp.asarray(fa), kc_raw.reshape(n1, n2 * c))
    hspec = filter_spectrum(jnp.asarray(gf[:nf]), kspec.reshape(n1, 2 * n2, c), ss, d)
    k = np.arange(2 * n2)
    perm = (k // (2 * KRON)) * KRON + (k % KRON) + ((k // KRON) % 2) * n2
    eye = np.eye(KRON, dtype=np.float32)
    fk = jnp.asarray(np.kron(fa[:2 * nf, :n1h], eye), dtype=BF16)
    weight = np.repeat(np.where((np.arange(nf) == 0) | (np.arange(nf) == n1h), 1.0, 2.0), 2)
    ik = jnp.asarray(np.kron(ia[:, :2 * nf] * weight[None, :], eye), dtype=BF16)
    gfs = jnp.asarray(gf[:nf][:, :, perm], dtype=BF16)
    gbs = jnp.asarray(gb[:nf][:, perm, :], dtype=BF16)
    proj4 = proj.reshape(bsz, n1h, n2, proj.shape[1])
    cb = conv_b.reshape(1, -1)
    ct = min(256, c)
    nc = c // ct
    seq_blk = lambda group: pl.BlockSpec((None, n1h, n2, ct), lambda j, b: (b, 0, 0, group * nc + j))
    conv_blk = lambda group: [pl.BlockSpec((3, ct), lambda j, b: (0, group * nc + j)),
                              pl.BlockSpec((1, ct), lambda j, b: (0, group * nc + j))]
    ya = pl.pallas_call(
        _hyena_kernel,
        grid=(nc, bsz),
        in_specs=[_resident(fk.shape), _resident(ik.shape), _resident(gfs.shape), _resident(gbs.shape),
                  pl.BlockSpec((nf, 2 * n2, ct), lambda j, b: (0, 0, j), pipeline_mode=pl.Buffered(1)),
                  seq_blk(0), seq_blk(1), seq_blk(2), seq_blk(3)]
        + conv_blk(0) + conv_blk(1) + conv_blk(2),
        out_specs=pl.BlockSpec((None, n1h, n2, ct), lambda j, b: (b, 0, 0, j)),
        out_shape=jax.ShapeDtypeStruct((bsz, n1h, n2, c), BF16),
        scratch_shapes=[pltpu.VMEM((n1h, n2, ct), F32),
                        pltpu.VMEM((nf, 2 * n2, ct), BF16), pltpu.VMEM((nf, 2 * n2, ct), BF16)],
        compiler_params=_cparams("parallel", "arbitrary"),
        name="hyena_branch",
    )(fk, ik, gfs, gbs, hspec, proj4, proj4, proj4, proj4, conv_w, cb, conv_w, cb, conv_w, cb)
    return ya.reshape(bsz * seq, c)


def _gmlp_kernel(u_ref, v_ref, gb_ref, ng_ref, ws_ref, bias_ref, o_ref):
    vn = _rms(v_ref[...].astype(F32), ng_ref[...]).astype(BF16)
    tr, c = vn.shape
    gc = c // GM_GROUPS
    for n in range(tr // CHUNK):
        rows = slice(n * CHUNK, (n + 1) * CHUNK)
        for g in range(GM_GROUPS):
            cols = slice(g * gc, (g + 1) * gc)
            s = jnp.dot(ws_ref[g], vn[rows, cols], preferred_element_type=F32) + bias_ref[:, cols]
            y = u_ref[rows, cols].astype(F32) * s * _silu(gb_ref[rows, cols].astype(F32))
            o_ref[rows, cols] = y.astype(o_ref.dtype)


def gmlp_branch(proj, norm_g, ws, bs, c, col0):
    m = proj.shape[0]
    tr = min(512, m)
    off = col0 // c
    bias = jnp.repeat(bs.T, c // GM_GROUPS, axis=1)
    blk = lambda o: pl.BlockSpec((tr, c), lambda i: (i, off + o))
    return pl.pallas_call(
        _gmlp_kernel,
        grid=(m // tr,),
        in_specs=[blk(0), blk(1), blk(2),
                  pl.BlockSpec((1, c), lambda i: (0, 0)),
                  pl.BlockSpec(ws.shape, lambda i: (0, 0, 0)),
                  pl.BlockSpec(bias.shape, lambda i: (0, 0))],
        out_specs=pl.BlockSpec((tr, c), lambda i: (i, 0)),
        out_shape=jax.ShapeDtypeStruct((m, c), BF16),
        compiler_params=_cparams("parallel"),
        name="gmlp_branch",
    )(proj, proj, proj, norm_g.reshape(1, c), ws.astype(BF16), bias)


def _out_ple_kernel(ya_ref, yb_ref, h_ref, p_ref, wa_ref, wb_ref, pg_ref, gw_ref, up_ref, fg_ref, o_ref,
                    *, final):
    mix = (jnp.dot(ya_ref[...].astype(BF16), wa_ref[...], preferred_element_type=F32)
           + jnp.dot(yb_ref[...].astype(BF16), wb_ref[...], preferred_element_type=F32))
    h1 = h_ref[...] + mix
    r = _rms(h1, pg_ref[...]).astype(BF16)
    gate = jax.nn.sigmoid(jnp.dot(r, gw_ref[...], preferred_element_type=F32))
    up = jnp.dot(p_ref[...].astype(BF16), up_ref[...], preferred_element_type=F32)
    h2 = h1 + up * gate
    if final:
        h2 = _rms(h2, fg_ref[...])
    o_ref[...] = h2


def out_ple(ya, yb, h, p, w_out, ple_g, gate_w, ple_up, final_g, final):
    m, d = h.shape
    ca = ya.shape[1]
    tm = min(1024, m)
    rowblk = lambda a: pl.BlockSpec((tm, a.shape[1]), lambda i: (i, 0))
    full = lambda a: _resident(a.shape)
    wa = w_out[:ca].astype(BF16)
    wb = w_out[ca:].astype(BF16)
    args = (ya, yb, h, p, wa, wb, ple_g.reshape(1, d), gate_w.astype(BF16), ple_up.astype(BF16),
            final_g.reshape(1, d))
    return pl.pallas_call(
        functools.partial(_out_ple_kernel, final=final),
        grid=(m // tm,),
        in_specs=[rowblk(a) for a in args[:4]] + [full(a) for a in args[4:]],
        out_specs=pl.BlockSpec((tm, d), lambda i: (i, 0)),
        out_shape=jax.ShapeDtypeStruct((m, d), F32),
        compiler_params=_cparams("parallel"),
        name="out_ple",
    )(*args)


POOL_HALO = 16


def _pool_kernel(x_ref, prev_ref, next_ref, gc_ref, w_ref, b_ref, sc_ref, o_ref, pad_ref, *, seq):
    tr, c = x_ref.shape
    i = pl.program_id(0)
    t0 = (i * tr) % seq
    zeros = jnp.zeros((POOL_HALO, c), F32)
    pad_ref[0:POOL_HALO] = jnp.where(t0 == 0, zeros, prev_ref[...].astype(F32))
    pad_ref[POOL_HALO:POOL_HALO + tr] = x_ref[...].astype(F32)
    pad_ref[POOL_HALO + tr:] = jnp.where(t0 + tr == seq, zeros, next_ref[...].astype(F32))
    t = t0 + lax.broadcasted_iota(jnp.int32, (tr, 1), 0)
    gcw = c // len(POOL_WINDOWS)
    for g, w in enumerate(POOL_WINDOWS):
        cols = slice(g * gcw, (g + 1) * gcw)
        acc = pad_ref[POOL_HALO - w // 2:POOL_HALO - w // 2 + tr, cols]
        for o in range(-w // 2 + 1, w // 2):
            acc = acc + pad_ref[POOL_HALO + o:POOL_HALO + o + tr, cols]
        cnt = (jnp.minimum(t + w // 2, seq) - jnp.maximum(t - w // 2, 0)).astype(F32)
        dlt = acc / cnt - pad_ref[POOL_HALO:POOL_HALO + tr, cols]
        y = jnp.dot(dlt.astype(BF16), w_ref[g], preferred_element_type=F32) + b_ref[:, cols]
        o_ref[:, cols] = (y * sc_ref[:, cols] * _silu(gc_ref[:, cols].astype(F32))).astype(o_ref.dtype)


def pool_branch(proj, pool_w, pool_b, pool_scale, seq, c):
    m = proj.shape[0]
    tr = min(512, seq)
    hb = tr // POOL_HALO
    nblk = m // POOL_HALO
    return pl.pallas_call(
        functools.partial(_pool_kernel, seq=seq),
        grid=(m // tr,),
        in_specs=[pl.BlockSpec((tr, c), lambda i: (i, 0)),
                  pl.BlockSpec((POOL_HALO, c), lambda i: (jnp.maximum(i * hb - 1, 0), 0)),
                  pl.BlockSpec((POOL_HALO, c), lambda i: (jnp.minimum((i + 1) * hb, nblk - 1), 0)),
                  pl.BlockSpec((tr, c), lambda i: (i, 1)),
                  pl.BlockSpec(pool_w.shape, lambda i: (0, 0, 0)),
                  pl.BlockSpec((1, c), lambda i: (0, 0)),
                  pl.BlockSpec((1, c), lambda i: (0, 0))],
        out_specs=pl.BlockSpec((tr, c), lambda i: (i, 0)),
        out_shape=jax.ShapeDtypeStruct((m, c), BF16),
        scratch_shapes=[pltpu.VMEM((tr + 2 * POOL_HALO, c), F32)],
        compiler_params=_cparams("parallel"),
        name="pool_branch",
    )(proj, proj, proj, proj, pool_w.astype(BF16), pool_b.reshape(1, c), pool_scale.reshape(1, c))


NA_ROWS_PER_STEP = 16
NA_LOOKAHEAD = 3
LOG2E = math.log2(math.e)


def _natten_bias_table(rpb):
    nh = rpb.shape[0]
    q = np.arange(GRID_W)[:, None]
    kc = np.arange(GRID_W)[None, :]
    start = np.clip(q - NA_KW // 2, 0, GRID_W - NA_KW)
    inside = (kc >= start) & (kc < start + NA_KW)
    dc = kc - q + (NA_KW - 1)
    onehot = ((dc[None] == np.arange(2 * NA_KW - 1)[:, None, None]) & inside[None]).astype(np.float32)
    m = jnp.einsum('hrd,dqk->hrqk', rpb.astype(F32), jnp.asarray(onehot), precision=lax.Precision.HIGHEST)
    m = m * LOG2E + jnp.asarray(np.where(inside, 0.0, NEG_BIG).astype(np.float32))
    t = jnp.stack([m[:, NA_KH - 1 - o:2 * NA_KH - 1 - o] for o in range(NA_KH)], axis=1)
    t = t.reshape(nh // 2, 2, NA_KH, NA_KH, GRID_W, GRID_W)
    t = jnp.transpose(t, (0, 2, 1, 4, 3, 5))
    return t.reshape(nh // 2, NA_KH, 2 * GRID_W, NA_KH * GRID_W)


def _natten_kernel(q_ref, k_ref, v_ref, gd_ref, t_ref, o_ref, *, rows):
    rg = pl.program_id(2)
    hd = NA_HEAD_DIM
    lane = lax.broadcasted_iota(jnp.int32, (GRID_W, 2 * hd), 1)
    first = lane < hd
    scale = hd ** -0.5 * LOG2E
    nk = NA_KH * GRID_W

    def key_start(rr):
        r = rg * NA_ROWS_PER_STEP + rr
        rs = jnp.clip(r - NA_KH // 2, 0, rows - NA_KH)
        return r - rs, pl.multiple_of(rs * GRID_W, GRID_W)

    def scores(rr):
        off, k0 = key_start(rr)
        q2 = q_ref[rr * GRID_W:(rr + 1) * GRID_W, :].astype(F32) * scale
        qs = jnp.concatenate([jnp.where(first, q2, 0.0), jnp.where(first, 0.0, q2)], axis=0).astype(BF16)
        s = lax.dot_general(qs, k_ref[pl.ds(k0, nk), :], (((1,), (1,)), ((), ())),
                            preferred_element_type=F32)
        return s + t_ref[off]

    pending = [scores(rr) for rr in range(NA_LOOKAHEAD)]
    for rr in range(NA_ROWS_PER_STEP):
        if rr + NA_LOOKAHEAD < NA_ROWS_PER_STEP:
            pending.append(scores(rr + NA_LOOKAHEAD))
        s = pending.pop(0)
        _, k0 = key_start(rr)
        qrows = slice(rr * GRID_W, (rr + 1) * GRID_W)
        p = jnp.exp2(s - jnp.max(s, axis=-1, keepdims=True))
        den = jnp.sum(p, axis=-1, keepdims=True)
        o2 = jnp.dot(p.astype(BF16), v_ref[pl.ds(k0, nk), :], preferred_element_type=F32) / den
        att = jnp.where(first, o2[:GRID_W], o2[GRID_W:])
        o_ref[qrows, :] = (att * _silu(gd_ref[qrows, :].astype(F32))).astype(o_ref.dtype)


def natten_branch(proj, rpb, bsz, seq, c, col0):
    rows = seq // GRID_W
    assert rows >= NA_KH and rows % NA_ROWS_PER_STEP == 0
    table = _natten_bias_table(rpb)
    pw = 2 * NA_HEAD_DIM
    npair = c // pw
    off = col0 // pw
    tq = NA_ROWS_PER_STEP * GRID_W
    nrg = rows // NA_ROWS_PER_STEP
    qblk = lambda o: pl.BlockSpec((tq, pw), lambda b, hp, rg: (b * nrg + rg, off + o * npair + hp))
    kvblk = lambda o: pl.BlockSpec((seq, pw), lambda b, hp, rg: (b, off + o * npair + hp))
    return pl.pallas_call(
        functools.partial(_natten_kernel, rows=rows),
        grid=(bsz, npair, nrg),
        in_specs=[qblk(0), kvblk(1), kvblk(2), qblk(3),
                  pl.BlockSpec((None, NA_KH, 2 * GRID_W, NA_KH * GRID_W), lambda b, hp, rg: (hp, 0, 0, 0))],
        out_specs=pl.BlockSpec((tq, pw), lambda b, hp, rg: (b * nrg + rg, hp)),
        out_shape=jax.ShapeDtypeStruct((bsz * seq, c), BF16),
        compiler_params=_cparams("parallel", "parallel", "parallel"),
        name="natten_branch",
    )(proj, proj, proj, proj, table)


def even_layer_mix(h, norm_g, w_in, conv_w, conv_b, hy_w0, hy_b0, hy_w1, hy_b1, hy_w2, hy_b2, hy_wout,
                   hy_freq, hy_d, gm_norm_g, gm_ws, gm_bs, bsz, seq):
    c = h.shape[1]
    proj = norm_matmul(h, norm_g, w_in.astype(BF16))
    kc_raw, ss = hyena_filter(seq, c, hy_w0, hy_b0, hy_w1, hy_b1, hy_w2, hy_b2, hy_wout, hy_freq)
    ya = hyena_branch(proj, conv_w, conv_b, hy_d, kc_raw, ss, bsz, seq, c)
    yb = gmlp_branch(proj, gm_norm_g, gm_ws, gm_bs, c, 4 * c)
    return ya, yb


def odd_layer_mix(h, norm_g, w_in, pool_w, pool_b, pool_scale, rpb, bsz, seq):
    c = h.shape[1]
    proj = norm_matmul(h, norm_g, w_in.astype(BF16))
    yc = pool_branch(proj, pool_w, pool_b, pool_scale, seq, c)
    yd = natten_branch(proj, rpb, bsz, seq, c, 2 * c)
    return yc, yd


def kernel(x, p, norm_g, final_g, ev_w_in, ev_conv_w, ev_conv_b, hy_w0, hy_b0, hy_w1, hy_b1, hy_w2, hy_b2, hy_wout, hy_freq, hy_d, gm_norm_g, gm_ws, gm_bs, ev_w_out, od_w_in, pool_w, pool_b, pool_scale, na_rpb, od_w_out, ple_up, ple_gate_w, ple_g):
    bsz, seq, d = x.shape
    depth = p.shape[0]
    h = x.reshape(bsz * seq, d)
    for i in range(depth):
        j = i // 2
        if i % 2 == 0:
            ya, yb = even_layer_mix(h, norm_g[i], ev_w_in[j], ev_conv_w[j], ev_conv_b[j], hy_w0[j], hy_b0[j],
                                    hy_w1[j], hy_b1[j], hy_w2[j], hy_b2[j], hy_wout[j], hy_freq[j], hy_d[j],
                                    gm_norm_g[j], gm_ws[j], gm_bs[j], bsz, seq)
            w_out = ev_w_out[j]
        else:
            ya, yb = odd_layer_mix(h, norm_g[i], od_w_in[j], pool_w[j], pool_b[j], pool_scale[j], na_rpb[j],
                                   bsz, seq)
            w_out = od_w_out[j]
        h = out_ple(ya, yb, h, p[i].reshape(bsz * seq, -1), w_out, ple_g[i], ple_gate_w[i], ple_up[i],
                    final_g, final=(i == depth - 1))
    return h.reshape(bsz, seq, d)
```

```python
import functools
import math

import jax
import jax.numpy as jnp
import numpy as np
from jax import lax
from jax.experimental import pallas as pl
from jax.experimental.pallas import tpu as pltpu

F32 = jnp.float32
BF16 = jnp.bfloat16

EPS = 1e-6
GRID_W = 64
PLE_DIM = 256
HY_BANDS = 16
HY_FAST_DECAY = 0.3
HY_SLOW_DECAY = 1.5
HY_TARGET = 1e-2
GM_GROUPS = 8
CHUNK = 128
POOL_WINDOWS = (2, 4, 8, 16)
NA_HEADS = 16
NA_HEAD_DIM = 64
NA_KH = 8
NA_KW = 16
FFT_N2 = 64
NEG_BIG = -1e30
VMEM_LIMIT = 56 * 1024 * 1024


def _cparams(*sem):
    return pltpu.CompilerParams(dimension_semantics=sem, vmem_limit_bytes=VMEM_LIMIT)


def _silu(x):
    return x * jax.nn.sigmoid(x)


def _rms(x, g):
    return x * lax.rsqrt(jnp.mean(x * x, axis=-1, keepdims=True) + EPS) * g


def _resident(shape):
    return pl.BlockSpec(shape, lambda *_: (0,) * len(shape), pipeline_mode=pl.Buffered(1))


def _norm_matmul_kernel(h_ref, g_ref, w_ref, o_ref, *, tn):
    hn = _rms(h_ref[...], g_ref[...]).astype(BF16)
    for j in range(o_ref.shape[1] // tn):
        cols = slice(j * tn, (j + 1) * tn)
        o_ref[:, cols] = jnp.dot(hn, w_ref[:, cols], preferred_element_type=F32).astype(o_ref.dtype)


def norm_matmul(h, g, w):
    m, d = h.shape
    n = w.shape[1]
    tm = min(512, m)
    tn = min(512, n)
    return pl.pallas_call(
        functools.partial(_norm_matmul_kernel, tn=tn),
        grid=(m // tm,),
        in_specs=[pl.BlockSpec((tm, d), lambda i: (i, 0)),
                  _resident((1, d)),
                  _resident((d, n))],
        out_specs=pl.BlockSpec((tm, n), lambda i: (i, 0)),
        out_shape=jax.ShapeDtypeStruct((m, n), BF16),
        compiler_params=_cparams("parallel"),
        name="norm_matmul",
    )(h, g.reshape(1, d), w)


def _short_conv(x, w, b):
    n = x.shape[0]
    row = lax.broadcasted_iota(jnp.int32, x.shape, 0)
    xm = jnp.where(row == 0, 0.0, pltpu.roll(x, 1, 0))
    xp = jnp.where(row == n - 1, 0.0, pltpu.roll(x, n - 1, 0))
    return xm * w[0:1] + x * w[1:2] + xp * w[2:3] + b


def _short_conv_block(x_ref, w_ref, b_ref):
    n1h, n2, ct = x_ref.shape
    return _short_conv(x_ref[...].astype(F32).reshape(n1h * n2, ct), w_ref[...], b_ref[...])


def _positional_features(seq):
    f32 = np.float32
    t = np.linspace(0.0, 1.0, seq, dtype=f32)[:, None]
    ang = (f32(2.0 * math.pi) * np.arange(seq, dtype=f32)[:, None] / f32(seq)).astype(f32)
    bands = np.linspace(1e-4, HY_BANDS - 1, HY_BANDS, dtype=f32)[None, :]
    ba = (bands * ang).astype(f32)
    feats = np.concatenate([t, np.cos(ba), -np.sin(ba)], axis=-1).astype(f32)
    feats2 = np.concatenate([feats, feats[:1], feats[1:][::-1]], axis=0)
    pad = np.zeros((2 * seq, 128 - feats2.shape[1]), f32)
    return np.concatenate([feats2, pad], axis=1)


def _filter_kernel(feat_ref, w0_ref, b0_ref, w1_ref, b1_ref, w2_ref, b2_ref, wo_ref, fr_ref, dl_ref,
                   kc_ref, ss_ref, *, seq, tr):
    hp = lax.Precision.HIGHEST
    i = pl.program_id(0)
    fr = fr_ref[...]
    h = jnp.sin(fr * (jnp.dot(feat_ref[...], w0_ref[...], precision=hp, preferred_element_type=F32) + b0_ref[...]))
    h = jnp.sin(fr * (jnp.dot(h, w1_ref[...], precision=hp, preferred_element_type=F32) + b1_ref[...]))
    h = jnp.sin(fr * (jnp.dot(h, w2_ref[...], precision=hp, preferred_element_type=F32) + b2_ref[...]))
    k = jnp.dot(h, wo_ref[...], precision=hp, preferred_element_type=F32)
    s = i * tr + lax.broadcasted_iota(jnp.int32, (tr, 1), 0)
    lag = jnp.where(s < seq, s, 2 * seq - s).astype(F32)
    t = lag * (1.0 / (seq - 1))
    k = jnp.where(s == seq, 0.0, k * jnp.exp(-t * dl_ref[...]))
    kc_ref[...] = k

    @pl.when(i == 0)
    def _():
        ss_ref[...] = jnp.zeros_like(ss_ref)

    ss_ref[...] += jnp.sum(k * k, axis=0, keepdims=True)


def hyena_filter(seq, c, w0, b0, w1, b1, w2, b2, wout, freq):
    feats = jnp.asarray(_positional_features(seq))
    hid = w0.shape[1]
    w0p = jnp.zeros((128, hid), F32).at[:w0.shape[0]].set(w0)
    max_decay = math.log(HY_TARGET) / HY_FAST_DECAY
    min_decay = math.log(HY_TARGET) / HY_SLOW_DECAY
    deltas = jnp.asarray(np.abs(np.linspace(min_decay, max_decay, c, dtype=np.float32)))[None, :]
    tr = min(512, seq)
    half = seq // tr
    full = lambda a: pl.BlockSpec(a.shape, lambda i: (0,) * a.ndim)
    row = lambda a: a.reshape(1, -1)
    args = (feats, w0p, row(b0), w1, row(b1), w2, row(b2), wout, row(freq), deltas)
    specs = [full(a) for a in args[1:]]
    specs[6] = pl.BlockSpec((hid, c), lambda i: (0, i // half))
    return pl.pallas_call(
        functools.partial(_filter_kernel, seq=seq, tr=tr),
        grid=(2 * half,),
        in_specs=[pl.BlockSpec((tr, 128), lambda i: (i, 0))] + specs,
        out_specs=[pl.BlockSpec((tr, c), lambda i: (i, 0)), pl.BlockSpec((1, c), lambda i: (0, 0))],
        out_shape=[jax.ShapeDtypeStruct((2 * seq, c), F32), jax.ShapeDtypeStruct((1, c), F32)],
        compiler_params=_cparams("arbitrary"),
        name="hyena_filter",
    )(*args)


def _dft_tables(seq):
    n = 2 * seq
    n2 = FFT_N2
    n1 = n // n2
    f1 = np.arange(n1)[:, None]
    s1 = np.arange(n1)[None, :]
    a = 2.0 * np.pi * (f1 * s1 % n1) / n1
    fa = np.empty((2 * n1, n1), np.float64)
    fa[0::2] = np.cos(a)
    fa[1::2] = -np.sin(a)
    ia = np.empty((n1 // 2, 2 * n1), np.float64)
    at = a.T[: n1 // 2]
    ia[:, 0::2] = np.cos(at) / n
    ia[:, 1::2] = -np.sin(at) / n
    f1b = np.arange(n1)[:, None, None]
    f2 = np.arange(n2)[None, :, None]
    s2 = np.arange(n2)[None, None, :]
    ph = 2.0 * np.pi * ((s2 * f2 * n1 + s2 * f1b) % n) / n
    gr, gi = np.cos(ph), -np.sin(ph)
    gf = np.concatenate([np.concatenate([gr, -gi], axis=2),
                         np.concatenate([gi, gr], axis=2)], axis=1)
    er, ei = np.transpose(gr, (0, 2, 1)), -np.transpose(gi, (0, 2, 1))
    gb = np.concatenate([np.concatenate([er, -ei], axis=2),
                         np.concatenate([ei, er], axis=2)], axis=1)
    return fa.astype(np.float32), ia.astype(np.float32), gf.astype(np.float32), gb.astype(np.float32)


def _filter_stage_a_kernel(wh_ref, wl_ref, x_ref, o_ref):
    n1, n2, ct = x_ref.shape
    nf = o_ref.shape[0]
    for oc in range(n2 // KRON):
        x8 = x_ref[:, oc * KRON:(oc + 1) * KRON, :].reshape(n1 * KRON, ct)
        hi = x8.astype(BF16)
        lo = (x8 - hi.astype(F32)).astype(BF16)
        y = (jnp.dot(wh_ref[...], hi, preferred_element_type=F32)
             + jnp.dot(wh_ref[...], lo, preferred_element_type=F32)
             + jnp.dot(wl_ref[...], hi, preferred_element_type=F32))
        o_ref[:, oc * 2 * KRON:(oc + 1) * 2 * KRON, :] = y.reshape(nf, 2 * KRON, ct)


def filter_stage_a(w, x3):
    n1, n2, c = x3.shape
    nf = w.shape[0] // (2 * KRON)
    ct = min(256, c)
    wh = w.astype(BF16)
    wl = (w - wh.astype(F32)).astype(BF16)
    return pl.pallas_call(
        _filter_stage_a_kernel,
        grid=(c // ct,),
        in_specs=[_resident(wh.shape), _resident(wl.shape),
                  pl.BlockSpec((n1, n2, ct), lambda j: (0, 0, j))],
        out_specs=pl.BlockSpec((nf, 2 * n2, ct), lambda j: (0, 0, j)),
        out_shape=jax.ShapeDtypeStruct((nf, 2 * n2, c), F32),
        compiler_params=_cparams("parallel"),
        name="filter_stage_a",
    )(wh, wl, x3)


def _bmm(g, x, precise):
    dn = (((2,), (1,)), ((0,), (0,)))
    if precise:
        return lax.dot_general(g, x, dn, precision=lax.Precision.HIGHEST, preferred_element_type=F32)
    return lax.dot_general(g, x.astype(BF16), dn, preferred_element_type=F32)


def _filter_spectrum_kernel(g_ref, x_ref, ss_ref, d_ref, o_ref):
    spec = _bmm(g_ref[...], x_ref[...], True) * lax.rsqrt(ss_ref[...] + EPS)
    is_real = lax.broadcasted_iota(jnp.int32, spec.shape, 1) < spec.shape[1] // 2
    o_ref[...] = spec + jnp.where(is_real, d_ref[...], 0.0)


def _largest_divisor(n, cap):
    return max(d for d in range(1, cap + 1) if n % d == 0)


def filter_spectrum(gf, x, ss, d):
    nf = gf.shape[0]
    _, k2, c = x.shape
    ft = _largest_divisor(nf, 8)
    ct = min(256, c)
    return pl.pallas_call(
        _filter_spectrum_kernel,
        grid=(nf // ft, c // ct),
        in_specs=[pl.BlockSpec((ft, k2, k2), lambda i, j: (i, 0, 0)),
                  pl.BlockSpec((ft, k2, ct), lambda i, j: (i, 0, j)),
                  pl.BlockSpec((1, ct), lambda i, j: (0, j)),
                  pl.BlockSpec((1, ct), lambda i, j: (0, j))],
        out_specs=pl.BlockSpec((ft, k2, ct), lambda i, j: (i, 0, j)),
        out_shape=jax.ShapeDtypeStruct((nf, k2, c), F32),
        compiler_params=_cparams("parallel", "parallel"),
        name="filter_spectrum",
    )(gf, x, ss, d.reshape(1, c))


KRON = 8
STAGE_B_CHUNK = 16


def _hyena_kernel(fk_ref, ik_ref, gf_ref, gb_ref, h_ref, x0_ref, x1_ref, v_ref, ga_ref,
                  w0_ref, b0_ref, w1_ref, b1_ref, wv_ref, bv_ref, o_ref, seq_ref, za_ref, zb_ref):
    n1h, n2, ct = x1_ref.shape
    nf = za_ref.shape[0]
    z = _short_conv_block(x1_ref, w1_ref, b1_ref) * _short_conv_block(v_ref, wv_ref, bv_ref)
    seq_ref[...] = z.reshape(n1h, n2, ct)
    for oc in range(n2 // KRON):
        x8 = seq_ref[:, oc * KRON:(oc + 1) * KRON, :].reshape(n1h * KRON, ct).astype(BF16)
        y = jnp.dot(fk_ref[...], x8, preferred_element_type=F32)
        za_ref[:, oc * 2 * KRON:(oc + 1) * 2 * KRON, :] = y.reshape(nf, 2 * KRON, ct).astype(BF16)
    ft = _largest_divisor(nf, STAGE_B_CHUNK)
    for f0 in range(0, nf, ft):
        fs = slice(f0, f0 + ft)
        spec = _bmm(gf_ref[fs], za_ref[fs], False)
        sr, si = spec[:, :n2], spec[:, n2:]
        hr, hi = h_ref[fs, :n2], h_ref[fs, n2:]
        prod = jnp.concatenate([sr * hr - si * hi, sr * hi + si * hr], axis=1)
        zb_ref[fs] = _bmm(gb_ref[fs], prod, False).astype(BF16)
    gate = _silu(ga_ref[...].astype(F32).reshape(n1h * n2, ct))
    seq_ref[...] = (_short_conv_block(x0_ref, w0_ref, b0_ref) * gate).reshape(n1h, n2, ct)
    for oc in range(n2 // KRON):
        x8 = zb_ref[:, oc * 2 * KRON:(oc + 1) * 2 * KRON, :].reshape(nf * 2 * KRON, ct)
        y = jnp.dot(ik_ref[...], x8, preferred_element_type=F32).reshape(n1h, KRON, ct)
        sl = slice(oc * KRON, (oc + 1) * KRON)
        o_ref[:, sl, :] = (y * seq_ref[:, sl, :]).astype(o_ref.dtype)


def hyena_branch(proj, conv_w, conv_b, d, kc_raw, ss, bsz, seq, c):
    n2 = FFT_N2
    n1 = 2 * seq // n2
    n1h = n1 // 2
    nf = n1h + 1
    fa, ia, gf, gb = _dft_tables(seq)
    k = np.arange(2 * n2)
    perm = (k // (2 * KRON)) * KRON + (k % KRON) + ((k // KRON) % 2) * n2
    eye = np.eye(KRON, dtype=np.float32)
    kspec = filter_stage_a(jnp.asarray(np.kron(fa[:2 * nf], eye)), kc_raw.reshape(n1, n2, c))
    hspec = filter_spectrum(jnp.asarray(gf[:nf][:, :, perm]), kspec, ss, d)
    fk = jnp.asarray(np.kron(fa[:2 * nf, :n1h], eye), dtype=BF16)
    weight = np.repeat(np.where((np.arange(nf) == 0) | (np.arange(nf) == n1h), 1.0, 2.0), 2)
    ik = jnp.asarray(np.kron(ia[:, :2 * nf] * weight[None, :], eye), dtype=BF16)
    gfs = jnp.asarray(gf[:nf][:, :, perm], dtype=BF16)
    gbs = jnp.asarray(gb[:nf][:, perm, :], dtype=BF16)
    proj4 = proj.reshape(bsz, n1h, n2, proj.shape[1])
    cb = conv_b.reshape(1, -1)
    ct = min(256, c)
    nc = c // ct
    seq_blk = lambda group: pl.BlockSpec((None, n1h, n2, ct), lambda j, b: (b, 0, 0, group * nc + j))
    conv_blk = lambda group: [pl.BlockSpec((3, ct), lambda j, b: (0, group * nc + j)),
                              pl.BlockSpec((1, ct), lambda j, b: (0, group * nc + j))]
    ya = pl.pallas_call(
        _hyena_kernel,
        grid=(nc, bsz),
        in_specs=[_resident(fk.shape), _resident(ik.shape), _resident(gfs.shape), _resident(gbs.shape),
                  pl.BlockSpec((nf, 2 * n2, ct), lambda j, b: (0, 0, j), pipeline_mode=pl.Buffered(1)),
                  seq_blk(0), seq_blk(1), seq_blk(2), seq_blk(3)]
        + conv_blk(0) + conv_blk(1) + conv_blk(2),
        out_specs=pl.BlockSpec((None, n1h, n2, ct), lambda j, b: (b, 0, 0, j)),
        out_shape=jax.ShapeDtypeStruct((bsz, n1h, n2, c), BF16),
        scratch_shapes=[pltpu.VMEM((n1h, n2, ct), F32),
                        pltpu.VMEM((nf, 2 * n2, ct), BF16), pltpu.VMEM((nf, 2 * n2, ct), BF16)],
        compiler_params=_cparams("parallel", "arbitrary"),
        name="hyena_branch",
    )(fk, ik, gfs, gbs, hspec, proj4, proj4, proj4, proj4, conv_w, cb, conv_w, cb, conv_w, cb)
    return ya.reshape(bsz * seq, c)


def _gmlp_kernel(u_ref, v_ref, gb_ref, ng_ref, ws_ref, bias_ref, o_ref):
    vn = _rms(v_ref[...].astype(F32), ng_ref[...]).astype(BF16)
    tr, c = vn.shape
    gc = c // GM_GROUPS
    for n in range(tr // CHUNK):
        rows = slice(n * CHUNK, (n + 1) * CHUNK)
        for g in range(GM_GROUPS):
            cols = slice(g * gc, (g + 1) * gc)
            s = jnp.dot(ws_ref[g], vn[rows, cols], preferred_element_type=F32) + bias_ref[:, cols]
            y = u_ref[rows, cols].astype(F32) * s * _silu(gb_ref[rows, cols].astype(F32))
            o_ref[rows, cols] = y.astype(o_ref.dtype)


def gmlp_branch(proj, norm_g, ws, bs, c, col0):
    m = proj.shape[0]
    tr = min(512, m)
    off = col0 // c
    bias = jnp.repeat(bs.T, c // GM_GROUPS, axis=1)
    blk = lambda o: pl.BlockSpec((tr, c), lambda i: (i, off + o))
    return pl.pallas_call(
        _gmlp_kernel,
        grid=(m // tr,),
        in_specs=[blk(0), blk(1), blk(2),
                  pl.BlockSpec((1, c), lambda i: (0, 0)),
                  pl.BlockSpec(ws.shape, lambda i: (0, 0, 0)),
                  pl.BlockSpec(bias.shape, lambda i: (0, 0))],
        out_specs=pl.BlockSpec((tr, c), lambda i: (i, 0)),
        out_shape=jax.ShapeDtypeStruct((m, c), BF16),
        compiler_params=_cparams("parallel"),
        name="gmlp_branch",
    )(proj, proj, proj, norm_g.reshape(1, c), ws.astype(BF16), bias)


def _out_ple_kernel(ya_ref, yb_ref, h_ref, p_ref, wa_ref, wb_ref, pg_ref, gw_ref, up_ref, fg_ref, o_ref,
                    *, final):
    mix = (jnp.dot(ya_ref[...].astype(BF16), wa_ref[...], preferred_element_type=F32)
           + jnp.dot(yb_ref[...].astype(BF16), wb_ref[...], preferred_element_type=F32))
    h1 = h_ref[...] + mix
    r = _rms(h1, pg_ref[...]).astype(BF16)
    gate = jax.nn.sigmoid(jnp.dot(r, gw_ref[...], preferred_element_type=F32))
    up = jnp.dot(p_ref[...].astype(BF16), up_ref[...], preferred_element_type=F32)
    h2 = h1 + up * gate
    if final:
        h2 = _rms(h2, fg_ref[...])
    o_ref[...] = h2


def out_ple(ya, yb, h, p, w_out, ple_g, gate_w, ple_up, final_g, final):
    m, d = h.shape
    ca = ya.shape[1]
    tm = min(1024, m)
    rowblk = lambda a: pl.BlockSpec((tm, a.shape[1]), lambda i: (i, 0))
    full = lambda a: _resident(a.shape)
    wa = w_out[:ca].astype(BF16)
    wb = w_out[ca:].astype(BF16)
    args = (ya, yb, h, p, wa, wb, ple_g.reshape(1, d), gate_w.astype(BF16), ple_up.astype(BF16),
            final_g.reshape(1, d))
    return pl.pallas_call(
        functools.partial(_out_ple_kernel, final=final),
        grid=(m // tm,),
        in_specs=[rowblk(a) for a in args[:4]] + [full(a) for a in args[4:]],
        out_specs=pl.BlockSpec((tm, d), lambda i: (i, 0)),
        out_shape=jax.ShapeDtypeStruct((m, d), F32),
        compiler_params=_cparams("parallel"),
        name="out_ple",
    )(*args)


POOL_HALO = 16


def _pool_kernel(x_ref, prev_ref, next_ref, gc_ref, band_ref, top_ref, bot_ref, w_ref, b_ref, sc_ref, o_ref,
                 *, seq):
    tr, c = x_ref.shape
    hl = POOL_HALO
    i = pl.program_id(0)
    t0 = (i * tr) % seq
    zeros = jnp.zeros((hl, c), BF16)
    prev = jnp.where(t0 == 0, zeros, prev_ref[...])
    nxt = jnp.where(t0 + tr == seq, zeros, next_ref[...])
    t = t0 + lax.broadcasted_iota(jnp.int32, (tr, 1), 0)
    gcw = c // len(POOL_WINDOWS)
    for g, w in enumerate(POOL_WINDOWS):
        cols = slice(g * gcw, (g + 1) * gcw)
        acc = jnp.dot(band_ref[g], x_ref[:, cols], preferred_element_type=F32)
        top = acc[:hl] + jnp.dot(top_ref[g], prev[:, cols], preferred_element_type=F32)
        bot = acc[tr - hl:] + jnp.dot(bot_ref[g], nxt[:, cols], preferred_element_type=F32)
        acc = jnp.concatenate([top, acc[hl:tr - hl], bot], axis=0)
        cnt = (jnp.minimum(t + w // 2, seq) - jnp.maximum(t - w // 2, 0)).astype(F32)
        dlt = acc * (1.0 / cnt) - x_ref[:, cols].astype(F32)
        y = jnp.dot(dlt.astype(BF16), w_ref[g], preferred_element_type=F32) + b_ref[:, cols]
        o_ref[:, cols] = (y * sc_ref[:, cols] * _silu(gc_ref[:, cols].astype(F32))).astype(o_ref.dtype)


def _pool_bands(tr):
    hl = POOL_HALO
    ng = len(POOL_WINDOWS)
    band = np.zeros((ng, tr, tr), np.float32)
    top = np.zeros((ng, hl, hl), np.float32)
    bot = np.zeros((ng, hl, hl), np.float32)
    t = np.arange(tr)[:, None]
    s = np.arange(tr)[None, :]
    a = np.arange(hl)[:, None]
    j = np.arange(hl)[None, :]
    for g, w in enumerate(POOL_WINDOWS):
        band[g] = (s - t >= -(w // 2)) & (s - t < w // 2)
        top[g] = (j - hl) - a >= -(w // 2)
        bot[g] = (hl + j) - a < w // 2
    return band, top, bot


def pool_branch(proj, pool_w, pool_b, pool_scale, seq, c):
    m = proj.shape[0]
    tr = min(512, seq)
    hb = tr // POOL_HALO
    nblk = m // POOL_HALO
    band, top, bot = (jnp.asarray(a, dtype=BF16) for a in _pool_bands(tr))
    return pl.pallas_call(
        functools.partial(_pool_kernel, seq=seq),
        grid=(m // tr,),
        in_specs=[pl.BlockSpec((tr, c), lambda i: (i, 0)),
                  pl.BlockSpec((POOL_HALO, c), lambda i: (jnp.maximum(i * hb - 1, 0), 0)),
                  pl.BlockSpec((POOL_HALO, c), lambda i: (jnp.minimum((i + 1) * hb, nblk - 1), 0)),
                  pl.BlockSpec((tr, c), lambda i: (i, 1)),
                  _resident(band.shape), _resident(top.shape), _resident(bot.shape),
                  _resident(pool_w.shape), _resident((1, c)), _resident((1, c))],
        out_specs=pl.BlockSpec((tr, c), lambda i: (i, 0)),
        out_shape=jax.ShapeDtypeStruct((m, c), BF16),
        compiler_params=_cparams("parallel"),
        name="pool_branch",
    )(proj, proj, proj, proj, band, top, bot, pool_w.astype(BF16), pool_b.reshape(1, c),
      pool_scale.reshape(1, c))


NA_ROWS_PER_STEP = 16
NA_LOOKAHEAD = 3
LOG2E = math.log2(math.e)


def _natten_bias_table(rpb):
    nh = rpb.shape[0]
    q = np.arange(GRID_W)[:, None]
    kc = np.arange(GRID_W)[None, :]
    start = np.clip(q - NA_KW // 2, 0, GRID_W - NA_KW)
    inside = (kc >= start) & (kc < start + NA_KW)
    dc = kc - q + (NA_KW - 1)
    onehot = ((dc[None] == np.arange(2 * NA_KW - 1)[:, None, None]) & inside[None]).astype(np.float32)
    m = jnp.einsum('hrd,dqk->hrqk', rpb.astype(F32), jnp.asarray(onehot), precision=lax.Precision.HIGHEST)
    m = m * LOG2E + jnp.asarray(np.where(inside, 0.0, NEG_BIG).astype(np.float32))
    t = jnp.stack([m[:, NA_KH - 1 - o:2 * NA_KH - 1 - o] for o in range(NA_KH)], axis=1)
    t = t.reshape(nh // 2, 2, NA_KH, NA_KH, GRID_W, GRID_W)
    t = jnp.transpose(t, (0, 2, 1, 4, 3, 5))
    return t.reshape(nh // 2, NA_KH, 2 * GRID_W, NA_KH * GRID_W)


def _natten_kernel(q_ref, k_ref, v_ref, gd_ref, t_ref, o_ref, *, rows):
    rg = pl.program_id(2)
    hd = NA_HEAD_DIM
    lane = lax.broadcasted_iota(jnp.int32, (GRID_W, 2 * hd), 1)
    first = lane < hd
    scale = hd ** -0.5 * LOG2E
    nk = NA_KH * GRID_W

    def key_start(rr):
        r = rg * NA_ROWS_PER_STEP + rr
        rs = jnp.clip(r - NA_KH // 2, 0, rows - NA_KH)
        return r - rs, pl.multiple_of(rs * GRID_W, GRID_W)

    def scores(rr):
        off, k0 = key_start(rr)
        q2 = q_ref[rr * GRID_W:(rr + 1) * GRID_W, :].astype(F32) * scale
        qs = jnp.concatenate([jnp.where(first, q2, 0.0), jnp.where(first, 0.0, q2)], axis=0).astype(BF16)
        s = lax.dot_general(qs, k_ref[pl.ds(k0, nk), :], (((1,), (1,)), ((), ())),
                            preferred_element_type=F32)
        return s + t_ref[off]

    pending = [scores(rr) for rr in range(NA_LOOKAHEAD)]
    for rr in range(NA_ROWS_PER_STEP):
        if rr + NA_LOOKAHEAD < NA_ROWS_PER_STEP:
            pending.append(scores(rr + NA_LOOKAHEAD))
        s = pending.pop(0)
        _, k0 = key_start(rr)
        qrows = slice(rr * GRID_W, (rr + 1) * GRID_W)
        p = jnp.exp2(s - jnp.max(s, axis=-1, keepdims=True))
        den = jnp.sum(p, axis=-1, keepdims=True)
        o2 = jnp.dot(p.astype(BF16), v_ref[pl.ds(k0, nk), :], preferred_element_type=F32) / den
        att = jnp.where(first, o2[:GRID_W], o2[GRID_W:])
        o_ref[qrows, :] = (att * _silu(gd_ref[qrows, :].astype(F32))).astype(o_ref.dtype)


def natten_branch(proj, rpb, bsz, seq, c, col0):
    rows = seq // GRID_W
    assert rows >= NA_KH and rows % NA_ROWS_PER_STEP == 0
    table = _natten_bias_table(rpb)
    pw = 2 * NA_HEAD_DIM
    npair = c // pw
    off = col0 // pw
    tq = NA_ROWS_PER_STEP * GRID_W
    nrg = rows // NA_ROWS_PER_STEP
    qblk = lambda o: pl.BlockSpec((tq, pw), lambda b, hp, rg: (b * nrg + rg, off + o * npair + hp))
    kvblk = lambda o: pl.BlockSpec((seq, pw), lambda b, hp, rg: (b, off + o * npair + hp))
    return pl.pallas_call(
        functools.partial(_natten_kernel, rows=rows),
        grid=(bsz, npair, nrg),
        in_specs=[qblk(0), kvblk(1), kvblk(2), qblk(3),
                  pl.BlockSpec((None, NA_KH, 2 * GRID_W, NA_KH * GRID_W), lambda b, hp, rg: (hp, 0, 0, 0))],
        out_specs=pl.BlockSpec((tq, pw), lambda b, hp, rg: (b * nrg + rg, hp)),
        out_shape=jax.ShapeDtypeStruct((bsz * seq, c), BF16),
        compiler_params=_cparams("parallel", "parallel", "parallel"),
        name="natten_branch",
    )(proj, proj, proj, proj, table)


def even_layer_mix(h, norm_g, w_in, conv_w, conv_b, hy_w0, hy_b0, hy_w1, hy_b1, hy_w2, hy_b2, hy_wout,
                   hy_freq, hy_d, gm_norm_g, gm_ws, gm_bs, bsz, seq):
    c = h.shape[1]
    proj = norm_matmul(h, norm_g, w_in.astype(BF16))
    kc_raw, ss = hyena_filter(seq, c, hy_w0, hy_b0, hy_w1, hy_b1, hy_w2, hy_b2, hy_wout, hy_freq)
    ya = hyena_branch(proj, conv_w, conv_b, hy_d, kc_raw, ss, bsz, seq, c)
    yb = gmlp_branch(proj, gm_norm_g, gm_ws, gm_bs, c, 4 * c)
    return ya, yb


def odd_layer_mix(h, norm_g, w_in, pool_w, pool_b, pool_scale, rpb, bsz, seq):
    c = h.shape[1]
    proj = norm_matmul(h, norm_g, w_in.astype(BF16))
    yc = pool_branch(proj, pool_w, pool_b, pool_scale, seq, c)
    yd = natten_branch(proj, rpb, bsz, seq, c, 2 * c)
    return yc, yd


def kernel(x, p, norm_g, final_g, ev_w_in, ev_conv_w, ev_conv_b, hy_w0, hy_b0, hy_w1, hy_b1, hy_w2, hy_b2, hy_wout, hy_freq, hy_d, gm_norm_g, gm_ws, gm_bs, ev_w_out, od_w_in, pool_w, pool_b, pool_scale, na_rpb, od_w_out, ple_up, ple_gate_w, ple_g):
    bsz, seq, d = x.shape
    depth = p.shape[0]
    h = x.reshape(bsz * seq, d)
    for i in range(depth):
        j = i // 2
        if i % 2 == 0:
            ya, yb = even_layer_mix(h, norm_g[i], ev_w_in[j], ev_conv_w[j], ev_conv_b[j], hy_w0[j], hy_b0[j],
                                    hy_w1[j], hy_b1[j], hy_w2[j], hy_b2[j], hy_wout[j], hy_freq[j], hy_d[j],
                                    gm_norm_g[j], gm_ws[j], gm_bs[j], bsz, seq)
            w_out = ev_w_out[j]
        else:
            ya, yb = odd_layer_mix(h, norm_g[i], od_w_in[j], pool_w[j], pool_b[j], pool_scale[j], na_rpb[j],
                                   bsz, seq)
            w_out = od_w_out[j]
        h = out_ple(ya, yb, h, p[i].reshape(bsz * seq, -1), w_out, ple_g[i], ple_gate_w[i], ple_up[i],
                    final_g, final=(i == depth - 1))
    return h.reshape(bsz, seq, d)
```

```python
import functools
import math

import jax
import jax.numpy as jnp
import numpy as np
from jax import lax
from jax.experimental import pallas as pl
from jax.experimental.pallas import tpu as pltpu

F32 = jnp.float32
BF16 = jnp.bfloat16

EPS = 1e-6
GRID_W = 64
PLE_DIM = 256
HY_BANDS = 16
HY_FAST_DECAY = 0.3
HY_SLOW_DECAY = 1.5
HY_TARGET = 1e-2
GM_GROUPS = 8
CHUNK = 128
POOL_WINDOWS = (2, 4, 8, 16)
NA_HEADS = 16
NA_HEAD_DIM = 64
NA_KH = 8
NA_KW = 16
FFT_N2 = 64
NEG_BIG = -1e30
VMEM_LIMIT = 56 * 1024 * 1024


def _cparams(*sem):
    return pltpu.CompilerParams(dimension_semantics=sem, vmem_limit_bytes=VMEM_LIMIT)


def _silu(x):
    return x * jax.nn.sigmoid(x)


def _rms(x, g):
    return x * lax.rsqrt(jnp.mean(x * x, axis=-1, keepdims=True) + EPS) * g


def _resident(shape):
    return pl.BlockSpec(shape, lambda *_: (0,) * len(shape), pipeline_mode=pl.Buffered(1))


def _norm_matmul_kernel(h_ref, g_ref, w_ref, o_ref, *, tn):
    hn = _rms(h_ref[...], g_ref[...]).astype(BF16)
    for j in range(o_ref.shape[1] // tn):
        cols = slice(j * tn, (j + 1) * tn)
        o_ref[:, cols] = jnp.dot(hn, w_ref[:, cols], preferred_element_type=F32).astype(o_ref.dtype)


def norm_matmul(h, g, w):
    m, d = h.shape
    n = w.shape[1]
    tm = min(512, m)
    tn = min(512, n)
    return pl.pallas_call(
        functools.partial(_norm_matmul_kernel, tn=tn),
        grid=(m // tm,),
        in_specs=[pl.BlockSpec((tm, d), lambda i: (i, 0)),
                  _resident((1, d)),
                  _resident((d, n))],
        out_specs=pl.BlockSpec((tm, n), lambda i: (i, 0)),
        out_shape=jax.ShapeDtypeStruct((m, n), BF16),
        compiler_params=_cparams("parallel"),
        name="norm_matmul",
    )(h, g.reshape(1, d), w)


EDGE = 8


def _even_proj_kernel(h_ref, hp_ref, hx_ref, g_ref, w_ref, cw_ref, cb_ref, oa_ref, ob_ref, *, seq, tn):
    tm = h_ref.shape[0]
    c = oa_ref.shape[1] // 2
    t0 = (pl.program_id(0) * tm) % seq
    g = g_ref[...]
    hn = _rms(h_ref[...], g).astype(BF16)
    edge = _rms(jnp.concatenate([hp_ref[...], hx_ref[...]], axis=0), g).astype(BF16)
    row = lax.broadcasted_iota(jnp.int32, (tm, tn), 0)

    def proj(col0):
        return jnp.dot(hn, w_ref[:, col0:col0 + tn], preferred_element_type=F32)

    def conv_operands(group, j):
        col0 = group * c + j * tn
        return proj(col0), jnp.dot(edge, w_ref[:, col0:col0 + tn], preferred_element_type=F32), col0

    def conv(x, xe, col0):
        before = jnp.where(t0 == 0, 0.0, xe[EDGE - 1:EDGE])
        after = jnp.where(t0 + tm == seq, 0.0, xe[EDGE:EDGE + 1])
        xm = jnp.where(row == 0, before, pltpu.roll(x, 1, 0))
        xp = jnp.where(row == tm - 1, after, pltpu.roll(x, tm - 1, 0))
        w = cw_ref[:, col0:col0 + tn]
        return xm * w[0:1] + x * w[1:2] + xp * w[2:3] + cb_ref[:, col0:col0 + tn]

    nj = c // tn
    plain = list(range(3 * c // tn))
    for j in range(nj):
        ops = [conv_operands(group, j) for group in range(3)]
        gate = proj(3 * c + j * tn)
        for k in plain[j * len(plain) // nj:(j + 1) * len(plain) // nj]:
            ob_ref[:, k * tn:(k + 1) * tn] = proj(4 * c + k * tn).astype(ob_ref.dtype)
        oa_ref[:, j * tn:(j + 1) * tn] = (conv(*ops[0]) * _silu(gate)).astype(oa_ref.dtype)
        oa_ref[:, c + j * tn:c + (j + 1) * tn] = (conv(*ops[1]) * conv(*ops[2])).astype(oa_ref.dtype)


def even_proj(h, g, w, conv_w, conv_b, seq):
    m, d = h.shape
    c = w.shape[1] // 7
    tm = min(512, seq)
    tn = min(512, c)
    eb = tm // EDGE
    nblk = m // EDGE
    return pl.pallas_call(
        functools.partial(_even_proj_kernel, seq=seq, tn=tn),
        grid=(m // tm,),
        in_specs=[pl.BlockSpec((tm, d), lambda i: (i, 0)),
                  pl.BlockSpec((EDGE, d), lambda i: (jnp.maximum(i * eb - 1, 0), 0)),
                  pl.BlockSpec((EDGE, d), lambda i: (jnp.minimum((i + 1) * eb, nblk - 1), 0)),
                  _resident((1, d)), _resident(w.shape), _resident(conv_w.shape), _resident((1, 3 * c))],
        out_specs=[pl.BlockSpec((tm, 2 * c), lambda i: (i, 0)), pl.BlockSpec((tm, 3 * c), lambda i: (i, 0))],
        out_shape=[jax.ShapeDtypeStruct((m, 2 * c), BF16), jax.ShapeDtypeStruct((m, 3 * c), BF16)],
        compiler_params=_cparams("parallel"),
        name="even_proj",
    )(h, h, h, g.reshape(1, d), w, conv_w, conv_b.reshape(1, -1))


def _positional_features(seq):
    f32 = np.float32
    t = np.linspace(0.0, 1.0, seq, dtype=f32)[:, None]
    ang = (f32(2.0 * math.pi) * np.arange(seq, dtype=f32)[:, None] / f32(seq)).astype(f32)
    bands = np.linspace(1e-4, HY_BANDS - 1, HY_BANDS, dtype=f32)[None, :]
    ba = (bands * ang).astype(f32)
    feats = np.concatenate([t, np.cos(ba), -np.sin(ba)], axis=-1).astype(f32)
    feats2 = np.concatenate([feats, feats[:1], feats[1:][::-1]], axis=0)
    pad = np.zeros((2 * seq, 128 - feats2.shape[1]), f32)
    return np.concatenate([feats2, pad], axis=1)


def _filter_kernel(feat_ref, w0_ref, b0_ref, w1_ref, b1_ref, w2_ref, b2_ref, wo_ref, fr_ref, dl_ref,
                   kc_ref, ss_ref, *, seq, tr):
    hp = lax.Precision.HIGHEST
    i = pl.program_id(0)
    fr = fr_ref[...]
    h = jnp.sin(fr * (jnp.dot(feat_ref[...], w0_ref[...], precision=hp, preferred_element_type=F32) + b0_ref[...]))
    h = jnp.sin(fr * (jnp.dot(h, w1_ref[...], precision=hp, preferred_element_type=F32) + b1_ref[...]))
    h = jnp.sin(fr * (jnp.dot(h, w2_ref[...], precision=hp, preferred_element_type=F32) + b2_ref[...]))
    k = jnp.dot(h, wo_ref[...], precision=hp, preferred_element_type=F32)
    s = i * tr + lax.broadcasted_iota(jnp.int32, (tr, 1), 0)
    lag = jnp.where(s < seq, s, 2 * seq - s).astype(F32)
    t = lag * (1.0 / (seq - 1))
    k = jnp.where(s == seq, 0.0, k * jnp.exp(-t * dl_ref[...]))
    kc_ref[...] = k

    @pl.when(i == 0)
    def _():
        ss_ref[...] = jnp.zeros_like(ss_ref)

    ss_ref[...] += jnp.sum(k * k, axis=0, keepdims=True)


def hyena_filter(seq, c, w0, b0, w1, b1, w2, b2, wout, freq):
    feats = jnp.asarray(_positional_features(seq))
    hid = w0.shape[1]
    w0p = jnp.zeros((128, hid), F32).at[:w0.shape[0]].set(w0)
    max_decay = math.log(HY_TARGET) / HY_FAST_DECAY
    min_decay = math.log(HY_TARGET) / HY_SLOW_DECAY
    deltas = jnp.asarray(np.abs(np.linspace(min_decay, max_decay, c, dtype=np.float32)))[None, :]
    tr = min(512, seq)
    half = seq // tr
    full = lambda a: pl.BlockSpec(a.shape, lambda i: (0,) * a.ndim)
    row = lambda a: a.reshape(1, -1)
    args = (feats, w0p, row(b0), w1, row(b1), w2, row(b2), wout, row(freq), deltas)
    specs = [full(a) for a in args[1:]]
    specs[6] = pl.BlockSpec((hid, c), lambda i: (0, i // half))
    return pl.pallas_call(
        functools.partial(_filter_kernel, seq=seq, tr=tr),
        grid=(2 * half,),
        in_specs=[pl.BlockSpec((tr, 128), lambda i: (i, 0))] + specs,
        out_specs=[pl.BlockSpec((tr, c), lambda i: (i, 0)), pl.BlockSpec((1, c), lambda i: (0, 0))],
        out_shape=[jax.ShapeDtypeStruct((2 * seq, c), F32), jax.ShapeDtypeStruct((1, c), F32)],
        compiler_params=_cparams("arbitrary"),
        name="hyena_filter",
    )(*args)


def _dft_tables(seq):
    n = 2 * seq
    n2 = FFT_N2
    n1 = n // n2
    f1 = np.arange(n1)[:, None]
    s1 = np.arange(n1)[None, :]
    a = 2.0 * np.pi * (f1 * s1 % n1) / n1
    fa = np.empty((2 * n1, n1), np.float64)
    fa[0::2] = np.cos(a)
    fa[1::2] = -np.sin(a)
    ia = np.empty((n1 // 2, 2 * n1), np.float64)
    at = a.T[: n1 // 2]
    ia[:, 0::2] = np.cos(at) / n
    ia[:, 1::2] = -np.sin(at) / n
    f1b = np.arange(n1)[:, None, None]
    f2 = np.arange(n2)[None, :, None]
    s2 = np.arange(n2)[None, None, :]
    ph = 2.0 * np.pi * ((s2 * f2 * n1 + s2 * f1b) % n) / n
    gr, gi = np.cos(ph), -np.sin(ph)
    gf = np.concatenate([np.concatenate([gr, -gi], axis=2),
                         np.concatenate([gi, gr], axis=2)], axis=1)
    er, ei = np.transpose(gr, (0, 2, 1)), -np.transpose(gi, (0, 2, 1))
    gb = np.concatenate([np.concatenate([er, -ei], axis=2),
                         np.concatenate([ei, er], axis=2)], axis=1)
    return fa.astype(np.float32), ia.astype(np.float32), gf.astype(np.float32), gb.astype(np.float32)


def _filter_stage_a_kernel(wh_ref, wl_ref, x_ref, o_ref):
    n1, n2, ct = x_ref.shape
    nf = o_ref.shape[0]
    for oc in range(n2 // KRON):
        x8 = x_ref[:, oc * KRON:(oc + 1) * KRON, :].reshape(n1 * KRON, ct)
        hi = x8.astype(BF16)
        lo = (x8 - hi.astype(F32)).astype(BF16)
        y = (jnp.dot(wh_ref[...], hi, preferred_element_type=F32)
             + jnp.dot(wh_ref[...], lo, preferred_element_type=F32)
             + jnp.dot(wl_ref[...], hi, preferred_element_type=F32))
        o_ref[:, oc * 2 * KRON:(oc + 1) * 2 * KRON, :] = y.reshape(nf, 2 * KRON, ct)


def filter_stage_a(w, x3):
    n1, n2, c = x3.shape
    nf = w.shape[0] // (2 * KRON)
    ct = min(256, c)
    wh = w.astype(BF16)
    wl = (w - wh.astype(F32)).astype(BF16)
    return pl.pallas_call(
        _filter_stage_a_kernel,
        grid=(c // ct,),
        in_specs=[_resident(wh.shape), _resident(wl.shape),
                  pl.BlockSpec((n1, n2, ct), lambda j: (0, 0, j))],
        out_specs=pl.BlockSpec((nf, 2 * n2, ct), lambda j: (0, 0, j)),
        out_shape=jax.ShapeDtypeStruct((nf, 2 * n2, c), F32),
        compiler_params=_cparams("parallel"),
        name="filter_stage_a",
    )(wh, wl, x3)


def _bmm(g, x, precise):
    dn = (((2,), (1,)), ((0,), (0,)))
    if precise:
        return lax.dot_general(g, x, dn, precision=lax.Precision.HIGHEST, preferred_element_type=F32)
    return lax.dot_general(g, x.astype(BF16), dn, preferred_element_type=F32)


def _filter_spectrum_kernel(g_ref, x_ref, ss_ref, d_ref, o_ref):
    spec = _bmm(g_ref[...], x_ref[...], True) * lax.rsqrt(ss_ref[...] + EPS)
    is_real = lax.broadcasted_iota(jnp.int32, spec.shape, 1) < spec.shape[1] // 2
    o_ref[...] = spec + jnp.where(is_real, d_ref[...], 0.0)


def _largest_divisor(n, cap):
    return max(d for d in range(1, cap + 1) if n % d == 0)


def filter_spectrum(gf, x, ss, d):
    nf = gf.shape[0]
    _, k2, c = x.shape
    ft = _largest_divisor(nf, 8)
    ct = min(256, c)
    return pl.pallas_call(
        _filter_spectrum_kernel,
        grid=(nf // ft, c // ct),
        in_specs=[pl.BlockSpec((ft, k2, k2), lambda i, j: (i, 0, 0)),
                  pl.BlockSpec((ft, k2, ct), lambda i, j: (i, 0, j)),
                  pl.BlockSpec((1, ct), lambda i, j: (0, j)),
                  pl.BlockSpec((1, ct), lambda i, j: (0, j))],
        out_specs=pl.BlockSpec((ft, k2, ct), lambda i, j: (i, 0, j)),
        out_shape=jax.ShapeDtypeStruct((nf, k2, c), F32),
        compiler_params=_cparams("parallel", "parallel"),
        name="filter_spectrum",
    )(gf, x, ss, d.reshape(1, c))


KRON = 8
STAGE_B_CHUNK = 16


def _hyena_kernel(fk_ref, ik_ref, gf_ref, gb_ref, h_ref, xg_ref, z_ref, o_ref, seq_ref, za_ref, zb_ref):
    n1h, n2, ct = z_ref.shape
    nf = za_ref.shape[0]
    seq_ref[...] = z_ref[...].astype(F32)
    for oc in range(n2 // KRON):
        x8 = seq_ref[:, oc * KRON:(oc + 1) * KRON, :].reshape(n1h * KRON, ct).astype(BF16)
        y = jnp.dot(fk_ref[...], x8, preferred_element_type=F32)
        za_ref[:, oc * 2 * KRON:(oc + 1) * 2 * KRON, :] = y.reshape(nf, 2 * KRON, ct).astype(BF16)
    ft = _largest_divisor(nf, STAGE_B_CHUNK)
    for f0 in range(0, nf, ft):
        fs = slice(f0, f0 + ft)
        spec = _bmm(gf_ref[fs], za_ref[fs], False)
        sr, si = spec[:, :n2], spec[:, n2:]
        hr, hi = h_ref[fs, :n2], h_ref[fs, n2:]
        prod = jnp.concatenate([sr * hr - si * hi, sr * hi + si * hr], axis=1)
        zb_ref[fs] = _bmm(gb_ref[fs], prod, False).astype(BF16)
    seq_ref[...] = xg_ref[...].astype(F32)
    for oc in range(n2 // KRON):
        x8 = zb_ref[:, oc * 2 * KRON:(oc + 1) * 2 * KRON, :].reshape(nf * 2 * KRON, ct)
        y = jnp.dot(ik_ref[...], x8, preferred_element_type=F32).reshape(n1h, KRON, ct)
        sl = slice(oc * KRON, (oc + 1) * KRON)
        o_ref[:, sl, :] = (y * seq_ref[:, sl, :]).astype(o_ref.dtype)


def hyena_branch(xgz, d, kc_raw, ss, bsz, seq, c):
    n2 = FFT_N2
    n1 = 2 * seq // n2
    n1h = n1 // 2
    nf = n1h + 1
    fa, ia, gf, gb = _dft_tables(seq)
    k = np.arange(2 * n2)
    perm = (k // (2 * KRON)) * KRON + (k % KRON) + ((k // KRON) % 2) * n2
    eye = np.eye(KRON, dtype=np.float32)
    kspec = filter_stage_a(jnp.asarray(np.kron(fa[:2 * nf], eye)), kc_raw.reshape(n1, n2, c))
    hspec = filter_spectrum(jnp.asarray(gf[:nf][:, :, perm]), kspec, ss, d)
    fk = jnp.asarray(np.kron(fa[:2 * nf, :n1h], eye), dtype=BF16)
    weight = np.repeat(np.where((np.arange(nf) == 0) | (np.arange(nf) == n1h), 1.0, 2.0), 2)
    ik = jnp.asarray(np.kron(ia[:, :2 * nf] * weight[None, :], eye), dtype=BF16)
    gfs = jnp.asarray(gf[:nf][:, :, perm], dtype=BF16)
    gbs = jnp.asarray(gb[:nf][:, perm, :], dtype=BF16)
    xgz4 = xgz.reshape(bsz, n1h, n2, 2 * c)
    ct = min(256, c)
    nc = c // ct
    seq_blk = lambda group: pl.BlockSpec((None, n1h, n2, ct), lambda j, b: (b, 0, 0, group * nc + j))
    ya = pl.pallas_call(
        _hyena_kernel,
        grid=(nc, bsz),
        in_specs=[_resident(fk.shape), _resident(ik.shape), _resident(gfs.shape), _resident(gbs.shape),
                  pl.BlockSpec((nf, 2 * n2, ct), lambda j, b: (0, 0, j), pipeline_mode=pl.Buffered(1)),
                  seq_blk(0), seq_blk(1)],
        out_specs=pl.BlockSpec((None, n1h, n2, ct), lambda j, b: (b, 0, 0, j)),
        out_shape=jax.ShapeDtypeStruct((bsz, n1h, n2, c), BF16),
        scratch_shapes=[pltpu.VMEM((n1h, n2, ct), F32),
                        pltpu.VMEM((nf, 2 * n2, ct), BF16), pltpu.VMEM((nf, 2 * n2, ct), BF16)],
        compiler_params=_cparams("parallel", "arbitrary"),
        name="hyena_branch",
    )(fk, ik, gfs, gbs, hspec, xgz4, xgz4)
    return ya.reshape(bsz * seq, c)


def _gmlp_kernel(u_ref, v_ref, gb_ref, ng_ref, ws_ref, bias_ref, o_ref):
    vn = _rms(v_ref[...].astype(F32), ng_ref[...]).astype(BF16)
    tr, c = vn.shape
    gc = c // GM_GROUPS
    for n in range(tr // CHUNK):
        rows = slice(n * CHUNK, (n + 1) * CHUNK)
        for g in range(GM_GROUPS):
            cols = slice(g * gc, (g + 1) * gc)
            s = jnp.dot(ws_ref[g], vn[rows, cols], preferred_element_type=F32) + bias_ref[:, cols]
            y = u_ref[rows, cols].astype(F32) * s * _silu(gb_ref[rows, cols].astype(F32))
            o_ref[rows, cols] = y.astype(o_ref.dtype)


def gmlp_branch(proj, norm_g, ws, bs, c, col0):
    m = proj.shape[0]
    tr = min(512, m)
    off = col0 // c
    bias = jnp.repeat(bs.T, c // GM_GROUPS, axis=1)
    blk = lambda o: pl.BlockSpec((tr, c), lambda i: (i, off + o))
    return pl.pallas_call(
        _gmlp_kernel,
        grid=(m // tr,),
        in_specs=[blk(0), blk(1), blk(2),
                  pl.BlockSpec((1, c), lambda i: (0, 0)),
                  pl.BlockSpec(ws.shape, lambda i: (0, 0, 0)),
                  pl.BlockSpec(bias.shape, lambda i: (0, 0))],
        out_specs=pl.BlockSpec((tr, c), lambda i: (i, 0)),
        out_shape=jax.ShapeDtypeStruct((m, c), BF16),
        compiler_params=_cparams("parallel"),
        name="gmlp_branch",
    )(proj, proj, proj, norm_g.reshape(1, c), ws.astype(BF16), bias)


def _out_ple_kernel(ya_ref, yb_ref, h_ref, p_ref, wa_ref, wb_ref, pg_ref, gw_ref, up_ref, fg_ref, o_ref,
                    *, final):
    mix = (jnp.dot(ya_ref[...].astype(BF16), wa_ref[...], preferred_element_type=F32)
           + jnp.dot(yb_ref[...].astype(BF16), wb_ref[...], preferred_element_type=F32))
    h1 = h_ref[...] + mix
    r = _rms(h1, pg_ref[...]).astype(BF16)
    gate = jax.nn.sigmoid(jnp.dot(r, gw_ref[...], preferred_element_type=F32))
    up = jnp.dot(p_ref[...].astype(BF16), up_ref[...], preferred_element_type=F32)
    h2 = h1 + up * gate
    if final:
        h2 = _rms(h2, fg_ref[...])
    o_ref[...] = h2


def out_ple(ya, yb, h, p, w_out, ple_g, gate_w, ple_up, final_g, final):
    m, d = h.shape
    ca = ya.shape[1]
    tm = min(1024, m)
    rowblk = lambda a: pl.BlockSpec((tm, a.shape[1]), lambda i: (i, 0))
    full = lambda a: _resident(a.shape)
    wa = w_out[:ca].astype(BF16)
    wb = w_out[ca:].astype(BF16)
    args = (ya, yb, h, p, wa, wb, ple_g.reshape(1, d), gate_w.astype(BF16), ple_up.astype(BF16),
            final_g.reshape(1, d))
    return pl.pallas_call(
        functools.partial(_out_ple_kernel, final=final),
        grid=(m // tm,),
        in_specs=[rowblk(a) for a in args[:4]] + [full(a) for a in args[4:]],
        out_specs=pl.BlockSpec((tm, d), lambda i: (i, 0)),
        out_shape=jax.ShapeDtypeStruct((m, d), F32),
        compiler_params=_cparams("parallel"),
        name="out_ple",
    )(*args)


POOL_HALO = 16


def _pool_kernel(x_ref, prev_ref, next_ref, gc_ref, band_ref, top_ref, bot_ref, w_ref, b_ref, sc_ref, o_ref,
                 *, seq):
    tr, c = x_ref.shape
    hl = POOL_HALO
    i = pl.program_id(0)
    t0 = (i * tr) % seq
    zeros = jnp.zeros((hl, c), BF16)
    prev = jnp.where(t0 == 0, zeros, prev_ref[...])
    nxt = jnp.where(t0 + tr == seq, zeros, next_ref[...])
    t = t0 + lax.broadcasted_iota(jnp.int32, (tr, 1), 0)
    gcw = c // len(POOL_WINDOWS)
    for g, w in enumerate(POOL_WINDOWS):
        cols = slice(g * gcw, (g + 1) * gcw)
        acc = jnp.dot(band_ref[g], x_ref[:, cols], preferred_element_type=F32)
        top = acc[:hl] + jnp.dot(top_ref[g], prev[:, cols], preferred_element_type=F32)
        bot = acc[tr - hl:] + jnp.dot(bot_ref[g], nxt[:, cols], preferred_element_type=F32)
        acc = jnp.concatenate([top, acc[hl:tr - hl], bot], axis=0)
        cnt = (jnp.minimum(t + w // 2, seq) - jnp.maximum(t - w // 2, 0)).astype(F32)
        dlt = acc * (1.0 / cnt) - x_ref[:, cols].astype(F32)
        y = jnp.dot(dlt.astype(BF16), w_ref[g], preferred_element_type=F32) + b_ref[:, cols]
        o_ref[:, cols] = (y * sc_ref[:, cols] * _silu(gc_ref[:, cols].astype(F32))).astype(o_ref.dtype)


def _pool_bands(tr):
    hl = POOL_HALO
    ng = len(POOL_WINDOWS)
    band = np.zeros((ng, tr, tr), np.float32)
    top = np.zeros((ng, hl, hl), np.float32)
    bot = np.zeros((ng, hl, hl), np.float32)
    t = np.arange(tr)[:, None]
    s = np.arange(tr)[None, :]
    a = np.arange(hl)[:, None]
    j = np.arange(hl)[None, :]
    for g, w in enumerate(POOL_WINDOWS):
        band[g] = (s - t >= -(w // 2)) & (s - t < w // 2)
        top[g] = (j - hl) - a >= -(w // 2)
        bot[g] = (hl + j) - a < w // 2
    return band, top, bot


def pool_branch(proj, pool_w, pool_b, pool_scale, seq, c):
    m = proj.shape[0]
    tr = min(512, seq)
    hb = tr // POOL_HALO
    nblk = m // POOL_HALO
    band, top, bot = (jnp.asarray(a, dtype=BF16) for a in _pool_bands(tr))
    return pl.pallas_call(
        functools.partial(_pool_kernel, seq=seq),
        grid=(m // tr,),
        in_specs=[pl.BlockSpec((tr, c), lambda i: (i, 0)),
                  pl.BlockSpec((POOL_HALO, c), lambda i: (jnp.maximum(i * hb - 1, 0), 0)),
                  pl.BlockSpec((POOL_HALO, c), lambda i: (jnp.minimum((i + 1) * hb, nblk - 1), 0)),
                  pl.BlockSpec((tr, c), lambda i: (i, 1)),
                  _resident(band.shape), _resident(top.shape), _resident(bot.shape),
                  _resident(pool_w.shape), _resident((1, c)), _resident((1, c))],
        out_specs=pl.BlockSpec((tr, c), lambda i: (i, 0)),
        out_shape=jax.ShapeDtypeStruct((m, c), BF16),
        compiler_params=_cparams("parallel"),
        name="pool_branch",
    )(proj, proj, proj, proj, band, top, bot, pool_w.astype(BF16), pool_b.reshape(1, c),
      pool_scale.reshape(1, c))


NA_KBLK = 16
NA_BAND = 2
NA_ROWS_PER_STEP = 16
NA_LOOKAHEAD = 3
LOG2E = math.log2(math.e)


def _natten_bias_table(rpb):
    nh = rpb.shape[0]
    q = np.arange(GRID_W)[:, None]
    kc = np.arange(GRID_W)[None, :]
    start = np.clip(q - NA_KW // 2, 0, GRID_W - NA_KW)
    inside = (kc >= start) & (kc < start + NA_KW)
    dc = kc - q + (NA_KW - 1)
    onehot = ((dc[None] == np.arange(2 * NA_KW - 1)[:, None, None]) & inside[None]).astype(np.float32)
    m = jnp.einsum('hrd,dqk->hrqk', rpb.astype(F32), jnp.asarray(onehot), precision=lax.Precision.HIGHEST)
    m = m * LOG2E + jnp.asarray(np.where(inside, 0.0, NEG_BIG).astype(np.float32))
    t = jnp.stack([m[:, NA_KH - 1 - o:2 * NA_KH - 1 - o] for o in range(NA_KH)], axis=1)
    t = t.reshape(nh // 2, 2, NA_KH, NA_KH, GRID_W, GRID_W // NA_KBLK, NA_KBLK)
    t = jnp.transpose(t, (0, 2, 1, 4, 5, 3, 6))
    t = t.reshape(nh // 2, NA_KH, 2, GRID_W, NA_KH * GRID_W)
    lanes = NA_KH * NA_KBLK
    t = jnp.concatenate([t[:, :, :, qa:qb, b0 * lanes:(b0 + NA_BAND) * lanes] for qa, qb, b0 in _natten_rects()],
                        axis=3)
    return t.reshape(nh // 2, NA_KH, 2 * GRID_W, NA_BAND * lanes)


def _natten_rects():
    nblk = GRID_W // NA_KBLK
    rects = []
    for q0 in range(0, GRID_W, 8):
        lo = min(max(q0 - NA_KW // 2, 0), GRID_W - NA_KW)
        hi = min(max(q0 + 7 - NA_KW // 2, 0), GRID_W - NA_KW) + NA_KW
        b0 = min(lo // NA_KBLK, nblk - NA_BAND)
        assert (hi - 1) // NA_KBLK < b0 + NA_BAND
        if rects and rects[-1][2] == b0:
            rects[-1] = (rects[-1][0], q0 + 8, b0)
        else:
            rects.append((q0, q0 + 8, b0))
    return rects


def _natten_kernel(q_ref, k_ref, v_ref, gd_ref, t_ref, o_ref, *, rows):
    rg = pl.program_id(2)
    hd = NA_HEAD_DIM
    lane = lax.broadcasted_iota(jnp.int32, (GRID_W, 2 * hd), 1)
    first = lane < hd
    scale = hd ** -0.5 * LOG2E
    nblk = GRID_W // NA_KBLK
    lanes_per_blk = NA_KH * NA_KBLK
    boxes = [(hh * GRID_W + qa, hh * GRID_W + qb, b0) for hh in range(2) for qa, qb, b0 in _natten_rects()]

    def key_start(rr):
        r = rg * NA_ROWS_PER_STEP + rr
        rs = jnp.clip(r - NA_KH // 2, 0, rows - NA_KH)
        return r - rs, pl.multiple_of(rs * GRID_W, GRID_W)

    def key_block(ref, k0):
        blk = ref[pl.ds(k0, NA_KH * GRID_W), :]
        return jnp.concatenate([blk[j * GRID_W + b * NA_KBLK:j * GRID_W + (b + 1) * NA_KBLK]
                                for b in range(nblk) for j in range(NA_KH)], axis=0)

    def scores(rr):
        _, k0 = key_start(rr)
        q2 = q_ref[rr * GRID_W:(rr + 1) * GRID_W, :].astype(F32) * scale
        qs = jnp.concatenate([jnp.where(first, q2, 0.0), jnp.where(first, 0.0, q2)], axis=0).astype(BF16)
        return lax.dot_general(qs, key_block(k_ref, k0), (((1,), (1,)), ((), ())), preferred_element_type=F32)

    pending = [scores(rr) for rr in range(NA_LOOKAHEAD)]
    for rr in range(NA_ROWS_PER_STEP):
        if rr + NA_LOOKAHEAD < NA_ROWS_PER_STEP:
            pending.append(scores(rr + NA_LOOKAHEAD))
        s = pending.pop(0)
        off, k0 = key_start(rr)
        qrows = slice(rr * GRID_W, (rr + 1) * GRID_W)
        sb = jnp.concatenate([s[ra:rb, b0 * lanes_per_blk:(b0 + NA_BAND) * lanes_per_blk]
                              for ra, rb, b0 in boxes], axis=0) + t_ref[off]
        p = jnp.exp2(sb - jnp.max(sb, axis=-1, keepdims=True))
        den = jnp.sum(p, axis=-1, keepdims=True)
        p_rows = []
        for ra, rb, b0 in boxes:
            parts = [p[ra:rb]]
            if b0 > 0:
                parts.insert(0, jnp.zeros((rb - ra, b0 * lanes_per_blk), F32))
            if b0 + NA_BAND < nblk:
                parts.append(jnp.zeros((rb - ra, (nblk - b0 - NA_BAND) * lanes_per_blk), F32))
            p_rows.append(jnp.concatenate(parts, axis=1))
        pm = jnp.concatenate(p_rows, axis=0).astype(BF16)
        o2 = jnp.dot(pm, key_block(v_ref, k0), preferred_element_type=F32) / den
        att = jnp.where(first, o2[:GRID_W], o2[GRID_W:])
        o_ref[qrows, :] = (att * _silu(gd_ref[qrows, :].astype(F32))).astype(o_ref.dtype)


def natten_branch(proj, rpb, bsz, seq, c, col0):
    rows = seq // GRID_W
    assert rows >= NA_KH and rows % NA_ROWS_PER_STEP == 0
    table = _natten_bias_table(rpb)
    pw = 2 * NA_HEAD_DIM
    npair = c // pw
    off = col0 // pw
    tq = NA_ROWS_PER_STEP * GRID_W
    nrg = rows // NA_ROWS_PER_STEP
    qblk = lambda o: pl.BlockSpec((tq, pw), lambda b, hp, rg: (b * nrg + rg, off + o * npair + hp))
    kvblk = lambda o: pl.BlockSpec((seq, pw), lambda b, hp, rg: (b, off + o * npair + hp))
    return pl.pallas_call(
        functools.partial(_natten_kernel, rows=rows),
        grid=(bsz, npair, nrg),
        in_specs=[qblk(0), kvblk(1), kvblk(2), qblk(3),
                  pl.BlockSpec((None,) + table.shape[1:], lambda b, hp, rg: (hp, 0, 0, 0))],
        out_specs=pl.BlockSpec((tq, pw), lambda b, hp, rg: (b * nrg + rg, hp)),
        out_shape=jax.ShapeDtypeStruct((bsz * seq, c), BF16),
        compiler_params=_cparams("parallel", "parallel", "parallel"),
        name="natten_branch",
    )(proj, proj, proj, proj, table)


def even_layer_mix(h, norm_g, w_in, conv_w, conv_b, hy_w0, hy_b0, hy_w1, hy_b1, hy_w2, hy_b2, hy_wout,
                   hy_freq, hy_d, gm_norm_g, gm_ws, gm_bs, bsz, seq):
    c = h.shape[1]
    xgz, uvg = even_proj(h, norm_g, w_in.astype(BF16), conv_w, conv_b, seq)
    kc_raw, ss = hyena_filter(seq, c, hy_w0, hy_b0, hy_w1, hy_b1, hy_w2, hy_b2, hy_wout, hy_freq)
    ya = hyena_branch(xgz, hy_d, kc_raw, ss, bsz, seq, c)
    yb = gmlp_branch(uvg, gm_norm_g, gm_ws, gm_bs, c, 0)
    return ya, yb


def odd_layer_mix(h, norm_g, w_in, pool_w, pool_b, pool_scale, rpb, bsz, seq):
    c = h.shape[1]
    proj = norm_matmul(h, norm_g, w_in.astype(BF16))
    yc = pool_branch(proj, pool_w, pool_b, pool_scale, seq, c)
    yd = natten_branch(proj, rpb, bsz, seq, c, 2 * c)
    return yc, yd


def kernel(x, p, norm_g, final_g, ev_w_in, ev_conv_w, ev_conv_b, hy_w0, hy_b0, hy_w1, hy_b1, hy_w2, hy_b2, hy_wout, hy_freq, hy_d, gm_norm_g, gm_ws, gm_bs, ev_w_out, od_w_in, pool_w, pool_b, pool_scale, na_rpb, od_w_out, ple_up, ple_gate_w, ple_g):
    bsz, seq, d = x.shape
    depth = p.shape[0]
    h = x.reshape(bsz * seq, d)
    for i in range(depth):
        j = i // 2
        if i % 2 == 0:
            ya, yb = even_layer_mix(h, norm_g[i], ev_w_in[j], ev_conv_w[j], ev_conv_b[j], hy_w0[j], hy_b0[j],
                                    hy_w1[j], hy_b1[j], hy_w2[j], hy_b2[j], hy_wout[j], hy_freq[j], hy_d[j],
                                    gm_norm_g[j], gm_ws[j], gm_bs[j], bsz, seq)
            w_out = ev_w_out[j]
        else:
            ya, yb = odd_layer_mix(h, norm_g[i], od_w_in[j], pool_w[j], pool_b[j], pool_scale[j], na_rpb[j],
                                   bsz, seq)
            w_out = od_w_out[j]
        h = out_ple(ya, yb, h, p[i].reshape(bsz * seq, -1), w_out, ple_g[i], ple_gate_w[i], ple_up[i],
                    final_g, final=(i == depth - 1))
    return h.reshape(bsz, seq, d)
```

```python
import functools
import math

import jax
import jax.numpy as jnp
import numpy as np
from jax import lax
from jax.experimental import pallas as pl
from jax.experimental.pallas import tpu as pltpu

F32 = jnp.float32
BF16 = jnp.bfloat16

EPS = 1e-6
GRID_W = 64
PLE_DIM = 256
HY_BANDS = 16
HY_FAST_DECAY = 0.3
HY_SLOW_DECAY = 1.5
HY_TARGET = 1e-2
GM_GROUPS = 8
CHUNK = 128
POOL_WINDOWS = (2, 4, 8, 16)
NA_HEADS = 16
NA_HEAD_DIM = 64
NA_KH = 8
NA_KW = 16
FFT_N2 = 64
NEG_BIG = -1e30
VMEM_LIMIT = 56 * 1024 * 1024


def _cparams(*sem):
    return pltpu.CompilerParams(dimension_semantics=sem, vmem_limit_bytes=VMEM_LIMIT)


def _silu(x):
    return x * jax.nn.sigmoid(x)


def _rms(x, g):
    return x * lax.rsqrt(jnp.mean(x * x, axis=-1, keepdims=True) + EPS) * g


def _resident(shape):
    return pl.BlockSpec(shape, lambda *_: (0,) * len(shape), pipeline_mode=pl.Buffered(1))


def _norm_matmul_kernel(h_ref, g_ref, w_ref, o_ref, *, tn):
    hn = _rms(h_ref[...], g_ref[...]).astype(BF16)
    for j in range(o_ref.shape[1] // tn):
        cols = slice(j * tn, (j + 1) * tn)
        o_ref[:, cols] = jnp.dot(hn, w_ref[:, cols], preferred_element_type=F32).astype(o_ref.dtype)


def norm_matmul(h, g, w):
    m, d = h.shape
    n = w.shape[1]
    tm = min(512, m)
    tn = min(512, n)
    return pl.pallas_call(
        functools.partial(_norm_matmul_kernel, tn=tn),
        grid=(m // tm,),
        in_specs=[pl.BlockSpec((tm, d), lambda i: (i, 0)),
                  _resident((1, d)),
                  _resident((d, n))],
        out_specs=pl.BlockSpec((tm, n), lambda i: (i, 0)),
        out_shape=jax.ShapeDtypeStruct((m, n), BF16),
        compiler_params=_cparams("parallel"),
        name="norm_matmul",
    )(h, g.reshape(1, d), w)


def _short_conv(x, w, b):
    n = x.shape[0]
    row = lax.broadcasted_iota(jnp.int32, x.shape, 0)
    xm = jnp.where(row == 0, 0.0, pltpu.roll(x, 1, 0))
    xp = jnp.where(row == n - 1, 0.0, pltpu.roll(x, n - 1, 0))
    return xm * w[0:1] + x * w[1:2] + xp * w[2:3] + b


def _short_conv_block(x_ref, w_ref, b_ref):
    n1h, n2, ct = x_ref.shape
    return _short_conv(x_ref[...].astype(F32).reshape(n1h * n2, ct), w_ref[...], b_ref[...])


def _positional_features(seq):
    f32 = np.float32
    t = np.linspace(0.0, 1.0, seq, dtype=f32)[:, None]
    ang = (f32(2.0 * math.pi) * np.arange(seq, dtype=f32)[:, None] / f32(seq)).astype(f32)
    bands = np.linspace(1e-4, HY_BANDS - 1, HY_BANDS, dtype=f32)[None, :]
    ba = (bands * ang).astype(f32)
    feats = np.concatenate([t, np.cos(ba), -np.sin(ba)], axis=-1).astype(f32)
    feats2 = np.concatenate([feats, feats[:1], feats[1:][::-1]], axis=0)
    pad = np.zeros((2 * seq, 128 - feats2.shape[1]), f32)
    return np.concatenate([feats2, pad], axis=1)


def _filter_kernel(feat_ref, w0_ref, b0_ref, w1_ref, b1_ref, w2_ref, b2_ref, wo_ref, fr_ref, dl_ref,
                   kc_ref, ss_ref, *, seq, tr):
    hp = lax.Precision.HIGHEST
    i = pl.program_id(0)
    fr = fr_ref[...]
    h = jnp.sin(fr * (jnp.dot(feat_ref[...], w0_ref[...], precision=hp, preferred_element_type=F32) + b0_ref[...]))
    h = jnp.sin(fr * (jnp.dot(h, w1_ref[...], precision=hp, preferred_element_type=F32) + b1_ref[...]))
    h = jnp.sin(fr * (jnp.dot(h, w2_ref[...], precision=hp, preferred_element_type=F32) + b2_ref[...]))
    k = jnp.dot(h, wo_ref[...], precision=hp, preferred_element_type=F32)
    s = i * tr + lax.broadcasted_iota(jnp.int32, (tr, 1), 0)
    lag = jnp.where(s < seq, s, 2 * seq - s).astype(F32)
    t = lag * (1.0 / (seq - 1))
    k = jnp.where(s == seq, 0.0, k * jnp.exp(-t * dl_ref[...]))
    kc_ref[...] = k

    @pl.when(i == 0)
    def _():
        ss_ref[...] = jnp.zeros_like(ss_ref)

    ss_ref[...] += jnp.sum(k * k, axis=0, keepdims=True)


def hyena_filter(seq, c, w0, b0, w1, b1, w2, b2, wout, freq):
    feats = jnp.asarray(_positional_features(seq))
    hid = w0.shape[1]
    w0p = jnp.zeros((128, hid), F32).at[:w0.shape[0]].set(w0)
    max_decay = math.log(HY_TARGET) / HY_FAST_DECAY
    min_decay = math.log(HY_TARGET) / HY_SLOW_DECAY
    deltas = jnp.asarray(np.abs(np.linspace(min_decay, max_decay, c, dtype=np.float32)))[None, :]
    tr = min(512, seq)
    half = seq // tr
    full = lambda a: pl.BlockSpec(a.shape, lambda i: (0,) * a.ndim)
    row = lambda a: a.reshape(1, -1)
    args = (feats, w0p, row(b0), w1, row(b1), w2, row(b2), wout, row(freq), deltas)
    specs = [full(a) for a in args[1:]]
    specs[6] = pl.BlockSpec((hid, c), lambda i: (0, i // half))
    return pl.pallas_call(
        functools.partial(_filter_kernel, seq=seq, tr=tr),
        grid=(2 * half,),
        in_specs=[pl.BlockSpec((tr, 128), lambda i: (i, 0))] + specs,
        out_specs=[pl.BlockSpec((tr, c), lambda i: (i, 0)), pl.BlockSpec((1, c), lambda i: (0, 0))],
        out_shape=[jax.ShapeDtypeStruct((2 * seq, c), F32), jax.ShapeDtypeStruct((1, c), F32)],
        compiler_params=_cparams("arbitrary"),
        name="hyena_filter",
    )(*args)


def _dft_tables(seq):
    n = 2 * seq
    n2 = FFT_N2
    n1 = n // n2
    f1 = np.arange(n1)[:, None]
    s1 = np.arange(n1)[None, :]
    a = 2.0 * np.pi * (f1 * s1 % n1) / n1
    fa = np.empty((2 * n1, n1), np.float64)
    fa[0::2] = np.cos(a)
    fa[1::2] = -np.sin(a)
    ia = np.empty((n1 // 2, 2 * n1), np.float64)
    at = a.T[: n1 // 2]
    ia[:, 0::2] = np.cos(at) / n
    ia[:, 1::2] = -np.sin(at) / n
    f1b = np.arange(n1)[:, None, None]
    f2 = np.arange(n2)[None, :, None]
    s2 = np.arange(n2)[None, None, :]
    ph = 2.0 * np.pi * ((s2 * f2 * n1 + s2 * f1b) % n) / n
    gr, gi = np.cos(ph), -np.sin(ph)
    gf = np.concatenate([np.concatenate([gr, -gi], axis=2),
                         np.concatenate([gi, gr], axis=2)], axis=1)
    er, ei = np.transpose(gr, (0, 2, 1)), -np.transpose(gi, (0, 2, 1))
    gb = np.concatenate([np.concatenate([er, -ei], axis=2),
                         np.concatenate([ei, er], axis=2)], axis=1)
    return fa.astype(np.float32), ia.astype(np.float32), gf.astype(np.float32), gb.astype(np.float32)


def _filter_stage_a_kernel(wh_ref, wl_ref, x_ref, o_ref):
    n1, n2, ct = x_ref.shape
    nf = o_ref.shape[0]
    for oc in range(n2 // KRON):
        x8 = x_ref[:, oc * KRON:(oc + 1) * KRON, :].reshape(n1 * KRON, ct)
        hi = x8.astype(BF16)
        lo = (x8 - hi.astype(F32)).astype(BF16)
        y = (jnp.dot(wh_ref[...], hi, preferred_element_type=F32)
             + jnp.dot(wh_ref[...], lo, preferred_element_type=F32)
             + jnp.dot(wl_ref[...], hi, preferred_element_type=F32))
        o_ref[:, oc * 2 * KRON:(oc + 1) * 2 * KRON, :] = y.reshape(nf, 2 * KRON, ct)


def filter_stage_a(w, x3):
    n1, n2, c = x3.shape
    nf = w.shape[0] // (2 * KRON)
    ct = min(256, c)
    wh = w.astype(BF16)
    wl = (w - wh.astype(F32)).astype(BF16)
    return pl.pallas_call(
        _filter_stage_a_kernel,
        grid=(c // ct,),
        in_specs=[_resident(wh.shape), _resident(wl.shape),
                  pl.BlockSpec((n1, n2, ct), lambda j: (0, 0, j))],
        out_specs=pl.BlockSpec((nf, 2 * n2, ct), lambda j: (0, 0, j)),
        out_shape=jax.ShapeDtypeStruct((nf, 2 * n2, c), F32),
        compiler_params=_cparams("parallel"),
        name="filter_stage_a",
    )(wh, wl, x3)


def _bmm(g, x, precise):
    dn = (((2,), (1,)), ((0,), (0,)))
    if precise:
        return lax.dot_general(g, x, dn, precision=lax.Precision.HIGHEST, preferred_element_type=F32)
    return lax.dot_general(g, x.astype(BF16), dn, preferred_element_type=F32)


def _filter_spectrum_kernel(g_ref, x_ref, ss_ref, d_ref, o_ref):
    spec = _bmm(g_ref[...], x_ref[...], True) * lax.rsqrt(ss_ref[...] + EPS)
    is_real = lax.broadcasted_iota(jnp.int32, spec.shape, 1) < spec.shape[1] // 2
    o_ref[...] = spec + jnp.where(is_real, d_ref[...], 0.0)


def _largest_divisor(n, cap):
    return max(d for d in range(1, cap + 1) if n % d == 0)


def filter_spectrum(gf, x, ss, d):
    nf = gf.shape[0]
    _, k2, c = x.shape
    ft = _largest_divisor(nf, 8)
    ct = min(256, c)
    return pl.pallas_call(
        _filter_spectrum_kernel,
        grid=(nf // ft, c // ct),
        in_specs=[pl.BlockSpec((ft, k2, k2), lambda i, j: (i, 0, 0)),
                  pl.BlockSpec((ft, k2, ct), lambda i, j: (i, 0, j)),
                  pl.BlockSpec((1, ct), lambda i, j: (0, j)),
                  pl.BlockSpec((1, ct), lambda i, j: (0, j))],
        out_specs=pl.BlockSpec((ft, k2, ct), lambda i, j: (i, 0, j)),
        out_shape=jax.ShapeDtypeStruct((nf, k2, c), F32),
        compiler_params=_cparams("parallel", "parallel"),
        name="filter_spectrum",
    )(gf, x, ss, d.reshape(1, c))


KRON = 8
STAGE_B_CHUNK = 16


def _hyena_kernel(fk_ref, ik_ref, gf_ref, gb_ref, h_ref, x0_ref, x1_ref, v_ref, ga_ref,
                  w0_ref, b0_ref, w1_ref, b1_ref, wv_ref, bv_ref, o_ref, seq_ref, xg_ref, za_ref, zb_ref):
    n1h, n2, ct = x1_ref.shape
    nf = za_ref.shape[0]
    z = _short_conv_block(x1_ref, w1_ref, b1_ref) * _short_conv_block(v_ref, wv_ref, bv_ref)
    seq_ref[...] = z.reshape(n1h, n2, ct)
    gate = _silu(ga_ref[...].astype(F32).reshape(n1h * n2, ct))
    xg_ref[...] = (_short_conv_block(x0_ref, w0_ref, b0_ref) * gate).reshape(n1h, n2, ct)
    for oc in range(n2 // KRON):
        x8 = seq_ref[:, oc * KRON:(oc + 1) * KRON, :].reshape(n1h * KRON, ct).astype(BF16)
        y = jnp.dot(fk_ref[...], x8, preferred_element_type=F32)
        za_ref[:, oc * 2 * KRON:(oc + 1) * 2 * KRON, :] = y.reshape(nf, 2 * KRON, ct).astype(BF16)
    ft = _largest_divisor(nf, STAGE_B_CHUNK)
    for f0 in range(0, nf, ft):
        fs = slice(f0, f0 + ft)
        spec = _bmm(gf_ref[fs], za_ref[fs], False)
        sr, si = spec[:, :n2], spec[:, n2:]
        hr, hi = h_ref[fs, :n2], h_ref[fs, n2:]
        prod = jnp.concatenate([sr * hr - si * hi, sr * hi + si * hr], axis=1)
        zb_ref[fs] = _bmm(gb_ref[fs], prod, False).astype(BF16)
    for oc in range(n2 // KRON):
        x8 = zb_ref[:, oc * 2 * KRON:(oc + 1) * 2 * KRON, :].reshape(nf * 2 * KRON, ct)
        y = jnp.dot(ik_ref[...], x8, preferred_element_type=F32).reshape(n1h, KRON, ct)
        sl = slice(oc * KRON, (oc + 1) * KRON)
        o_ref[:, sl, :] = (y * xg_ref[:, sl, :]).astype(o_ref.dtype)


def hyena_branch(proj, conv_w, conv_b, d, kc_raw, ss, bsz, seq, c):
    n2 = FFT_N2
    n1 = 2 * seq // n2
    n1h = n1 // 2
    nf = n1h + 1
    fa, ia, gf, gb = _dft_tables(seq)
    k = np.arange(2 * n2)
    perm = (k // (2 * KRON)) * KRON + (k % KRON) + ((k // KRON) % 2) * n2
    eye = np.eye(KRON, dtype=np.float32)
    kspec = filter_stage_a(jnp.asarray(np.kron(fa[:2 * nf], eye)), kc_raw.reshape(n1, n2, c))
    hspec = filter_spectrum(jnp.asarray(gf[:nf][:, :, perm]), kspec, ss, d)
    fk = jnp.asarray(np.kron(fa[:2 * nf, :n1h], eye), dtype=BF16)
    weight = np.repeat(np.where((np.arange(nf) == 0) | (np.arange(nf) == n1h), 1.0, 2.0), 2)
    ik = jnp.asarray(np.kron(ia[:, :2 * nf] * weight[None, :], eye), dtype=BF16)
    gfs = jnp.asarray(gf[:nf][:, :, perm], dtype=BF16)
    gbs = jnp.asarray(gb[:nf][:, perm, :], dtype=BF16)
    proj4 = proj.reshape(bsz, n1h, n2, proj.shape[1])
    cb = conv_b.reshape(1, -1)
    ct = min(256, c)
    nc = c // ct
    seq_blk = lambda group: pl.BlockSpec((None, n1h, n2, ct), lambda j, b: (b, 0, 0, group * nc + j))
    conv_blk = lambda group: [pl.BlockSpec((3, ct), lambda j, b: (0, group * nc + j)),
                              pl.BlockSpec((1, ct), lambda j, b: (0, group * nc + j))]
    ya = pl.pallas_call(
        _hyena_kernel,
        grid=(nc, bsz),
        in_specs=[_resident(fk.shape), _resident(ik.shape), _resident(gfs.shape), _resident(gbs.shape),
                  pl.BlockSpec((nf, 2 * n2, ct), lambda j, b: (0, 0, j), pipeline_mode=pl.Buffered(1)),
                  seq_blk(0), seq_blk(1), seq_blk(2), seq_blk(3)]
        + conv_blk(0) + conv_blk(1) + conv_blk(2),
        out_specs=pl.BlockSpec((None, n1h, n2, ct), lambda j, b: (b, 0, 0, j)),
        out_shape=jax.ShapeDtypeStruct((bsz, n1h, n2, c), BF16),
        scratch_shapes=[pltpu.VMEM((n1h, n2, ct), F32), pltpu.VMEM((n1h, n2, ct), F32),
                        pltpu.VMEM((nf, 2 * n2, ct), BF16), pltpu.VMEM((nf, 2 * n2, ct), BF16)],
        compiler_params=_cparams("parallel", "arbitrary"),
        name="hyena_branch",
    )(fk, ik, gfs, gbs, hspec, proj4, proj4, proj4, proj4, conv_w, cb, conv_w, cb, conv_w, cb)
    return ya.reshape(bsz * seq, c)


def _gmlp_kernel(u_ref, v_ref, gb_ref, ng_ref, ws_ref, bias_ref, o_ref):
    vn = _rms(v_ref[...].astype(F32), ng_ref[...]).astype(BF16)
    tr, c = vn.shape
    gc = c // GM_GROUPS
    for n in range(tr // CHUNK):
        rows = slice(n * CHUNK, (n + 1) * CHUNK)
        for g in range(GM_GROUPS):
            cols = slice(g * gc, (g + 1) * gc)
            s = jnp.dot(ws_ref[g], vn[rows, cols], preferred_element_type=F32) + bias_ref[:, cols]
            y = u_ref[rows, cols].astype(F32) * s * _silu(gb_ref[rows, cols].astype(F32))
            o_ref[rows, cols] = y.astype(o_ref.dtype)


def _out_ple_tail(ya_ref, yb_ref, h_ref, p_ref, wa_ref, wb_ref, pg_ref, gw_ref, up_ref, fg_ref, o_ref, final):
    mix = (jnp.dot(ya_ref[...].astype(BF16), wa_ref[...], preferred_element_type=F32)
           + jnp.dot(yb_ref[...].astype(BF16), wb_ref[...], preferred_element_type=F32))
    h1 = h_ref[...] + mix
    r = _rms(h1, pg_ref[...]).astype(BF16)
    gate = jax.nn.sigmoid(jnp.dot(r, gw_ref[...], preferred_element_type=F32))
    up = jnp.dot(p_ref[...].astype(BF16), up_ref[...], preferred_element_type=F32)
    h2 = h1 + up * gate
    if final:
        h2 = _rms(h2, fg_ref[...])
    o_ref[...] = h2


def _out_ple_kernel(ya_ref, yb_ref, *rest, final):
    _out_ple_tail(ya_ref, yb_ref, *rest, final)


def _out_ple_gmlp_kernel(ya_ref, u_ref, v_ref, gb_ref, ng_ref, ws_ref, bias_ref, *rest, final):
    *tail, yb_ref = rest
    _gmlp_kernel(u_ref, v_ref, gb_ref, ng_ref, ws_ref, bias_ref, yb_ref)
    _out_ple_tail(ya_ref, yb_ref, *tail, final)


def out_ple(ya, yb, h, p, w_out, ple_g, gate_w, ple_up, final_g, final, gmlp=None):
    m, d = h.shape
    ca = ya.shape[1]
    tm = min(1024, m)
    rowblk = lambda a: pl.BlockSpec((tm, a.shape[1]), lambda i: (i, 0))
    full = lambda a: _resident(a.shape)
    wa = w_out[:ca].astype(BF16)
    wb = w_out[ca:].astype(BF16)
    tail = (h, p, wa, wb, ple_g.reshape(1, d), gate_w.astype(BF16), ple_up.astype(BF16), final_g.reshape(1, d))
    tail_specs = [rowblk(a) for a in tail[:2]] + [full(a) for a in tail[2:]]
    if gmlp is None:
        kern, args, specs, scratch = _out_ple_kernel, (ya, yb), [rowblk(ya), rowblk(yb)], []
    else:
        proj, col0, norm_g, ws, bs = gmlp
        c = (proj.shape[1] - col0) // 3
        off = col0 // c
        bias = jnp.repeat(bs.T, c // GM_GROUPS, axis=1)
        blk = lambda o: pl.BlockSpec((tm, c), lambda i: (i, off + o))
        extra = (norm_g.reshape(1, c), ws.astype(BF16), bias)
        kern, args = _out_ple_gmlp_kernel, (ya, proj, proj, proj) + extra
        specs = [rowblk(ya), blk(0), blk(1), blk(2)] + [full(a) for a in extra]
        scratch = [pltpu.VMEM((tm, c), BF16)]
    return pl.pallas_call(
        functools.partial(kern, final=final),
        grid=(m // tm,),
        in_specs=specs + tail_specs,
        out_specs=pl.BlockSpec((tm, d), lambda i: (i, 0)),
        out_shape=jax.ShapeDtypeStruct((m, d), F32),
        scratch_shapes=scratch,
        compiler_params=_cparams("parallel"),
        name="out_ple",
    )(*args, *tail)


POOL_HALO = 16


def _pool_kernel(x_ref, prev_ref, next_ref, gc_ref, band_ref, top_ref, bot_ref, w_ref, b_ref, sc_ref, o_ref,
                 *, seq):
    tr, c = x_ref.shape
    hl = POOL_HALO
    i = pl.program_id(0)
    t0 = (i * tr) % seq
    zeros = jnp.zeros((hl, c), BF16)
    prev = jnp.where(t0 == 0, zeros, prev_ref[...])
    nxt = jnp.where(t0 + tr == seq, zeros, next_ref[...])
    t = t0 + lax.broadcasted_iota(jnp.int32, (tr, 1), 0)
    gcw = c // len(POOL_WINDOWS)
    for g, w in enumerate(POOL_WINDOWS):
        cols = slice(g * gcw, (g + 1) * gcw)
        acc = jnp.dot(band_ref[g], x_ref[:, cols], preferred_element_type=F32)
        top = acc[:hl] + jnp.dot(top_ref[g], prev[:, cols], preferred_element_type=F32)
        bot = acc[tr - hl:] + jnp.dot(bot_ref[g], nxt[:, cols], preferred_element_type=F32)
        acc = jnp.concatenate([top, acc[hl:tr - hl], bot], axis=0)
        cnt = (jnp.minimum(t + w // 2, seq) - jnp.maximum(t - w // 2, 0)).astype(F32)
        dlt = acc * (1.0 / cnt) - x_ref[:, cols].astype(F32)
        y = jnp.dot(dlt.astype(BF16), w_ref[g], preferred_element_type=F32) + b_ref[:, cols]
        o_ref[:, cols] = (y * sc_ref[:, cols] * _silu(gc_ref[:, cols].astype(F32))).astype(o_ref.dtype)


def _pool_bands(tr):
    hl = POOL_HALO
    ng = len(POOL_WINDOWS)
    band = np.zeros((ng, tr, tr), np.float32)
    top = np.zeros((ng, hl, hl), np.float32)
    bot = np.zeros((ng, hl, hl), np.float32)
    t = np.arange(tr)[:, None]
    s = np.arange(tr)[None, :]
    a = np.arange(hl)[:, None]
    j = np.arange(hl)[None, :]
    for g, w in enumerate(POOL_WINDOWS):
        band[g] = (s - t >= -(w // 2)) & (s - t < w // 2)
        top[g] = (j - hl) - a >= -(w // 2)
        bot[g] = (hl + j) - a < w // 2
    return band, top, bot


def pool_branch(proj, pool_w, pool_b, pool_scale, seq, c):
    m = proj.shape[0]
    tr = min(512, seq)
    hb = tr // POOL_HALO
    nblk = m // POOL_HALO
    band, top, bot = (jnp.asarray(a, dtype=BF16) for a in _pool_bands(tr))
    return pl.pallas_call(
        functools.partial(_pool_kernel, seq=seq),
        grid=(m // tr,),
        in_specs=[pl.BlockSpec((tr, c), lambda i: (i, 0)),
                  pl.BlockSpec((POOL_HALO, c), lambda i: (jnp.maximum(i * hb - 1, 0), 0)),
                  pl.BlockSpec((POOL_HALO, c), lambda i: (jnp.minimum((i + 1) * hb, nblk - 1), 0)),
                  pl.BlockSpec((tr, c), lambda i: (i, 1)),
                  _resident(band.shape), _resident(top.shape), _resident(bot.shape),
                  _resident(pool_w.shape), _resident((1, c)), _resident((1, c))],
        out_specs=pl.BlockSpec((tr, c), lambda i: (i, 0)),
        out_shape=jax.ShapeDtypeStruct((m, c), BF16),
        compiler_params=_cparams("parallel"),
        name="pool_branch",
    )(proj, proj, proj, proj, band, top, bot, pool_w.astype(BF16), pool_b.reshape(1, c),
      pool_scale.reshape(1, c))


NA_KBLK = 16
NA_BAND = 2
NA_ROWS_PER_STEP = 32
NA_LOOKAHEAD = 3
LOG2E = math.log2(math.e)


def _natten_bias_table(rpb):
    nh = rpb.shape[0]
    q = np.arange(GRID_W)[:, None]
    kc = np.arange(GRID_W)[None, :]
    start = np.clip(q - NA_KW // 2, 0, GRID_W - NA_KW)
    inside = (kc >= start) & (kc < start + NA_KW)
    dc = kc - q + (NA_KW - 1)
    onehot = ((dc[None] == np.arange(2 * NA_KW - 1)[:, None, None]) & inside[None]).astype(np.float32)
    m = jnp.einsum('hrd,dqk->hrqk', rpb.astype(F32), jnp.asarray(onehot), precision=lax.Precision.HIGHEST)
    m = m * LOG2E + jnp.asarray(np.where(inside, 0.0, NEG_BIG).astype(np.float32))
    t = jnp.stack([m[:, NA_KH - 1 - o:2 * NA_KH - 1 - o] for o in range(NA_KH)], axis=1)
    t = t.reshape(nh // 2, 2, NA_KH, NA_KH, GRID_W, GRID_W // NA_KBLK, NA_KBLK)
    t = jnp.transpose(t, (0, 2, 1, 4, 5, 3, 6))
    t = t.reshape(nh // 2, NA_KH, 2, GRID_W, NA_KH * GRID_W)
    lanes = NA_KH * NA_KBLK
    t = jnp.concatenate([t[:, :, :, qa:qb, b0 * lanes:(b0 + NA_BAND) * lanes] for qa, qb, b0 in _natten_rects()],
                        axis=3)
    return t.reshape(nh // 2, NA_KH, 2 * GRID_W, NA_BAND * lanes)


def _natten_rects():
    nblk = GRID_W // NA_KBLK
    rects = []
    for q0 in range(0, GRID_W, 8):
        lo = min(max(q0 - NA_KW // 2, 0), GRID_W - NA_KW)
        hi = min(max(q0 + 7 - NA_KW // 2, 0), GRID_W - NA_KW) + NA_KW
        b0 = min(lo // NA_KBLK, nblk - NA_BAND)
        assert (hi - 1) // NA_KBLK < b0 + NA_BAND
        if rects and rects[-1][2] == b0:
            rects[-1] = (rects[-1][0], q0 + 8, b0)
        else:
            rects.append((q0, q0 + 8, b0))
    return rects


def _natten_kernel(q_ref, k_ref, v_ref, gd_ref, t_ref, o_ref, *, rows):
    rg = pl.program_id(2)
    hd = NA_HEAD_DIM
    lane = lax.broadcasted_iota(jnp.int32, (GRID_W, 2 * hd), 1)
    first = lane < hd
    scale = hd ** -0.5 * LOG2E
    nblk = GRID_W // NA_KBLK
    lanes_per_blk = NA_KH * NA_KBLK
    boxes = [(hh * GRID_W + qa, hh * GRID_W + qb, b0) for hh in range(2) for qa, qb, b0 in _natten_rects()]

    def key_start(rr):
        r = rg * NA_ROWS_PER_STEP + rr
        rs = jnp.clip(r - NA_KH // 2, 0, rows - NA_KH)
        return r - rs, pl.multiple_of(rs * GRID_W, GRID_W)

    def key_block(ref, k0):
        blk = ref[pl.ds(k0, NA_KH * GRID_W), :]
        return jnp.concatenate([blk[j * GRID_W + b * NA_KBLK:j * GRID_W + (b + 1) * NA_KBLK]
                                for b in range(nblk) for j in range(NA_KH)], axis=0)

    def scores(rr):
        _, k0 = key_start(rr)
        q2 = q_ref[rr * GRID_W:(rr + 1) * GRID_W, :].astype(F32) * scale
        qs = jnp.concatenate([jnp.where(first, q2, 0.0), jnp.where(first, 0.0, q2)], axis=0).astype(BF16)
        return lax.dot_general(qs, key_block(k_ref, k0), (((1,), (1,)), ((), ())), preferred_element_type=F32)

    pending = [scores(rr) for rr in range(NA_LOOKAHEAD)]
    for rr in range(NA_ROWS_PER_STEP):
        if rr + NA_LOOKAHEAD < NA_ROWS_PER_STEP:
            pending.append(scores(rr + NA_LOOKAHEAD))
        s = pending.pop(0)
        off, k0 = key_start(rr)
        qrows = slice(rr * GRID_W, (rr + 1) * GRID_W)
        sb = jnp.concatenate([s[ra:rb, b0 * lanes_per_blk:(b0 + NA_BAND) * lanes_per_blk]
                              for ra, rb, b0 in boxes], axis=0) + t_ref[off]
        p = jnp.exp2(sb - jnp.max(sb, axis=-1, keepdims=True))
        den = jnp.sum(p, axis=-1, keepdims=True)
        p_rows = []
        for ra, rb, b0 in boxes:
            parts = [p[ra:rb]]
            if b0 > 0:
                parts.insert(0, jnp.zeros((rb - ra, b0 * lanes_per_blk), F32))
            if b0 + NA_BAND < nblk:
                parts.append(jnp.zeros((rb - ra, (nblk - b0 - NA_BAND) * lanes_per_blk), F32))
            p_rows.append(jnp.concatenate(parts, axis=1))
        pm = jnp.concatenate(p_rows, axis=0).astype(BF16)
        o2 = jnp.dot(pm, key_block(v_ref, k0), preferred_element_type=F32) / den
        att = jnp.where(first, o2[:GRID_W], o2[GRID_W:])
        o_ref[qrows, :] = (att * _silu(gd_ref[qrows, :].astype(F32))).astype(o_ref.dtype)


def natten_branch(proj, rpb, bsz, seq, c, col0):
    rows = seq // GRID_W
    assert rows >= NA_KH and rows % NA_ROWS_PER_STEP == 0
    table = _natten_bias_table(rpb)
    pw = 2 * NA_HEAD_DIM
    npair = c // pw
    off = col0 // pw
    tq = NA_ROWS_PER_STEP * GRID_W
    nrg = rows // NA_ROWS_PER_STEP
    qblk = lambda o: pl.BlockSpec((tq, pw), lambda b, hp, rg: (b * nrg + rg, off + o * npair + hp))
    kvblk = lambda o: pl.BlockSpec((seq, pw), lambda b, hp, rg: (b, off + o * npair + hp))
    return pl.pallas_call(
        functools.partial(_natten_kernel, rows=rows),
        grid=(bsz, npair, nrg),
        in_specs=[qblk(0), kvblk(1), kvblk(2), qblk(3),
                  pl.BlockSpec((None,) + table.shape[1:], lambda b, hp, rg: (hp, 0, 0, 0))],
        out_specs=pl.BlockSpec((tq, pw), lambda b, hp, rg: (b * nrg + rg, hp)),
        out_shape=jax.ShapeDtypeStruct((bsz * seq, c), BF16),
        compiler_params=_cparams("parallel", "parallel", "parallel"),
        name="natten_branch",
    )(proj, proj, proj, proj, table)


def even_layer_mix(h, norm_g, w_in, conv_w, conv_b, hy_w0, hy_b0, hy_w1, hy_b1, hy_w2, hy_b2, hy_wout,
                   hy_freq, hy_d, bsz, seq):
    c = h.shape[1]
    proj = norm_matmul(h, norm_g, w_in.astype(BF16))
    kc_raw, ss = hyena_filter(seq, c, hy_w0, hy_b0, hy_w1, hy_b1, hy_w2, hy_b2, hy_wout, hy_freq)
    ya = hyena_branch(proj, conv_w, conv_b, hy_d, kc_raw, ss, bsz, seq, c)
    return ya, proj


def odd_layer_mix(h, norm_g, w_in, pool_w, pool_b, pool_scale, rpb, bsz, seq):
    c = h.shape[1]
    proj = norm_matmul(h, norm_g, w_in.astype(BF16))
    yc = pool_branch(proj, pool_w, pool_b, pool_scale, seq, c)
    yd = natten_branch(proj, rpb, bsz, seq, c, 2 * c)
    return yc, yd


def kernel(x, p, norm_g, final_g, ev_w_in, ev_conv_w, ev_conv_b, hy_w0, hy_b0, hy_w1, hy_b1, hy_w2, hy_b2, hy_wout, hy_freq, hy_d, gm_norm_g, gm_ws, gm_bs, ev_w_out, od_w_in, pool_w, pool_b, pool_scale, na_rpb, od_w_out, ple_up, ple_gate_w, ple_g):
    bsz, seq, d = x.shape
    depth = p.shape[0]
    h = x.reshape(bsz * seq, d)
    for i in range(depth):
        j = i // 2
        if i % 2 == 0:
            ya, proj = even_layer_mix(h, norm_g[i], ev_w_in[j], ev_conv_w[j], ev_conv_b[j], hy_w0[j], hy_b0[j],
                                      hy_w1[j], hy_b1[j], hy_w2[j], hy_b2[j], hy_wout[j], hy_freq[j], hy_d[j],
                                      bsz, seq)
            yb, gmlp, w_out = None, (proj, 4 * d, gm_norm_g[j], gm_ws[j], gm_bs[j]), ev_w_out[j]
        else:
            ya, yb = odd_layer_mix(h, norm_g[i], od_w_in[j], pool_w[j], pool_b[j], pool_scale[j], na_rpb[j],
                                   bsz, seq)
            gmlp, w_out = None, od_w_out[j]
        h = out_ple(ya, yb, h, p[i].reshape(bsz * seq, -1), w_out, ple_g[i], ple_gate_w[i], ple_up[i],
                    final_g, final=(i == depth - 1), gmlp=gmlp)
    return h.reshape(bsz, seq, d)
```

```python
import functools
import math

import jax
import jax.numpy as jnp
import numpy as np
from jax import lax
from jax.experimental import pallas as pl
from jax.experimental.pallas import tpu as pltpu

F32 = jnp.float32
BF16 = jnp.bfloat16

EPS = 1e-6
GRID_W = 64
PLE_DIM = 256
HY_BANDS = 16
HY_FAST_DECAY = 0.3
HY_SLOW_DECAY = 1.5
HY_TARGET = 1e-2
GM_GROUPS = 8
CHUNK = 128
POOL_WINDOWS = (2, 4, 8, 16)
NA_HEADS = 16
NA_HEAD_DIM = 64
NA_KH = 8
NA_KW = 16
FFT_N2 = 64
NEG_BIG = -1e30
V7X_VMEM_BYTES = 64 * 1024 * 1024
VMEM_LIMIT = V7X_VMEM_BYTES - 8 * 1024 * 1024


def _cparams(*sem):
    return pltpu.CompilerParams(dimension_semantics=sem, vmem_limit_bytes=VMEM_LIMIT)


def _silu(x):
    return x * jax.nn.sigmoid(x)


def _rms(x, g):
    return x * lax.rsqrt(jnp.mean(x * x, axis=-1, keepdims=True) + EPS) * g


def _resident(shape):
    return pl.BlockSpec(shape, lambda *_: (0,) * len(shape), pipeline_mode=pl.Buffered(1))


def _norm_matmul_kernel(h_ref, g_ref, w_ref, o_ref, *, tn):
    hn = _rms(h_ref[...], g_ref[...]).astype(BF16)
    for j in range(o_ref.shape[1] // tn):
        cols = slice(j * tn, (j + 1) * tn)
        o_ref[:, cols] = jnp.dot(hn, w_ref[:, cols], preferred_element_type=F32).astype(o_ref.dtype)


def norm_matmul(h, g, w):
    m, d = h.shape
    n = w.shape[1]
    tm = min(512, m)
    tn = min(512, n)
    return pl.pallas_call(
        functools.partial(_norm_matmul_kernel, tn=tn),
        grid=(m // tm,),
        in_specs=[pl.BlockSpec((tm, d), lambda i: (i, 0)),
                  _resident((1, d)),
                  _resident((d, n))],
        out_specs=pl.BlockSpec((tm, n), lambda i: (i, 0)),
        out_shape=jax.ShapeDtypeStruct((m, n), BF16),
        compiler_params=_cparams("parallel"),
        name="norm_matmul",
    )(h, g.reshape(1, d), w)


def _short_conv(x, w, b):
    n = x.shape[0]
    row = lax.broadcasted_iota(jnp.int32, x.shape, 0)
    xm = jnp.where(row == 0, 0.0, pltpu.roll(x, 1, 0))
    xp = jnp.where(row == n - 1, 0.0, pltpu.roll(x, n - 1, 0))
    return xm * w[0:1] + x * w[1:2] + xp * w[2:3] + b


def _short_conv_block(x_ref, w_ref, b_ref):
    n1h, n2, ct = x_ref.shape
    return _short_conv(x_ref[...].astype(F32).reshape(n1h * n2, ct), w_ref[...], b_ref[...])


def _positional_features(seq):
    f32 = np.float32
    t = np.linspace(0.0, 1.0, seq, dtype=f32)[:, None]
    ang = (f32(2.0 * math.pi) * np.arange(seq, dtype=f32)[:, None] / f32(seq)).astype(f32)
    bands = np.linspace(1e-4, HY_BANDS - 1, HY_BANDS, dtype=f32)[None, :]
    ba = (bands * ang).astype(f32)
    feats = np.concatenate([t, np.cos(ba), -np.sin(ba)], axis=-1).astype(f32)
    feats2 = np.concatenate([feats, feats[:1], feats[1:][::-1]], axis=0)
    pad = np.zeros((2 * seq, 128 - feats2.shape[1]), f32)
    return np.concatenate([feats2, pad], axis=1)


def _filter_kernel(feat_ref, w0_ref, b0_ref, w1_ref, b1_ref, w2_ref, b2_ref, wo_ref, fr_ref, dl_ref,
                   kc_ref, ss_ref, *, seq, tr):
    hp = lax.Precision.HIGHEST
    i = pl.program_id(0)
    fr = fr_ref[...]
    h = jnp.sin(fr * (jnp.dot(feat_ref[...], w0_ref[...], precision=hp, preferred_element_type=F32) + b0_ref[...]))
    h = jnp.sin(fr * (jnp.dot(h, w1_ref[...], precision=hp, preferred_element_type=F32) + b1_ref[...]))
    h = jnp.sin(fr * (jnp.dot(h, w2_ref[...], precision=hp, preferred_element_type=F32) + b2_ref[...]))
    k = jnp.dot(h, wo_ref[...], precision=hp, preferred_element_type=F32)
    s = i * tr + lax.broadcasted_iota(jnp.int32, (tr, 1), 0)
    lag = jnp.where(s < seq, s, 2 * seq - s).astype(F32)
    t = lag * (1.0 / (seq - 1))
    k = jnp.where(s == seq, 0.0, k * jnp.exp(-t * dl_ref[...]))
    kc_ref[...] = k

    @pl.when(i == 0)
    def _():
        ss_ref[...] = jnp.zeros_like(ss_ref)

    ss_ref[...] += jnp.sum(k * k, axis=0, keepdims=True)


def hyena_filter(seq, c, w0, b0, w1, b1, w2, b2, wout, freq):
    feats = jnp.asarray(_positional_features(seq))
    hid = w0.shape[1]
    w0p = jnp.pad(w0.astype(F32), ((0, 128 - w0.shape[0]), (0, 0)))
    max_decay = math.log(HY_TARGET) / HY_FAST_DECAY
    min_decay = math.log(HY_TARGET) / HY_SLOW_DECAY
    deltas = jnp.asarray(np.abs(np.linspace(min_decay, max_decay, c, dtype=np.float32)))[None, :]
    tr = min(512, seq)
    half = seq // tr
    full = lambda a: pl.BlockSpec(a.shape, lambda i: (0,) * a.ndim)
    row = lambda a: a.reshape(1, -1)
    args = (feats, w0p, row(b0), w1, row(b1), w2, row(b2), wout, row(freq), deltas)
    specs = [full(a) for a in args[1:]]
    specs[6] = pl.BlockSpec((hid, c), lambda i: (0, i // half))
    return pl.pallas_call(
        functools.partial(_filter_kernel, seq=seq, tr=tr),
        grid=(2 * half,),
        in_specs=[pl.BlockSpec((tr, 128), lambda i: (i, 0))] + specs,
        out_specs=[pl.BlockSpec((tr, c), lambda i: (i, 0)), pl.BlockSpec((1, c), lambda i: (0, 0))],
        out_shape=[jax.ShapeDtypeStruct((2 * seq, c), F32), jax.ShapeDtypeStruct((1, c), F32)],
        compiler_params=_cparams("arbitrary"),
        name="hyena_filter",
    )(*args)


def _dft_tables(seq):
    n = 2 * seq
    n2 = FFT_N2
    n1 = n // n2
    f1 = np.arange(n1)[:, None]
    s1 = np.arange(n1)[None, :]
    a = 2.0 * np.pi * (f1 * s1 % n1) / n1
    fa = np.empty((2 * n1, n1), np.float64)
    fa[0::2] = np.cos(a)
    fa[1::2] = -np.sin(a)
    ia = np.empty((n1 // 2, 2 * n1), np.float64)
    at = a.T[: n1 // 2]
    ia[:, 0::2] = np.cos(at) / n
    ia[:, 1::2] = -np.sin(at) / n
    f1b = np.arange(n1)[:, None, None]
    f2 = np.arange(n2)[None, :, None]
    s2 = np.arange(n2)[None, None, :]
    ph = 2.0 * np.pi * ((s2 * f2 * n1 + s2 * f1b) % n) / n
    gr, gi = np.cos(ph), -np.sin(ph)
    gf = np.concatenate([np.concatenate([gr, -gi], axis=2),
                         np.concatenate([gi, gr], axis=2)], axis=1)
    er, ei = np.transpose(gr, (0, 2, 1)), -np.transpose(gi, (0, 2, 1))
    gb = np.concatenate([np.concatenate([er, -ei], axis=2),
                         np.concatenate([ei, er], axis=2)], axis=1)
    return fa.astype(np.float32), ia.astype(np.float32), gf.astype(np.float32), gb.astype(np.float32)


def _filter_stage_a_kernel(wh_ref, wl_ref, x_ref, o_ref):
    n1, n2, ct = x_ref.shape
    nf = o_ref.shape[0]
    for oc in range(n2 // KRON):
        x8 = x_ref[:, oc * KRON:(oc + 1) * KRON, :].reshape(n1 * KRON, ct)
        hi = x8.astype(BF16)
        lo = (x8 - hi.astype(F32)).astype(BF16)
        y = (jnp.dot(wh_ref[...], hi, preferred_element_type=F32)
             + jnp.dot(wh_ref[...], lo, preferred_element_type=F32)
             + jnp.dot(wl_ref[...], hi, preferred_element_type=F32))
        o_ref[:, oc * 2 * KRON:(oc + 1) * 2 * KRON, :] = y.reshape(nf, 2 * KRON, ct)


def filter_stage_a(w, x3):
    n1, n2, c = x3.shape
    nf = w.shape[0] // (2 * KRON)
    ct = min(256, c)
    wh = w.astype(BF16)
    wl = (w - wh.astype(F32)).astype(BF16)
    return pl.pallas_call(
        _filter_stage_a_kernel,
        grid=(c // ct,),
        in_specs=[_resident(wh.shape), _resident(wl.shape),
                  pl.BlockSpec((n1, n2, ct), lambda j: (0, 0, j))],
        out_specs=pl.BlockSpec((nf, 2 * n2, ct), lambda j: (0, 0, j)),
        out_shape=jax.ShapeDtypeStruct((nf, 2 * n2, c), F32),
        compiler_params=_cparams("parallel"),
        name="filter_stage_a",
    )(wh, wl, x3)


def _bmm(g, x, precise):
    dn = (((2,), (1,)), ((0,), (0,)))
    if precise:
        return lax.dot_general(g, x, dn, precision=lax.Precision.HIGHEST, preferred_element_type=F32)
    return lax.dot_general(g, x.astype(BF16), dn, preferred_element_type=F32)


def _filter_spectrum_kernel(g_ref, x_ref, ss_ref, d_ref, o_ref):
    spec = _bmm(g_ref[...], x_ref[...], True) * lax.rsqrt(ss_ref[...] + EPS)
    is_real = lax.broadcasted_iota(jnp.int32, spec.shape, 1) < spec.shape[1] // 2
    o_ref[...] = spec + jnp.where(is_real, d_ref[...], 0.0)


def _largest_divisor(n, cap):
    return max(d for d in range(1, cap + 1) if n % d == 0)


def filter_spectrum(gf, x, ss, d):
    nf = gf.shape[0]
    _, k2, c = x.shape
    ft = _largest_divisor(nf, 8)
    ct = min(256, c)
    return pl.pallas_call(
        _filter_spectrum_kernel,
        grid=(nf // ft, c // ct),
        in_specs=[pl.BlockSpec((ft, k2, k2), lambda i, j: (i, 0, 0)),
                  pl.BlockSpec((ft, k2, ct), lambda i, j: (i, 0, j)),
                  pl.BlockSpec((1, ct), lambda i, j: (0, j)),
                  pl.BlockSpec((1, ct), lambda i, j: (0, j))],
        out_specs=pl.BlockSpec((ft, k2, ct), lambda i, j: (i, 0, j)),
        out_shape=jax.ShapeDtypeStruct((nf, k2, c), F32),
        compiler_params=_cparams("parallel", "parallel"),
        name="filter_spectrum",
    )(gf, x, ss, d.reshape(1, c))


KRON = 8
STAGE_B_CHUNK = 16


def _hyena_kernel(fk_ref, ik_ref, gf_ref, gb_ref, h_ref, x0_ref, x1_ref, v_ref, ga_ref,
                  w0_ref, b0_ref, w1_ref, b1_ref, wv_ref, bv_ref, o_ref, seq_ref, xg_ref, za_ref, zb_ref):
    n1h, n2, ct = x1_ref.shape
    nf = za_ref.shape[0]
    z = _short_conv_block(x1_ref, w1_ref, b1_ref) * _short_conv_block(v_ref, wv_ref, bv_ref)
    seq_ref[...] = z.reshape(n1h, n2, ct)
    gate = _silu(ga_ref[...].astype(F32).reshape(n1h * n2, ct))
    xg_ref[...] = (_short_conv_block(x0_ref, w0_ref, b0_ref) * gate).reshape(n1h, n2, ct)
    for oc in range(n2 // KRON):
        x8 = seq_ref[:, oc * KRON:(oc + 1) * KRON, :].reshape(n1h * KRON, ct).astype(BF16)
        y = jnp.dot(fk_ref[...], x8, preferred_element_type=F32)
        za_ref[:, oc * 2 * KRON:(oc + 1) * 2 * KRON, :] = y.reshape(nf, 2 * KRON, ct).astype(BF16)
    ft = _largest_divisor(nf, STAGE_B_CHUNK)
    for f0 in range(0, nf, ft):
        fs = slice(f0, f0 + ft)
        spec = _bmm(gf_ref[fs], za_ref[fs], False)
        sr, si = spec[:, :n2], spec[:, n2:]
        hr, hi = h_ref[fs, :n2], h_ref[fs, n2:]
        prod = jnp.concatenate([sr * hr - si * hi, sr * hi + si * hr], axis=1)
        zb_ref[fs] = _bmm(gb_ref[fs], prod, False).astype(BF16)
    for oc in range(n2 // KRON):
        x8 = zb_ref[:, oc * 2 * KRON:(oc + 1) * 2 * KRON, :].reshape(nf * 2 * KRON, ct)
        y = jnp.dot(ik_ref[...], x8, preferred_element_type=F32).reshape(n1h, KRON, ct)
        sl = slice(oc * KRON, (oc + 1) * KRON)
        o_ref[:, sl, :] = (y * xg_ref[:, sl, :]).astype(o_ref.dtype)


def hyena_branch(proj, conv_w, conv_b, d, kc_raw, ss, bsz, seq, c):
    n2 = FFT_N2
    n1 = 2 * seq // n2
    n1h = n1 // 2
    nf = n1h + 1
    fa, ia, gf, gb = _dft_tables(seq)
    k = np.arange(2 * n2)
    perm = (k // (2 * KRON)) * KRON + (k % KRON) + ((k // KRON) % 2) * n2
    eye = np.eye(KRON, dtype=np.float32)
    kspec = filter_stage_a(jnp.asarray(np.kron(fa[:2 * nf], eye)), kc_raw.reshape(n1, n2, c))
    hspec = filter_spectrum(jnp.asarray(gf[:nf][:, :, perm]), kspec, ss, d)
    fk = jnp.asarray(np.kron(fa[:2 * nf, :n1h], eye), dtype=BF16)
    weight = np.repeat(np.where((np.arange(nf) == 0) | (np.arange(nf) == n1h), 1.0, 2.0), 2)
    ik = jnp.asarray(np.kron(ia[:, :2 * nf] * weight[None, :], eye), dtype=BF16)
    gfs = jnp.asarray(gf[:nf][:, :, perm], dtype=BF16)
    gbs = jnp.asarray(gb[:nf][:, perm, :], dtype=BF16)
    proj4 = proj.reshape(bsz, n1h, n2, proj.shape[1])
    cb = conv_b.reshape(1, -1)
    ct = min(256, c)
    nc = c // ct
    seq_blk = lambda group: pl.BlockSpec((None, n1h, n2, ct), lambda j, b: (b, 0, 0, group * nc + j))
    conv_blk = lambda group: [pl.BlockSpec((3, ct), lambda j, b: (0, group * nc + j)),
                              pl.BlockSpec((1, ct), lambda j, b: (0, group * nc + j))]
    ya = pl.pallas_call(
        _hyena_kernel,
        grid=(nc, bsz),
        in_specs=[_resident(fk.shape), _resident(ik.shape), _resident(gfs.shape), _resident(gbs.shape),
                  pl.BlockSpec((nf, 2 * n2, ct), lambda j, b: (0, 0, j), pipeline_mode=pl.Buffered(1)),
                  seq_blk(0), seq_blk(1), seq_blk(2), seq_blk(3)]
        + conv_blk(0) + conv_blk(1) + conv_blk(2),
        out_specs=pl.BlockSpec((None, n1h, n2, ct), lambda j, b: (b, 0, 0, j)),
        out_shape=jax.ShapeDtypeStruct((bsz, n1h, n2, c), BF16),
        scratch_shapes=[pltpu.VMEM((n1h, n2, ct), F32), pltpu.VMEM((n1h, n2, ct), F32),
                        pltpu.VMEM((nf, 2 * n2, ct), BF16), pltpu.VMEM((nf, 2 * n2, ct), BF16)],
        compiler_params=_cparams("parallel", "arbitrary"),
        name="hyena_branch",
    )(fk, ik, gfs, gbs, hspec, proj4, proj4, proj4, proj4, conv_w, cb, conv_w, cb, conv_w, cb)
    return ya.reshape(bsz * seq, c)


def _gmlp_kernel(u_ref, v_ref, gb_ref, ng_ref, ws_ref, bias_ref, o_ref):
    vn = _rms(v_ref[...].astype(F32), ng_ref[...]).astype(BF16)
    tr, c = vn.shape
    gc = c // GM_GROUPS
    for n in range(tr // CHUNK):
        rows = slice(n * CHUNK, (n + 1) * CHUNK)
        for g in range(GM_GROUPS):
            cols = slice(g * gc, (g + 1) * gc)
            s = jnp.dot(ws_ref[g], vn[rows, cols], preferred_element_type=F32) + bias_ref[:, cols]
            y = u_ref[rows, cols].astype(F32) * s * _silu(gb_ref[rows, cols].astype(F32))
            o_ref[rows, cols] = y.astype(o_ref.dtype)


def _out_ple_tail(ya_ref, yb_ref, h_ref, p_ref, wa_ref, wb_ref, pg_ref, gw_ref, up_ref, fg_ref, o_ref, final):
    mix = (jnp.dot(ya_ref[...].astype(BF16), wa_ref[...], preferred_element_type=F32)
           + jnp.dot(yb_ref[...].astype(BF16), wb_ref[...], preferred_element_type=F32))
    h1 = h_ref[...] + mix
    r = _rms(h1, pg_ref[...]).astype(BF16)
    gate = jax.nn.sigmoid(jnp.dot(r, gw_ref[...], preferred_element_type=F32))
    up = jnp.dot(p_ref[...].astype(BF16), up_ref[...], preferred_element_type=F32)
    h2 = h1 + up * gate
    if final:
        h2 = _rms(h2, fg_ref[...])
    o_ref[...] = h2


def _out_ple_kernel(ya_ref, yb_ref, *rest, final):
    _out_ple_tail(ya_ref, yb_ref, *rest, final)


def _out_ple_gmlp_kernel(ya_ref, u_ref, v_ref, gb_ref, ng_ref, ws_ref, bias_ref, *rest, final):
    *tail, yb_ref = rest
    _gmlp_kernel(u_ref, v_ref, gb_ref, ng_ref, ws_ref, bias_ref, yb_ref)
    _out_ple_tail(ya_ref, yb_ref, *tail, final)


def out_ple(ya, yb, h, p, w_out, ple_g, gate_w, ple_up, final_g, final, gmlp=None):
    m, d = h.shape
    ca = ya.shape[1]
    tm = min(1024, m)
    rowblk = lambda a: pl.BlockSpec((tm, a.shape[1]), lambda i: (i, 0))
    full = lambda a: _resident(a.shape)
    wa = w_out[:ca].astype(BF16)
    wb = w_out[ca:].astype(BF16)
    tail = (h, p, wa, wb, ple_g.reshape(1, d), gate_w.astype(BF16), ple_up.astype(BF16), final_g.reshape(1, d))
    tail_specs = [rowblk(a) for a in tail[:2]] + [full(a) for a in tail[2:]]
    if gmlp is None:
        kern, args, specs, scratch = _out_ple_kernel, (ya, yb), [rowblk(ya), rowblk(yb)], []
    else:
        proj, col0, norm_g, ws, bs = gmlp
        c = (proj.shape[1] - col0) // 3
        off = col0 // c
        bias = jnp.repeat(bs.T, c // GM_GROUPS, axis=1)
        blk = lambda o: pl.BlockSpec((tm, c), lambda i: (i, off + o))
        extra = (norm_g.reshape(1, c), ws.astype(BF16), bias)
        kern, args = _out_ple_gmlp_kernel, (ya, proj, proj, proj) + extra
        specs = [rowblk(ya), blk(0), blk(1), blk(2)] + [full(a) for a in extra]
        scratch = [pltpu.VMEM((tm, c), BF16)]
    return pl.pallas_call(
        functools.partial(kern, final=final),
        grid=(m // tm,),
        in_specs=specs + tail_specs,
        out_specs=pl.BlockSpec((tm, d), lambda i: (i, 0)),
        out_shape=jax.ShapeDtypeStruct((m, d), F32),
        scratch_shapes=scratch,
        compiler_params=_cparams("parallel"),
        name="out_ple",
    )(*args, *tail)


POOL_HALO = 16


def _pool_kernel(x_ref, prev_ref, next_ref, gc_ref, band_ref, top_ref, bot_ref, w_ref, b_ref, sc_ref, o_ref,
                 *, seq):
    tr, c = x_ref.shape
    hl = POOL_HALO
    i = pl.program_id(0)
    t0 = (i * tr) % seq
    zeros = jnp.zeros((hl, c), BF16)
    prev = jnp.where(t0 == 0, zeros, prev_ref[...])
    nxt = jnp.where(t0 + tr == seq, zeros, next_ref[...])
    t = t0 + lax.broadcasted_iota(jnp.int32, (tr, 1), 0)
    gcw = c // len(POOL_WINDOWS)
    for g, w in enumerate(POOL_WINDOWS):
        cols = slice(g * gcw, (g + 1) * gcw)
        acc = jnp.dot(band_ref[g], x_ref[:, cols], preferred_element_type=F32)
        top = acc[:hl] + jnp.dot(top_ref[g], prev[:, cols], preferred_element_type=F32)
        bot = acc[tr - hl:] + jnp.dot(bot_ref[g], nxt[:, cols], preferred_element_type=F32)
        acc = jnp.concatenate([top, acc[hl:tr - hl], bot], axis=0)
        cnt = (jnp.minimum(t + w // 2, seq) - jnp.maximum(t - w // 2, 0)).astype(F32)
        dlt = acc * (1.0 / cnt) - x_ref[:, cols].astype(F32)
        y = jnp.dot(dlt.astype(BF16), w_ref[g], preferred_element_type=F32) + b_ref[:, cols]
        o_ref[:, cols] = (y * sc_ref[:, cols] * _silu(gc_ref[:, cols].astype(F32))).astype(o_ref.dtype)


def _pool_bands(tr):
    hl = POOL_HALO
    ng = len(POOL_WINDOWS)
    band = np.zeros((ng, tr, tr), np.float32)
    top = np.zeros((ng, hl, hl), np.float32)
    bot = np.zeros((ng, hl, hl), np.float32)
    t = np.arange(tr)[:, None]
    s = np.arange(tr)[None, :]
    a = np.arange(hl)[:, None]
    j = np.arange(hl)[None, :]
    for g, w in enumerate(POOL_WINDOWS):
        band[g] = (s - t >= -(w // 2)) & (s - t < w // 2)
        top[g] = (j - hl) - a >= -(w // 2)
        bot[g] = (hl + j) - a < w // 2
    return band, top, bot


def pool_branch(proj, pool_w, pool_b, pool_scale, seq, c):
    m = proj.shape[0]
    tr = min(512, seq)
    hb = tr // POOL_HALO
    nblk = m // POOL_HALO
    band, top, bot = (jnp.asarray(a, dtype=BF16) for a in _pool_bands(tr))
    return pl.pallas_call(
        functools.partial(_pool_kernel, seq=seq),
        grid=(m // tr,),
        in_specs=[pl.BlockSpec((tr, c), lambda i: (i, 0)),
                  pl.BlockSpec((POOL_HALO, c), lambda i: (jnp.maximum(i * hb - 1, 0), 0)),
                  pl.BlockSpec((POOL_HALO, c), lambda i: (jnp.minimum((i + 1) * hb, nblk - 1), 0)),
                  pl.BlockSpec((tr, c), lambda i: (i, 1)),
                  _resident(band.shape), _resident(top.shape), _resident(bot.shape),
                  _resident(pool_w.shape), _resident((1, c)), _resident((1, c))],
        out_specs=pl.BlockSpec((tr, c), lambda i: (i, 0)),
        out_shape=jax.ShapeDtypeStruct((m, c), BF16),
        compiler_params=_cparams("parallel"),
        name="pool_branch",
    )(proj, proj, proj, proj, band, top, bot, pool_w.astype(BF16), pool_b.reshape(1, c),
      pool_scale.reshape(1, c))


NA_KBLK = 16
NA_BAND = 2
NA_ROWS_PER_STEP = 32
NA_LOOKAHEAD = 3
LOG2E = math.log2(math.e)


def _natten_bias_table(rpb):
    nh = rpb.shape[0]
    nd = 2 * NA_KW - 1
    first_blk = np.zeros(GRID_W, np.int64)
    for qa, qb, b0 in _natten_rects():
        first_blk[qa:qb] = b0
    q = np.arange(GRID_W)[:, None, None]
    kc = (first_blk[:, None, None] + np.arange(NA_BAND)[None, :, None]) * NA_KBLK + np.arange(NA_KBLK)[None, None, :]
    start = np.clip(q - NA_KW // 2, 0, GRID_W - NA_KW)
    inside = (kc >= start) & (kc < start + NA_KW)
    onehot = ((kc - q + (NA_KW - 1))[..., None] == np.arange(nd)) & inside[..., None]
    o = np.arange(NA_KH)[:, None]
    j = np.arange(NA_KH)[None, :]
    rows = rpb.astype(F32)[:, j - o + (NA_KH - 1), :].reshape(nh // 2, 2, NA_KH, NA_KH, nd)
    t = jnp.einsum('phojd,qbkd->pohqbjk', rows, jnp.asarray(onehot.astype(np.float32)),
                   precision=lax.Precision.HIGHEST)
    mask = np.where(inside, 0.0, NEG_BIG).astype(np.float32)[None, None, None, :, :, None, :]
    t = t * LOG2E + jnp.asarray(mask)
    return t.reshape(nh // 2, NA_KH, 2 * GRID_W, NA_BAND * NA_KH * NA_KBLK)


def _natten_rects():
    nblk = GRID_W // NA_KBLK
    rects = []
    for q0 in range(0, GRID_W, 8):
        lo = min(max(q0 - NA_KW // 2, 0), GRID_W - NA_KW)
        hi = min(max(q0 + 7 - NA_KW // 2, 0), GRID_W - NA_KW) + NA_KW
        b0 = min(lo // NA_KBLK, nblk - NA_BAND)
        assert (hi - 1) // NA_KBLK < b0 + NA_BAND
        if rects and rects[-1][2] == b0:
            rects[-1] = (rects[-1][0], q0 + 8, b0)
        else:
            rects.append((q0, q0 + 8, b0))
    return rects


def _natten_kernel(q_ref, k_ref, v_ref, gd_ref, t_ref, o_ref, *, rows):
    rg = pl.program_id(2)
    hd = NA_HEAD_DIM
    lane = lax.broadcasted_iota(jnp.int32, (GRID_W, 2 * hd), 1)
    first = lane < hd
    scale = hd ** -0.5 * LOG2E
    nblk = GRID_W // NA_KBLK
    lanes_per_blk = NA_KH * NA_KBLK
    boxes = [(hh * GRID_W + qa, hh * GRID_W + qb, b0) for hh in range(2) for qa, qb, b0 in _natten_rects()]

    def key_start(rr):
        r = rg * NA_ROWS_PER_STEP + rr
        rs = jnp.clip(r - NA_KH // 2, 0, rows - NA_KH)
        return r - rs, pl.multiple_of(rs * GRID_W, GRID_W)

    def key_block(ref, k0):
        blk = ref[pl.ds(k0, NA_KH * GRID_W), :]
        return jnp.concatenate([blk[j * GRID_W + b * NA_KBLK:j * GRID_W + (b + 1) * NA_KBLK]
                                for b in range(nblk) for j in range(NA_KH)], axis=0)

    def scores(rr):
        _, k0 = key_start(rr)
        q2 = q_ref[rr * GRID_W:(rr + 1) * GRID_W, :].astype(F32) * scale
        qs = jnp.concatenate([jnp.where(first, q2, 0.0), jnp.where(first, 0.0, q2)], axis=0).astype(BF16)
        return lax.dot_general(qs, key_block(k_ref, k0), (((1,), (1,)), ((), ())), preferred_element_type=F32)

    pending = [scores(rr) for rr in range(NA_LOOKAHEAD)]
    for rr in range(NA_ROWS_PER_STEP):
        if rr + NA_LOOKAHEAD < NA_ROWS_PER_STEP:
            pending.append(scores(rr + NA_LOOKAHEAD))
        s = pending.pop(0)
        off, k0 = key_start(rr)
        qrows = slice(rr * GRID_W, (rr + 1) * GRID_W)
        sb = jnp.concatenate([s[ra:rb, b0 * lanes_per_blk:(b0 + NA_BAND) * lanes_per_blk]
                              for ra, rb, b0 in boxes], axis=0) + t_ref[off]
        p = jnp.exp2(sb - jnp.max(sb, axis=-1, keepdims=True))
        den = jnp.sum(p, axis=-1, keepdims=True)
        p_rows = []
        for ra, rb, b0 in boxes:
            parts = [p[ra:rb]]
            if b0 > 0:
                parts.insert(0, jnp.zeros((rb - ra, b0 * lanes_per_blk), F32))
            if b0 + NA_BAND < nblk:
                parts.append(jnp.zeros((rb - ra, (nblk - b0 - NA_BAND) * lanes_per_blk), F32))
            p_rows.append(jnp.concatenate(parts, axis=1))
        pm = jnp.concatenate(p_rows, axis=0).astype(BF16)
        o2 = jnp.dot(pm, key_block(v_ref, k0), preferred_element_type=F32) / den
        att = jnp.where(first, o2[:GRID_W], o2[GRID_W:])
        o_ref[qrows, :] = (att * _silu(gd_ref[qrows, :].astype(F32))).astype(o_ref.dtype)


def natten_branch(proj, rpb, bsz, seq, c, col0):
    rows = seq // GRID_W
    assert rows >= NA_KH and rows % NA_ROWS_PER_STEP == 0
    table = _natten_bias_table(rpb)
    pw = 2 * NA_HEAD_DIM
    npair = c // pw
    off = col0 // pw
    tq = NA_ROWS_PER_STEP * GRID_W
    nrg = rows // NA_ROWS_PER_STEP
    qblk = lambda o: pl.BlockSpec((tq, pw), lambda b, hp, rg: (b * nrg + rg, off + o * npair + hp))
    kvblk = lambda o: pl.BlockSpec((seq, pw), lambda b, hp, rg: (b, off + o * npair + hp))
    return pl.pallas_call(
        functools.partial(_natten_kernel, rows=rows),
        grid=(bsz, npair, nrg),
        in_specs=[qblk(0), kvblk(1), kvblk(2), qblk(3),
                  pl.BlockSpec((None,) + table.shape[1:], lambda b, hp, rg: (hp, 0, 0, 0))],
        out_specs=pl.BlockSpec((tq, pw), lambda b, hp, rg: (b * nrg + rg, hp)),
        out_shape=jax.ShapeDtypeStruct((bsz * seq, c), BF16),
        compiler_params=_cparams("parallel", "parallel", "parallel"),
        name="natten_branch",
    )(proj, proj, proj, proj, table)


def even_layer_mix(h, norm_g, w_in, conv_w, conv_b, hy_w0, hy_b0, hy_w1, hy_b1, hy_w2, hy_b2, hy_wout,
                   hy_freq, hy_d, bsz, seq):
    c = h.shape[1]
    proj = norm_matmul(h, norm_g, w_in.astype(BF16))
    kc_raw, ss = hyena_filter(seq, c, hy_w0, hy_b0, hy_w1, hy_b1, hy_w2, hy_b2, hy_wout, hy_freq)
    ya = hyena_branch(proj, conv_w, conv_b, hy_d, kc_raw, ss, bsz, seq, c)
    return ya, proj


def odd_layer_mix(h, norm_g, w_in, pool_w, pool_b, pool_scale, rpb, bsz, seq):
    c = h.shape[1]
    proj = norm_matmul(h, norm_g, w_in.astype(BF16))
    yc = pool_branch(proj, pool_w, pool_b, pool_scale, seq, c)
    yd = natten_branch(proj, rpb, bsz, seq, c, 2 * c)
    return yc, yd


def kernel(x, p, norm_g, final_g, ev_w_in, ev_conv_w, ev_conv_b, hy_w0, hy_b0, hy_w1, hy_b1, hy_w2, hy_b2, hy_wout, hy_freq, hy_d, gm_norm_g, gm_ws, gm_bs, ev_w_out, od_w_in, pool_w, pool_b, pool_scale, na_rpb, od_w_out, ple_up, ple_gate_w, ple_g):
    bsz, seq, d = x.shape
    depth = p.shape[0]
    h = x.reshape(bsz * seq, d)
    for i in range(depth):
        j = i // 2
        if i % 2 == 0:
            ya, proj = even_layer_mix(h, norm_g[i], ev_w_in[j], ev_conv_w[j], ev_conv_b[j], hy_w0[j], hy_b0[j],
                                      hy_w1[j], hy_b1[j], hy_w2[j], hy_b2[j], hy_wout[j], hy_freq[j], hy_d[j],
                                      bsz, seq)
            yb, gmlp, w_out = None, (proj, 4 * d, gm_norm_g[j], gm_ws[j], gm_bs[j]), ev_w_out[j]
        else:
            ya, yb = odd_layer_mix(h, norm_g[i], od_w_in[j], pool_w[j], pool_b[j], pool_scale[j], na_rpb[j],
                                   bsz, seq)
            gmlp, w_out = None, od_w_out[j]
        h = out_ple(ya, yb, h, p[i].reshape(bsz * seq, -1), w_out, ple_g[i], ple_gate_w[i], ple_up[i],
                    final_g, final=(i == depth - 1), gmlp=gmlp)
    return h.reshape(bsz, seq, d)
```

```python
import functools
import math

import jax
import jax.numpy as jnp
import numpy as np
from jax import lax
from jax.experimental import pallas as pl
from jax.experimental.pallas import tpu as pltpu

F32 = jnp.float32
BF16 = jnp.bfloat16

EPS = 1e-6
GRID_W = 64
PLE_DIM = 256
HY_BANDS = 16
HY_FAST_DECAY = 0.3
HY_SLOW_DECAY = 1.5
HY_TARGET = 1e-2
GM_GROUPS = 8
CHUNK = 128
POOL_WINDOWS = (2, 4, 8, 16)
NA_HEADS = 16
NA_HEAD_DIM = 64
NA_KH = 8
NA_KW = 16
FFT_N2 = 64
NEG_BIG = -1e30
V7X_VMEM_BYTES = 64 * 1024 * 1024
VMEM_LIMIT = V7X_VMEM_BYTES - 8 * 1024 * 1024


def _cparams(*sem):
    return pltpu.CompilerParams(dimension_semantics=sem, vmem_limit_bytes=VMEM_LIMIT)


def _silu(x):
    return x * jax.nn.sigmoid(x)


def _rms(x, g):
    return x * lax.rsqrt(jnp.mean(x * x, axis=-1, keepdims=True) + EPS) * g


def _resident(shape):
    return pl.BlockSpec(shape, lambda *_: (0,) * len(shape), pipeline_mode=pl.Buffered(1))


def _norm_matmul_kernel(h_ref, g_ref, w_ref, o_ref, *, tn):
    hn = _rms(h_ref[...], g_ref[...]).astype(BF16)
    for j in range(o_ref.shape[1] // tn):
        cols = slice(j * tn, (j + 1) * tn)
        o_ref[:, cols] = jnp.dot(hn, w_ref[:, cols], preferred_element_type=F32).astype(o_ref.dtype)


def norm_matmul(h, g, w):
    m, d = h.shape
    n = w.shape[1]
    tm = min(512, m)
    tn = min(512, n)
    return pl.pallas_call(
        functools.partial(_norm_matmul_kernel, tn=tn),
        grid=(m // tm,),
        in_specs=[pl.BlockSpec((tm, d), lambda i: (i, 0)),
                  _resident((1, d)),
                  _resident((d, n))],
        out_specs=pl.BlockSpec((tm, n), lambda i: (i, 0)),
        out_shape=jax.ShapeDtypeStruct((m, n), BF16),
        compiler_params=_cparams("parallel"),
        name="norm_matmul",
    )(h, g.reshape(1, d), w)


def _short_conv(x, w, b):
    n = x.shape[0]
    row = lax.broadcasted_iota(jnp.int32, x.shape, 0)
    xm = jnp.where(row == 0, 0.0, pltpu.roll(x, 1, 0))
    xp = jnp.where(row == n - 1, 0.0, pltpu.roll(x, n - 1, 0))
    return xm * w[0:1] + x * w[1:2] + xp * w[2:3] + b


def _short_conv_block(x_ref, w_ref, b_ref):
    n1h, n2, ct = x_ref.shape
    return _short_conv(x_ref[...].astype(F32).reshape(n1h * n2, ct), w_ref[...], b_ref[...])


def _positional_features(seq):
    f32 = np.float32
    t = np.linspace(0.0, 1.0, seq, dtype=f32)[:, None]
    ang = (f32(2.0 * math.pi) * np.arange(seq, dtype=f32)[:, None] / f32(seq)).astype(f32)
    bands = np.linspace(1e-4, HY_BANDS - 1, HY_BANDS, dtype=f32)[None, :]
    ba = (bands * ang).astype(f32)
    feats = np.concatenate([t, np.cos(ba), -np.sin(ba)], axis=-1).astype(f32)
    feats2 = np.concatenate([feats, feats[:1], feats[1:][::-1]], axis=0)
    pad = np.zeros((2 * seq, 128 - feats2.shape[1]), f32)
    return np.concatenate([feats2, pad], axis=1)


def _filter_kernel(feat_ref, w0_ref, b0_ref, w1_ref, b1_ref, w2_ref, b2_ref, wo_ref, fr_ref, dl_ref,
                   kc_ref, ss_ref, *, seq, tr):
    i = pl.program_id(0)
    fr = fr_ref[...]

    def dot3(x, w3_ref):
        hi = x.astype(BF16)
        lo = (x - hi.astype(F32)).astype(BF16)
        return jnp.dot(jnp.concatenate([hi, lo, hi], axis=1), w3_ref[...], preferred_element_type=F32)

    h = jnp.sin(fr * (dot3(feat_ref[...], w0_ref) + b0_ref[...]))
    h = jnp.sin(fr * (dot3(h, w1_ref) + b1_ref[...]))
    h = jnp.sin(fr * (dot3(h, w2_ref) + b2_ref[...]))
    k = dot3(h, wo_ref)
    s = i * tr + lax.broadcasted_iota(jnp.int32, (tr, 1), 0)
    lag = jnp.where(s < seq, s, 2 * seq - s).astype(F32)
    t = lag * (1.0 / (seq - 1))
    k = jnp.where(s == seq, 0.0, k * jnp.exp(-t * dl_ref[...]))
    kc_ref[...] = k

    @pl.when(i == 0)
    def _():
        ss_ref[...] = jnp.zeros_like(ss_ref)

    ss_ref[...] += jnp.sum(k * k, axis=0, keepdims=True)


def hyena_filter(seq, c, w0, b0, w1, b1, w2, b2, wout, freq):
    feats = jnp.asarray(_positional_features(seq))
    hid = w0.shape[1]
    w0p = jnp.pad(w0.astype(F32), ((0, 128 - w0.shape[0]), (0, 0)))
    max_decay = math.log(HY_TARGET) / HY_FAST_DECAY
    min_decay = math.log(HY_TARGET) / HY_SLOW_DECAY
    deltas = jnp.asarray(np.abs(np.linspace(min_decay, max_decay, c, dtype=np.float32)))[None, :]
    tr = min(512, seq)
    half = seq // tr
    full = lambda a: pl.BlockSpec(a.shape, lambda i: (0,) * a.ndim)
    row = lambda a: a.reshape(1, -1)

    def split3(w):
        hi = w.astype(BF16)
        lo = (w - hi.astype(F32)).astype(BF16)
        return jnp.concatenate([hi, hi, lo], axis=0)

    args = (feats, split3(w0p), row(b0), split3(w1), row(b1), split3(w2), row(b2), split3(wout), row(freq),
            deltas)
    specs = [full(a) for a in args[1:]]
    specs[6] = pl.BlockSpec((3 * hid, c), lambda i: (0, i // half))
    return pl.pallas_call(
        functools.partial(_filter_kernel, seq=seq, tr=tr),
        grid=(2 * half,),
        in_specs=[pl.BlockSpec((tr, 128), lambda i: (i, 0))] + specs,
        out_specs=[pl.BlockSpec((tr, c), lambda i: (i, 0)), pl.BlockSpec((1, c), lambda i: (0, 0))],
        out_shape=[jax.ShapeDtypeStruct((2 * seq, c), F32), jax.ShapeDtypeStruct((1, c), F32)],
        compiler_params=_cparams("arbitrary"),
        name="hyena_filter",
    )(*args)


def _dft_tables(seq):
    n = 2 * seq
    n2 = FFT_N2
    n1 = n // n2
    f1 = np.arange(n1)[:, None]
    s1 = np.arange(n1)[None, :]
    a = 2.0 * np.pi * (f1 * s1 % n1) / n1
    fa = np.empty((2 * n1, n1), np.float64)
    fa[0::2] = np.cos(a)
    fa[1::2] = -np.sin(a)
    ia = np.empty((n1 // 2, 2 * n1), np.float64)
    at = a.T[: n1 // 2]
    ia[:, 0::2] = np.cos(at) / n
    ia[:, 1::2] = -np.sin(at) / n
    f1b = np.arange(n1)[:, None, None]
    f2 = np.arange(n2)[None, :, None]
    s2 = np.arange(n2)[None, None, :]
    ph = 2.0 * np.pi * ((s2 * f2 * n1 + s2 * f1b) % n) / n
    gr, gi = np.cos(ph), -np.sin(ph)
    gf = np.concatenate([np.concatenate([gr, -gi], axis=2),
                         np.concatenate([gi, gr], axis=2)], axis=1)
    er, ei = np.transpose(gr, (0, 2, 1)), -np.transpose(gi, (0, 2, 1))
    gb = np.concatenate([np.concatenate([er, -ei], axis=2),
                         np.concatenate([ei, er], axis=2)], axis=1)
    return fa.astype(np.float32), ia.astype(np.float32), gf.astype(np.float32), gb.astype(np.float32)


def _filter_stage_a_kernel(wh_ref, wl_ref, x_ref, o_ref):
    n1, n2, ct = x_ref.shape
    nf = o_ref.shape[0]
    for oc in range(n2 // KRON):
        x8 = x_ref[:, oc * KRON:(oc + 1) * KRON, :].reshape(n1 * KRON, ct)
        hi = x8.astype(BF16)
        lo = (x8 - hi.astype(F32)).astype(BF16)
        y = (jnp.dot(wh_ref[...], hi, preferred_element_type=F32)
             + jnp.dot(wh_ref[...], lo, preferred_element_type=F32)
             + jnp.dot(wl_ref[...], hi, preferred_element_type=F32))
        o_ref[:, oc * 2 * KRON:(oc + 1) * 2 * KRON, :] = y.reshape(nf, 2 * KRON, ct)


def filter_stage_a(w, x3):
    n1, n2, c = x3.shape
    nf = w.shape[0] // (2 * KRON)
    ct = min(256, c)
    wh = w.astype(BF16)
    wl = (w - wh.astype(F32)).astype(BF16)
    return pl.pallas_call(
        _filter_stage_a_kernel,
        grid=(c // ct,),
        in_specs=[_resident(wh.shape), _resident(wl.shape),
                  pl.BlockSpec((n1, n2, ct), lambda j: (0, 0, j))],
        out_specs=pl.BlockSpec((nf, 2 * n2, ct), lambda j: (0, 0, j)),
        out_shape=jax.ShapeDtypeStruct((nf, 2 * n2, c), F32),
        compiler_params=_cparams("parallel"),
        name="filter_stage_a",
    )(wh, wl, x3)


def _bmm(g, x, precise):
    dn = (((2,), (1,)), ((0,), (0,)))
    if precise:
        return lax.dot_general(g, x, dn, precision=lax.Precision.HIGHEST, preferred_element_type=F32)
    return lax.dot_general(g, x.astype(BF16), dn, preferred_element_type=F32)


def _filter_spectrum_kernel(g_ref, x_ref, ss_ref, d_ref, o_ref):
    spec = _bmm(g_ref[...], x_ref[...], True) * lax.rsqrt(ss_ref[...] + EPS)
    is_real = lax.broadcasted_iota(jnp.int32, spec.shape, 1) < spec.shape[1] // 2
    o_ref[...] = spec + jnp.where(is_real, d_ref[...], 0.0)


def _largest_divisor(n, cap):
    return max(d for d in range(1, cap + 1) if n % d == 0)


def filter_spectrum(gf, x, ss, d):
    nf = gf.shape[0]
    _, k2, c = x.shape
    ft = _largest_divisor(nf, 8)
    ct = min(256, c)
    return pl.pallas_call(
        _filter_spectrum_kernel,
        grid=(nf // ft, c // ct),
        in_specs=[pl.BlockSpec((ft, k2, k2), lambda i, j: (i, 0, 0)),
                  pl.BlockSpec((ft, k2, ct), lambda i, j: (i, 0, j)),
                  pl.BlockSpec((1, ct), lambda i, j: (0, j)),
                  pl.BlockSpec((1, ct), lambda i, j: (0, j))],
        out_specs=pl.BlockSpec((ft, k2, ct), lambda i, j: (i, 0, j)),
        out_shape=jax.ShapeDtypeStruct((nf, k2, c), F32),
        compiler_params=_cparams("parallel", "parallel"),
        name="filter_spectrum",
    )(gf, x, ss, d.reshape(1, c))


KRON = 8
STAGE_B_CHUNK = 16


def _hyena_kernel(fk_ref, ik_ref, gf_ref, gb_ref, h_ref, x0_ref, x1_ref, v_ref, ga_ref,
                  w0_ref, b0_ref, w1_ref, b1_ref, wv_ref, bv_ref, o_ref, seq_ref, xg_ref, za_ref, zb_ref):
    n1h, n2, ct = x1_ref.shape
    nf = za_ref.shape[0]
    z = _short_conv_block(x1_ref, w1_ref, b1_ref) * _short_conv_block(v_ref, wv_ref, bv_ref)
    seq_ref[...] = z.reshape(n1h, n2, ct)
    gate = _silu(ga_ref[...].astype(F32).reshape(n1h * n2, ct))
    xg_ref[...] = (_short_conv_block(x0_ref, w0_ref, b0_ref) * gate).reshape(n1h, n2, ct)
    for oc in range(n2 // KRON):
        x8 = seq_ref[:, oc * KRON:(oc + 1) * KRON, :].reshape(n1h * KRON, ct).astype(BF16)
        y = jnp.dot(fk_ref[...], x8, preferred_element_type=F32)
        za_ref[:, oc * 2 * KRON:(oc + 1) * 2 * KRON, :] = y.reshape(nf, 2 * KRON, ct).astype(BF16)
    ft = _largest_divisor(nf, STAGE_B_CHUNK)
    for f0 in range(0, nf, ft):
        fs = slice(f0, f0 + ft)
        spec = _bmm(gf_ref[fs], za_ref[fs], False)
        sr, si = spec[:, :n2], spec[:, n2:]
        hr, hi = h_ref[fs, :n2], h_ref[fs, n2:]
        prod = jnp.concatenate([sr * hr - si * hi, sr * hi + si * hr], axis=1)
        zb_ref[fs] = _bmm(gb_ref[fs], prod, False).astype(BF16)
    for oc in range(n2 // KRON):
        x8 = zb_ref[:, oc * 2 * KRON:(oc + 1) * 2 * KRON, :].reshape(nf * 2 * KRON, ct)
        y = jnp.dot(ik_ref[...], x8, preferred_element_type=F32).reshape(n1h, KRON, ct)
        sl = slice(oc * KRON, (oc + 1) * KRON)
        o_ref[:, sl, :] = (y * xg_ref[:, sl, :]).astype(o_ref.dtype)


def hyena_branch(proj, conv_w, conv_b, d, kc_raw, ss, bsz, seq, c):
    n2 = FFT_N2
    n1 = 2 * seq // n2
    n1h = n1 // 2
    nf = n1h + 1
    fa, ia, gf, gb = _dft_tables(seq)
    k = np.arange(2 * n2)
    perm = (k // (2 * KRON)) * KRON + (k % KRON) + ((k // KRON) % 2) * n2
    eye = np.eye(KRON, dtype=np.float32)
    kspec = filter_stage_a(jnp.asarray(np.kron(fa[:2 * nf], eye)), kc_raw.reshape(n1, n2, c))
    hspec = filter_spectrum(jnp.asarray(gf[:nf][:, :, perm]), kspec, ss, d)
    fk = jnp.asarray(np.kron(fa[:2 * nf, :n1h], eye), dtype=BF16)
    weight = np.repeat(np.where((np.arange(nf) == 0) | (np.arange(nf) == n1h), 1.0, 2.0), 2)
    ik = jnp.asarray(np.kron(ia[:, :2 * nf] * weight[None, :], eye), dtype=BF16)
    gfs = jnp.asarray(gf[:nf][:, :, perm], dtype=BF16)
    gbs = jnp.asarray(gb[:nf][:, perm, :], dtype=BF16)
    proj4 = proj.reshape(bsz, n1h, n2, proj.shape[1])
    cb = conv_b.reshape(1, -1)
    ct = min(256, c)
    nc = c // ct
    seq_blk = lambda group: pl.BlockSpec((None, n1h, n2, ct), lambda j, b: (b, 0, 0, group * nc + j))
    conv_blk = lambda group: [pl.BlockSpec((3, ct), lambda j, b: (0, group * nc + j)),
                              pl.BlockSpec((1, ct), lambda j, b: (0, group * nc + j))]
    ya = pl.pallas_call(
        _hyena_kernel,
        grid=(nc, bsz),
        in_specs=[_resident(fk.shape), _resident(ik.shape), _resident(gfs.shape), _resident(gbs.shape),
                  pl.BlockSpec((nf, 2 * n2, ct), lambda j, b: (0, 0, j), pipeline_mode=pl.Buffered(1)),
                  seq_blk(0), seq_blk(1), seq_blk(2), seq_blk(3)]
        + conv_blk(0) + conv_blk(1) + conv_blk(2),
        out_specs=pl.BlockSpec((None, n1h, n2, ct), lambda j, b: (b, 0, 0, j)),
        out_shape=jax.ShapeDtypeStruct((bsz, n1h, n2, c), BF16),
        scratch_shapes=[pltpu.VMEM((n1h, n2, ct), F32), pltpu.VMEM((n1h, n2, ct), F32),
                        pltpu.VMEM((nf, 2 * n2, ct), BF16), pltpu.VMEM((nf, 2 * n2, ct), BF16)],
        compiler_params=_cparams("parallel", "arbitrary"),
        name="hyena_branch",
    )(fk, ik, gfs, gbs, hspec, proj4, proj4, proj4, proj4, conv_w, cb, conv_w, cb, conv_w, cb)
    return ya.reshape(bsz * seq, c)


def _gmlp_kernel(u_ref, v_ref, gb_ref, ng_ref, ws_ref, bias_ref, o_ref):
    vn = _rms(v_ref[...].astype(F32), ng_ref[...]).astype(BF16)
    tr, c = vn.shape
    gc = c // GM_GROUPS
    for n in range(tr // CHUNK):
        rows = slice(n * CHUNK, (n + 1) * CHUNK)
        for g in range(GM_GROUPS):
            cols = slice(g * gc, (g + 1) * gc)
            s = jnp.dot(ws_ref[g], vn[rows, cols], preferred_element_type=F32) + bias_ref[:, cols]
            y = u_ref[rows, cols].astype(F32) * s * _silu(gb_ref[rows, cols].astype(F32))
            o_ref[rows, cols] = y.astype(o_ref.dtype)


def _out_ple_tail(ya_ref, yb_ref, h_ref, p_ref, wa_ref, wb_ref, pg_ref, gw_ref, up_ref, fg_ref, o_ref, final):
    mix = (jnp.dot(ya_ref[...].astype(BF16), wa_ref[...], preferred_element_type=F32)
           + jnp.dot(yb_ref[...].astype(BF16), wb_ref[...], preferred_element_type=F32))
    h1 = h_ref[...] + mix
    r = _rms(h1, pg_ref[...]).astype(BF16)
    gate = jax.nn.sigmoid(jnp.dot(r, gw_ref[...], preferred_element_type=F32))
    up = jnp.dot(p_ref[...].astype(BF16), up_ref[...], preferred_element_type=F32)
    h2 = h1 + up * gate
    if final:
        h2 = _rms(h2, fg_ref[...])
    o_ref[...] = h2


def _out_ple_kernel(ya_ref, yb_ref, *rest, final):
    _out_ple_tail(ya_ref, yb_ref, *rest, final)


def _out_ple_gmlp_kernel(ya_ref, u_ref, v_ref, gb_ref, ng_ref, ws_ref, bias_ref, *rest, final):
    *tail, yb_ref = rest
    _gmlp_kernel(u_ref, v_ref, gb_ref, ng_ref, ws_ref, bias_ref, yb_ref)
    _out_ple_tail(ya_ref, yb_ref, *tail, final)


def out_ple(ya, yb, h, p, layer, w_out, ple_g, gate_w, ple_up, final_g, final, gmlp=None):
    m, d = h.shape
    ca = ya.shape[1]
    tm = min(1024, m)
    rowblk = lambda a: pl.BlockSpec((tm, a.shape[1]), lambda i: (i, 0))
    full = lambda a: _resident(a.shape)
    wa = w_out[:ca].astype(BF16)
    wb = w_out[ca:].astype(BF16)
    tail = (h, p, wa, wb, ple_g.reshape(1, d), gate_w.astype(BF16), ple_up.astype(BF16), final_g.reshape(1, d))
    pspec = pl.BlockSpec((None, tm, p.shape[2]), lambda i: (layer, i, 0))
    tail_specs = [rowblk(h), pspec] + [full(a) for a in tail[2:]]
    if gmlp is None:
        kern, args, specs, scratch = _out_ple_kernel, (ya, yb), [rowblk(ya), rowblk(yb)], []
    else:
        proj, col0, norm_g, ws, bs = gmlp
        c = (proj.shape[1] - col0) // 3
        off = col0 // c
        bias = jnp.repeat(bs.T, c // GM_GROUPS, axis=1)
        blk = lambda o: pl.BlockSpec((tm, c), lambda i: (i, off + o))
        extra = (norm_g.reshape(1, c), ws.astype(BF16), bias)
        kern, args = _out_ple_gmlp_kernel, (ya, proj, proj, proj) + extra
        specs = [rowblk(ya), blk(0), blk(1), blk(2)] + [full(a) for a in extra]
        scratch = [pltpu.VMEM((tm, c), BF16)]
    return pl.pallas_call(
        functools.partial(kern, final=final),
        grid=(m // tm,),
        in_specs=specs + tail_specs,
        out_specs=pl.BlockSpec((tm, d), lambda i: (i, 0)),
        out_shape=jax.ShapeDtypeStruct((m, d), F32),
        scratch_shapes=scratch,
        compiler_params=_cparams("parallel"),
        name="out_ple",
    )(*args, *tail)


POOL_HALO = 16


def _pool_kernel(x_ref, prev_ref, next_ref, gc_ref, band_ref, top_ref, bot_ref, w_ref, b_ref, sc_ref, o_ref,
                 *, seq):
    tr, c = x_ref.shape
    hl = POOL_HALO
    i = pl.program_id(0)
    t0 = (i * tr) % seq
    zeros = jnp.zeros((hl, c), BF16)
    prev = jnp.where(t0 == 0, zeros, prev_ref[...])
    nxt = jnp.where(t0 + tr == seq, zeros, next_ref[...])
    t = t0 + lax.broadcasted_iota(jnp.int32, (tr, 1), 0)
    gcw = c // len(POOL_WINDOWS)
    for g, w in enumerate(POOL_WINDOWS):
        cols = slice(g * gcw, (g + 1) * gcw)
        acc = jnp.dot(band_ref[g], x_ref[:, cols], preferred_element_type=F32)
        top = acc[:hl] + jnp.dot(top_ref[g], prev[:, cols], preferred_element_type=F32)
        bot = acc[tr - hl:] + jnp.dot(bot_ref[g], nxt[:, cols], preferred_element_type=F32)
        acc = jnp.concatenate([top, acc[hl:tr - hl], bot], axis=0)
        cnt = (jnp.minimum(t + w // 2, seq) - jnp.maximum(t - w // 2, 0)).astype(F32)
        dlt = acc * (1.0 / cnt) - x_ref[:, cols].astype(F32)
        y = jnp.dot(dlt.astype(BF16), w_ref[g], preferred_element_type=F32) + b_ref[:, cols]
        o_ref[:, cols] = (y * sc_ref[:, cols] * _silu(gc_ref[:, cols].astype(F32))).astype(o_ref.dtype)


def _pool_bands(tr):
    hl = POOL_HALO
    ng = len(POOL_WINDOWS)
    band = np.zeros((ng, tr, tr), np.float32)
    top = np.zeros((ng, hl, hl), np.float32)
    bot = np.zeros((ng, hl, hl), np.float32)
    t = np.arange(tr)[:, None]
    s = np.arange(tr)[None, :]
    a = np.arange(hl)[:, None]
    j = np.arange(hl)[None, :]
    for g, w in enumerate(POOL_WINDOWS):
        band[g] = (s - t >= -(w // 2)) & (s - t < w // 2)
        top[g] = (j - hl) - a >= -(w // 2)
        bot[g] = (hl + j) - a < w // 2
    return band, top, bot


def pool_branch(proj, pool_w, pool_b, pool_scale, seq, c):
    m = proj.shape[0]
    tr = min(512, seq)
    hb = tr // POOL_HALO
    nblk = m // POOL_HALO
    band, top, bot = (jnp.asarray(a, dtype=BF16) for a in _pool_bands(tr))
    return pl.pallas_call(
        functools.partial(_pool_kernel, seq=seq),
        grid=(m // tr,),
        in_specs=[pl.BlockSpec((tr, c), lambda i: (i, 0)),
                  pl.BlockSpec((POOL_HALO, c), lambda i: (jnp.maximum(i * hb - 1, 0), 0)),
                  pl.BlockSpec((POOL_HALO, c), lambda i: (jnp.minimum((i + 1) * hb, nblk - 1), 0)),
                  pl.BlockSpec((tr, c), lambda i: (i, 1)),
                  _resident(band.shape), _resident(top.shape), _resident(bot.shape),
                  _resident(pool_w.shape), _resident((1, c)), _resident((1, c))],
        out_specs=pl.BlockSpec((tr, c), lambda i: (i, 0)),
        out_shape=jax.ShapeDtypeStruct((m, c), BF16),
        compiler_params=_cparams("parallel"),
        name="pool_branch",
    )(proj, proj, proj, proj, band, top, bot, pool_w.astype(BF16), pool_b.reshape(1, c),
      pool_scale.reshape(1, c))


NA_KBLK = 16
NA_BAND = 2
NA_ROWS_PER_STEP = 32
NA_LOOKAHEAD = 3
LOG2E = math.log2(math.e)


def _natten_bias_table(rpb):
    nh = rpb.shape[0]
    nd = 2 * NA_KW - 1
    first_blk = np.zeros(GRID_W, np.int64)
    for qa, qb, b0 in _natten_rects():
        first_blk[qa:qb] = b0
    q = np.arange(GRID_W)[:, None, None]
    kc = (first_blk[:, None, None] + np.arange(NA_BAND)[None, :, None]) * NA_KBLK + np.arange(NA_KBLK)[None, None, :]
    start = np.clip(q - NA_KW // 2, 0, GRID_W - NA_KW)
    inside = (kc >= start) & (kc < start + NA_KW)
    onehot = ((kc - q + (NA_KW - 1))[..., None] == np.arange(nd)) & inside[..., None]
    o = np.arange(NA_KH)[:, None]
    j = np.arange(NA_KH)[None, :]
    rows = rpb.astype(F32)[:, j - o + (NA_KH - 1), :].reshape(nh // 2, 2, NA_KH, NA_KH, nd)
    t = jnp.einsum('phojd,qbkd->pohqbjk', rows, jnp.asarray(onehot.astype(np.float32)),
                   precision=lax.Precision.HIGHEST)
    mask = np.where(inside, 0.0, NEG_BIG).astype(np.float32)[None, None, None, :, :, None, :]
    t = t * LOG2E + jnp.asarray(mask)
    return t.reshape(nh // 2, NA_KH, 2 * GRID_W, NA_BAND * NA_KH * NA_KBLK)


def _natten_rects():
    nblk = GRID_W // NA_KBLK
    rects = []
    for q0 in range(0, GRID_W, 8):
        lo = min(max(q0 - NA_KW // 2, 0), GRID_W - NA_KW)
        hi = min(max(q0 + 7 - NA_KW // 2, 0), GRID_W - NA_KW) + NA_KW
        b0 = min(lo // NA_KBLK, nblk - NA_BAND)
        assert (hi - 1) // NA_KBLK < b0 + NA_BAND
        if rects and rects[-1][2] == b0:
            rects[-1] = (rects[-1][0], q0 + 8, b0)
        else:
            rects.append((q0, q0 + 8, b0))
    return rects


def _natten_kernel(q_ref, k_ref, v_ref, gd_ref, t_ref, o_ref, *, rows):
    rg = pl.program_id(2)
    hd = NA_HEAD_DIM
    lane = lax.broadcasted_iota(jnp.int32, (GRID_W, 2 * hd), 1)
    first = lane < hd
    scale = hd ** -0.5 * LOG2E
    nblk = GRID_W // NA_KBLK
    lanes_per_blk = NA_KH * NA_KBLK
    boxes = [(hh * GRID_W + qa, hh * GRID_W + qb, b0) for hh in range(2) for qa, qb, b0 in _natten_rects()]

    def key_start(rr):
        r = rg * NA_ROWS_PER_STEP + rr
        rs = jnp.clip(r - NA_KH // 2, 0, rows - NA_KH)
        return r - rs, pl.multiple_of(rs * GRID_W, GRID_W)

    def key_block(ref, k0):
        blk = ref[pl.ds(k0, NA_KH * GRID_W), :]
        return jnp.concatenate([blk[j * GRID_W + b * NA_KBLK:j * GRID_W + (b + 1) * NA_KBLK]
                                for b in range(nblk) for j in range(NA_KH)], axis=0)

    def scores(rr):
        _, k0 = key_start(rr)
        q2 = q_ref[rr * GRID_W:(rr + 1) * GRID_W, :].astype(F32) * scale
        qs = jnp.concatenate([jnp.where(first, q2, 0.0), jnp.where(first, 0.0, q2)], axis=0).astype(BF16)
        return lax.dot_general(qs, key_block(k_ref, k0), (((1,), (1,)), ((), ())), preferred_element_type=F32)

    pending = [scores(rr) for rr in range(NA_LOOKAHEAD)]
    for rr in range(NA_ROWS_PER_STEP):
        if rr + NA_LOOKAHEAD < NA_ROWS_PER_STEP:
            pending.append(scores(rr + NA_LOOKAHEAD))
        s = pending.pop(0)
        off, k0 = key_start(rr)
        qrows = slice(rr * GRID_W, (rr + 1) * GRID_W)
        sb = jnp.concatenate([s[ra:rb, b0 * lanes_per_blk:(b0 + NA_BAND) * lanes_per_blk]
                              for ra, rb, b0 in boxes], axis=0) + t_ref[off]
        p = jnp.exp2(sb - jnp.max(sb, axis=-1, keepdims=True))
        den = jnp.sum(p, axis=-1, keepdims=True)
        p_rows = []
        for ra, rb, b0 in boxes:
            parts = [p[ra:rb]]
            if b0 > 0:
                parts.insert(0, jnp.zeros((rb - ra, b0 * lanes_per_blk), F32))
            if b0 + NA_BAND < nblk:
                parts.append(jnp.zeros((rb - ra, (nblk - b0 - NA_BAND) * lanes_per_blk), F32))
            p_rows.append(jnp.concatenate(parts, axis=1))
        pm = jnp.concatenate(p_rows, axis=0).astype(BF16)
        o2 = jnp.dot(pm, key_block(v_ref, k0), preferred_element_type=F32) / den
        att = jnp.where(first, o2[:GRID_W], o2[GRID_W:])
        o_ref[qrows, :] = (att * _silu(gd_ref[qrows, :].astype(F32))).astype(o_ref.dtype)


def natten_branch(proj, rpb, bsz, seq, c, col0):
    rows = seq // GRID_W
    assert rows >= NA_KH and rows % NA_ROWS_PER_STEP == 0
    table = _natten_bias_table(rpb)
    pw = 2 * NA_HEAD_DIM
    npair = c // pw
    off = col0 // pw
    tq = NA_ROWS_PER_STEP * GRID_W
    nrg = rows // NA_ROWS_PER_STEP
    qblk = lambda o: pl.BlockSpec((tq, pw), lambda b, hp, rg: (b * nrg + rg, off + o * npair + hp))
    kvblk = lambda o: pl.BlockSpec((seq, pw), lambda b, hp, rg: (b, off + o * npair + hp))
    return pl.pallas_call(
        functools.partial(_natten_kernel, rows=rows),
        grid=(bsz, npair, nrg),
        in_specs=[qblk(0), kvblk(1), kvblk(2), qblk(3),
                  pl.BlockSpec((None,) + table.shape[1:], lambda b, hp, rg: (hp, 0, 0, 0))],
        out_specs=pl.BlockSpec((tq, pw), lambda b, hp, rg: (b * nrg + rg, hp)),
        out_shape=jax.ShapeDtypeStruct((bsz * seq, c), BF16),
        compiler_params=_cparams("parallel", "parallel", "parallel"),
        name="natten_branch",
    )(proj, proj, proj, proj, table)


def even_layer_mix(h, norm_g, w_in, conv_w, conv_b, hy_w0, hy_b0, hy_w1, hy_b1, hy_w2, hy_b2, hy_wout,
                   hy_freq, hy_d, bsz, seq):
    c = h.shape[1]
    proj = norm_matmul(h, norm_g, w_in.astype(BF16))
    kc_raw, ss = hyena_filter(seq, c, hy_w0, hy_b0, hy_w1, hy_b1, hy_w2, hy_b2, hy_wout, hy_freq)
    ya = hyena_branch(proj, conv_w, conv_b, hy_d, kc_raw, ss, bsz, seq, c)
    return ya, proj


def odd_layer_mix(h, norm_g, w_in, pool_w, pool_b, pool_scale, rpb, bsz, seq):
    c = h.shape[1]
    proj = norm_matmul(h, norm_g, w_in.astype(BF16))
    yc = pool_branch(proj, pool_w, pool_b, pool_scale, seq, c)
    yd = natten_branch(proj, rpb, bsz, seq, c, 2 * c)
    return yc, yd


def kernel(x, p, norm_g, final_g, ev_w_in, ev_conv_w, ev_conv_b, hy_w0, hy_b0, hy_w1, hy_b1, hy_w2, hy_b2, hy_wout, hy_freq, hy_d, gm_norm_g, gm_ws, gm_bs, ev_w_out, od_w_in, pool_w, pool_b, pool_scale, na_rpb, od_w_out, ple_up, ple_gate_w, ple_g):
    bsz, seq, d = x.shape
    depth = p.shape[0]
    h = x.reshape(bsz * seq, d)
    for i in range(depth):
        j = i // 2
        if i % 2 == 0:
            ya, proj = even_layer_mix(h, norm_g[i], ev_w_in[j], ev_conv_w[j], ev_conv_b[j], hy_w0[j], hy_b0[j],
                                      hy_w1[j], hy_b1[j], hy_w2[j], hy_b2[j], hy_wout[j], hy_freq[j], hy_d[j],
                                      bsz, seq)
            yb, gmlp, w_out = None, (proj, 4 * d, gm_norm_g[j], gm_ws[j], gm_bs[j]), ev_w_out[j]
        else:
            ya, yb = odd_layer_mix(h, norm_g[i], od_w_in[j], pool_w[j], pool_b[j], pool_scale[j], na_rpb[j],
                                   bsz, seq)
            gmlp, w_out = None, od_w_out[j]
        h = out_ple(ya, yb, h, p.reshape(depth, bsz * seq, -1), i, w_out, ple_g[i], ple_gate_w[i], ple_up[i],
                    final_g, final=(i == depth - 1), gmlp=gmlp)
    return h.reshape(bsz, seq, d)
```

```python
import functools
import math

import jax
import jax.numpy as jnp
import numpy as np
from jax import lax
from jax.experimental import pallas as pl
from jax.experimental.pallas import tpu as pltpu

F32 = jnp.float32
BF16 = jnp.bfloat16

EPS = 1e-6
GRID_W = 64
PLE_DIM = 256
HY_BANDS = 16
HY_FAST_DECAY = 0.3
HY_SLOW_DECAY = 1.5
HY_TARGET = 1e-2
GM_GROUPS = 8
CHUNK = 128
POOL_WINDOWS = (2, 4, 8, 16)
NA_HEADS = 16
NA_HEAD_DIM = 64
NA_KH = 8
NA_KW = 16
FFT_N2 = 64
NEG_BIG = -1e30
V7X_VMEM_BYTES = 64 * 1024 * 1024
VMEM_LIMIT = V7X_VMEM_BYTES - 8 * 1024 * 1024


def _cparams(*sem):
    return pltpu.CompilerParams(dimension_semantics=sem, vmem_limit_bytes=VMEM_LIMIT)


def _silu(x):
    return x * jax.nn.sigmoid(x)


def _rms(x, g):
    return x * lax.rsqrt(jnp.mean(x * x, axis=-1, keepdims=True) + EPS) * g


def _resident(shape):
    return pl.BlockSpec(shape, lambda *_: (0,) * len(shape), pipeline_mode=pl.Buffered(1))


def _norm_matmul_kernel(h_ref, g_ref, w_ref, o_ref, *, tn):
    hn = _rms(h_ref[...], g_ref[...]).astype(BF16)
    for j in range(o_ref.shape[1] // tn):
        cols = slice(j * tn, (j + 1) * tn)
        o_ref[:, cols] = jnp.dot(hn, w_ref[:, cols], preferred_element_type=F32).astype(o_ref.dtype)


def norm_matmul(h, g, w, layer):
    m, d = h.shape
    n = w.shape[2]
    tm = min(512, m)
    tn = min(512, n)
    return pl.pallas_call(
        functools.partial(_norm_matmul_kernel, tn=tn),
        grid=(m // tm,),
        in_specs=[pl.BlockSpec((tm, d), lambda i: (i, 0)),
                  _resident((1, d)),
                  pl.BlockSpec((None, d, n), lambda i: (layer, 0, 0), pipeline_mode=pl.Buffered(1))],
        out_specs=pl.BlockSpec((tm, n), lambda i: (i, 0)),
        out_shape=jax.ShapeDtypeStruct((m, n), BF16),
        compiler_params=_cparams("parallel"),
        name="norm_matmul",
    )(h, g.reshape(1, d), w)


def _short_conv(x, w, b):
    n = x.shape[0]
    row = lax.broadcasted_iota(jnp.int32, x.shape, 0)
    xm = jnp.where(row == 0, 0.0, pltpu.roll(x, 1, 0))
    xp = jnp.where(row == n - 1, 0.0, pltpu.roll(x, n - 1, 0))
    return xm * w[0:1] + x * w[1:2] + xp * w[2:3] + b


def _short_conv_block(x_ref, w_ref, b_ref):
    n1h, n2, ct = x_ref.shape
    return _short_conv(x_ref[...].astype(F32).reshape(n1h * n2, ct), w_ref[...], b_ref[...])


def _positional_features(seq):
    f32 = np.float32
    t = np.linspace(0.0, 1.0, seq, dtype=f32)[:, None]
    ang = (f32(2.0 * math.pi) * np.arange(seq, dtype=f32)[:, None] / f32(seq)).astype(f32)
    bands = np.linspace(1e-4, HY_BANDS - 1, HY_BANDS, dtype=f32)[None, :]
    ba = (bands * ang).astype(f32)
    feats = np.concatenate([t, np.cos(ba), -np.sin(ba)], axis=-1).astype(f32)
    feats2 = np.concatenate([feats, feats[:1], feats[1:][::-1]], axis=0)
    pad = np.zeros((2 * seq, 128 - feats2.shape[1]), f32)
    return np.concatenate([feats2, pad], axis=1)


def _filter_kernel(feat_ref, w0_ref, b0_ref, w1_ref, b1_ref, w2_ref, b2_ref, wo_ref, fr_ref, dl_ref,
                   kc_ref, ss_ref, *, seq, tr):
    i = pl.program_id(0)
    fr = fr_ref[...]

    def dot3(x, w3_ref):
        hi = x.astype(BF16)
        lo = (x - hi.astype(F32)).astype(BF16)
        return jnp.dot(jnp.concatenate([hi, lo, hi], axis=1), w3_ref[...], preferred_element_type=F32)

    h = jnp.sin(fr * (dot3(feat_ref[...], w0_ref) + b0_ref[...]))
    h = jnp.sin(fr * (dot3(h, w1_ref) + b1_ref[...]))
    h = jnp.sin(fr * (dot3(h, w2_ref) + b2_ref[...]))
    k = dot3(h, wo_ref)
    s = i * tr + lax.broadcasted_iota(jnp.int32, (tr, 1), 0)
    lag = jnp.where(s < seq, s, 2 * seq - s).astype(F32)
    t = lag * (1.0 / (seq - 1))
    k = jnp.where(s == seq, 0.0, k * jnp.exp(-t * dl_ref[...]))
    kc_ref[...] = k

    @pl.when(i == 0)
    def _():
        ss_ref[...] = jnp.zeros_like(ss_ref)

    ss_ref[...] += jnp.sum(k * k, axis=0, keepdims=True)


def hyena_filter(seq, c, w0, b0, w1, b1, w2, b2, wout, freq):
    feats = jnp.asarray(_positional_features(seq))
    hid = w0.shape[1]
    w0p = jnp.pad(w0.astype(F32), ((0, 128 - w0.shape[0]), (0, 0)))
    max_decay = math.log(HY_TARGET) / HY_FAST_DECAY
    min_decay = math.log(HY_TARGET) / HY_SLOW_DECAY
    deltas = jnp.asarray(np.abs(np.linspace(min_decay, max_decay, c, dtype=np.float32)))[None, :]
    tr = min(512, seq)
    half = seq // tr
    full = lambda a: pl.BlockSpec(a.shape, lambda i: (0,) * a.ndim)
    row = lambda a: a.reshape(1, -1)

    def split3(w):
        hi = w.astype(BF16)
        lo = (w - hi.astype(F32)).astype(BF16)
        return jnp.concatenate([hi, hi, lo], axis=0)

    args = (feats, split3(w0p), row(b0), split3(w1), row(b1), split3(w2), row(b2), split3(wout), row(freq),
            deltas)
    specs = [full(a) for a in args[1:]]
    specs[6] = pl.BlockSpec((3 * hid, c), lambda i: (0, i // half))
    return pl.pallas_call(
        functools.partial(_filter_kernel, seq=seq, tr=tr),
        grid=(2 * half,),
        in_specs=[pl.BlockSpec((tr, 128), lambda i: (i, 0))] + specs,
        out_specs=[pl.BlockSpec((tr, c), lambda i: (i, 0)), pl.BlockSpec((1, c), lambda i: (0, 0))],
        out_shape=[jax.ShapeDtypeStruct((2 * seq, c), F32), jax.ShapeDtypeStruct((1, c), F32)],
        compiler_params=_cparams("arbitrary"),
        name="hyena_filter",
    )(*args)


def _dft_tables(seq):
    n = 2 * seq
    n2 = FFT_N2
    n1 = n // n2
    f1 = np.arange(n1)[:, None]
    s1 = np.arange(n1)[None, :]
    a = 2.0 * np.pi * (f1 * s1 % n1) / n1
    fa = np.empty((2 * n1, n1), np.float64)
    fa[0::2] = np.cos(a)
    fa[1::2] = -np.sin(a)
    ia = np.empty((n1 // 2, 2 * n1), np.float64)
    at = a.T[: n1 // 2]
    ia[:, 0::2] = np.cos(at) / n
    ia[:, 1::2] = -np.sin(at) / n
    f1b = np.arange(n1)[:, None, None]
    f2 = np.arange(n2)[None, :, None]
    s2 = np.arange(n2)[None, None, :]
    ph = 2.0 * np.pi * ((s2 * f2 * n1 + s2 * f1b) % n) / n
    gr, gi = np.cos(ph), -np.sin(ph)
    gf = np.concatenate([np.concatenate([gr, -gi], axis=2),
                         np.concatenate([gi, gr], axis=2)], axis=1)
    er, ei = np.transpose(gr, (0, 2, 1)), -np.transpose(gi, (0, 2, 1))
    gb = np.concatenate([np.concatenate([er, -ei], axis=2),
                         np.concatenate([ei, er], axis=2)], axis=1)
    return fa.astype(np.float32), ia.astype(np.float32), gf.astype(np.float32), gb.astype(np.float32)


def _filter_stage_a_kernel(wh_ref, wl_ref, x_ref, o_ref):
    n1, n2, ct = x_ref.shape
    nf = o_ref.shape[0]
    for oc in range(n2 // KRON):
        x8 = x_ref[:, oc * KRON:(oc + 1) * KRON, :].reshape(n1 * KRON, ct)
        hi = x8.astype(BF16)
        lo = (x8 - hi.astype(F32)).astype(BF16)
        y = (jnp.dot(wh_ref[...], hi, preferred_element_type=F32)
             + jnp.dot(wh_ref[...], lo, preferred_element_type=F32)
             + jnp.dot(wl_ref[...], hi, preferred_element_type=F32))
        o_ref[:, oc * 2 * KRON:(oc + 1) * 2 * KRON, :] = y.reshape(nf, 2 * KRON, ct)


def filter_stage_a(w, x3):
    n1, n2, c = x3.shape
    nf = w.shape[0] // (2 * KRON)
    ct = min(256, c)
    wh = w.astype(BF16)
    wl = (w - wh.astype(F32)).astype(BF16)
    return pl.pallas_call(
        _filter_stage_a_kernel,
        grid=(c // ct,),
        in_specs=[_resident(wh.shape), _resident(wl.shape),
                  pl.BlockSpec((n1, n2, ct), lambda j: (0, 0, j))],
        out_specs=pl.BlockSpec((nf, 2 * n2, ct), lambda j: (0, 0, j)),
        out_shape=jax.ShapeDtypeStruct((nf, 2 * n2, c), F32),
        compiler_params=_cparams("parallel"),
        name="filter_stage_a",
    )(wh, wl, x3)


def _bmm(g, x, precise):
    dn = (((2,), (1,)), ((0,), (0,)))
    if precise:
        return lax.dot_general(g, x, dn, precision=lax.Precision.HIGHEST, preferred_element_type=F32)
    return lax.dot_general(g, x.astype(BF16), dn, preferred_element_type=F32)


def _filter_spectrum_kernel(g_ref, x_ref, ss_ref, d_ref, o_ref):
    spec = _bmm(g_ref[...], x_ref[...], True) * lax.rsqrt(ss_ref[...] + EPS)
    is_real = lax.broadcasted_iota(jnp.int32, spec.shape, 1) < spec.shape[1] // 2
    o_ref[...] = spec + jnp.where(is_real, d_ref[...], 0.0)


def _largest_divisor(n, cap):
    return max(d for d in range(1, cap + 1) if n % d == 0)


def filter_spectrum(gf, x, ss, d):
    nf = gf.shape[0]
    _, k2, c = x.shape
    ft = _largest_divisor(nf, 8)
    ct = min(256, c)
    return pl.pallas_call(
        _filter_spectrum_kernel,
        grid=(nf // ft, c // ct),
        in_specs=[pl.BlockSpec((ft, k2, k2), lambda i, j: (i, 0, 0)),
                  pl.BlockSpec((ft, k2, ct), lambda i, j: (i, 0, j)),
                  pl.BlockSpec((1, ct), lambda i, j: (0, j)),
                  pl.BlockSpec((1, ct), lambda i, j: (0, j))],
        out_specs=pl.BlockSpec((ft, k2, ct), lambda i, j: (i, 0, j)),
        out_shape=jax.ShapeDtypeStruct((nf, k2, c), F32),
        compiler_params=_cparams("parallel", "parallel"),
        name="filter_spectrum",
    )(gf, x, ss, d.reshape(1, c))


KRON = 8
STAGE_B_CHUNK = 16


def _hyena_kernel(fk_ref, ik_ref, gf_ref, gb_ref, h_ref, x0_ref, x1_ref, v_ref, ga_ref,
                  w0_ref, b0_ref, w1_ref, b1_ref, wv_ref, bv_ref, o_ref, seq_ref, xg_ref, za_ref, zb_ref):
    n1h, n2, ct = x1_ref.shape
    nf = za_ref.shape[0]
    z = _short_conv_block(x1_ref, w1_ref, b1_ref) * _short_conv_block(v_ref, wv_ref, bv_ref)
    seq_ref[...] = z.reshape(n1h, n2, ct)
    gate = _silu(ga_ref[...].astype(F32).reshape(n1h * n2, ct))
    xg_ref[...] = (_short_conv_block(x0_ref, w0_ref, b0_ref) * gate).reshape(n1h, n2, ct)
    for oc in range(n2 // KRON):
        x8 = seq_ref[:, oc * KRON:(oc + 1) * KRON, :].reshape(n1h * KRON, ct).astype(BF16)
        y = jnp.dot(fk_ref[...], x8, preferred_element_type=F32)
        za_ref[:, oc * 2 * KRON:(oc + 1) * 2 * KRON, :] = y.reshape(nf, 2 * KRON, ct).astype(BF16)
    ft = _largest_divisor(nf, STAGE_B_CHUNK)
    for f0 in range(0, nf, ft):
        fs = slice(f0, f0 + ft)
        spec = _bmm(gf_ref[fs], za_ref[fs], False)
        sr, si = spec[:, :n2], spec[:, n2:]
        hr, hi = h_ref[fs, :n2], h_ref[fs, n2:]
        prod = jnp.concatenate([sr * hr - si * hi, sr * hi + si * hr], axis=1)
        zb_ref[fs] = _bmm(gb_ref[fs], prod, False).astype(BF16)
    for oc in range(n2 // KRON):
        x8 = zb_ref[:, oc * 2 * KRON:(oc + 1) * 2 * KRON, :].reshape(nf * 2 * KRON, ct)
        y = jnp.dot(ik_ref[...], x8, preferred_element_type=F32).reshape(n1h, KRON, ct)
        sl = slice(oc * KRON, (oc + 1) * KRON)
        o_ref[:, sl, :] = (y * xg_ref[:, sl, :]).astype(o_ref.dtype)


def hyena_branch(proj, conv_w, conv_b, d, kc_raw, ss, bsz, seq, c):
    n2 = FFT_N2
    n1 = 2 * seq // n2
    n1h = n1 // 2
    nf = n1h + 1
    fa, ia, gf, gb = _dft_tables(seq)
    k = np.arange(2 * n2)
    perm = (k // (2 * KRON)) * KRON + (k % KRON) + ((k // KRON) % 2) * n2
    eye = np.eye(KRON, dtype=np.float32)
    kspec = filter_stage_a(jnp.asarray(np.kron(fa[:2 * nf], eye)), kc_raw.reshape(n1, n2, c))
    hspec = filter_spectrum(jnp.asarray(gf[:nf][:, :, perm]), kspec, ss, d)
    fk = jnp.asarray(np.kron(fa[:2 * nf, :n1h], eye), dtype=BF16)
    weight = np.repeat(np.where((np.arange(nf) == 0) | (np.arange(nf) == n1h), 1.0, 2.0), 2)
    ik = jnp.asarray(np.kron(ia[:, :2 * nf] * weight[None, :], eye), dtype=BF16)
    gfs = jnp.asarray(gf[:nf][:, :, perm], dtype=BF16)
    gbs = jnp.asarray(gb[:nf][:, perm, :], dtype=BF16)
    proj4 = proj.reshape(bsz, n1h, n2, proj.shape[1])
    cb = conv_b.reshape(1, -1)
    ct = min(256, c)
    nc = c // ct
    seq_blk = lambda group: pl.BlockSpec((None, n1h, n2, ct), lambda j, b: (b, 0, 0, group * nc + j))
    conv_blk = lambda group: [pl.BlockSpec((3, ct), lambda j, b: (0, group * nc + j)),
                              pl.BlockSpec((1, ct), lambda j, b: (0, group * nc + j))]
    ya = pl.pallas_call(
        _hyena_kernel,
        grid=(nc, bsz),
        in_specs=[_resident(fk.shape), _resident(ik.shape), _resident(gfs.shape), _resident(gbs.shape),
                  pl.BlockSpec((nf, 2 * n2, ct), lambda j, b: (0, 0, j), pipeline_mode=pl.Buffered(1)),
                  seq_blk(0), seq_blk(1), seq_blk(2), seq_blk(3)]
        + conv_blk(0) + conv_blk(1) + conv_blk(2),
        out_specs=pl.BlockSpec((None, n1h, n2, ct), lambda j, b: (b, 0, 0, j)),
        out_shape=jax.ShapeDtypeStruct((bsz, n1h, n2, c), BF16),
        scratch_shapes=[pltpu.VMEM((n1h, n2, ct), F32), pltpu.VMEM((n1h, n2, ct), F32),
                        pltpu.VMEM((nf, 2 * n2, ct), BF16), pltpu.VMEM((nf, 2 * n2, ct), BF16)],
        compiler_params=_cparams("parallel", "arbitrary"),
        name="hyena_branch",
    )(fk, ik, gfs, gbs, hspec, proj4, proj4, proj4, proj4, conv_w, cb, conv_w, cb, conv_w, cb)
    return ya.reshape(bsz * seq, c)


def _gmlp_kernel(u_ref, v_ref, gb_ref, ng_ref, ws_ref, bias_ref, o_ref):
    vn = _rms(v_ref[...].astype(F32), ng_ref[...]).astype(BF16)
    tr, c = vn.shape
    gc = c // GM_GROUPS
    for n in range(tr // CHUNK):
        rows = slice(n * CHUNK, (n + 1) * CHUNK)
        for g in range(GM_GROUPS):
            cols = slice(g * gc, (g + 1) * gc)
            s = jnp.dot(ws_ref[g], vn[rows, cols], preferred_element_type=F32) + bias_ref[:, cols]
            y = u_ref[rows, cols].astype(F32) * s * _silu(gb_ref[rows, cols].astype(F32))
            o_ref[rows, cols] = y.astype(o_ref.dtype)


def _out_ple_tail(ya_ref, yb_ref, h_ref, p_ref, wa_ref, wb_ref, pg_ref, gw_ref, up_ref, fg_ref, o_ref, final):
    mix = (jnp.dot(ya_ref[...].astype(BF16), wa_ref[...], preferred_element_type=F32)
           + jnp.dot(yb_ref[...].astype(BF16), wb_ref[...], preferred_element_type=F32))
    h1 = h_ref[...] + mix
    r = _rms(h1, pg_ref[...]).astype(BF16)
    gate = jax.nn.sigmoid(jnp.dot(r, gw_ref[...], preferred_element_type=F32))
    up = jnp.dot(p_ref[...].astype(BF16), up_ref[...], preferred_element_type=F32)
    h2 = h1 + up * gate
    if final:
        h2 = _rms(h2, fg_ref[...])
    o_ref[...] = h2


def _out_ple_kernel(ya_ref, yb_ref, *rest, final):
    _out_ple_tail(ya_ref, yb_ref, *rest, final)


def _out_ple_gmlp_kernel(ya_ref, u_ref, v_ref, gb_ref, ng_ref, ws_ref, bias_ref, *rest, final):
    *tail, yb_ref = rest
    _gmlp_kernel(u_ref, v_ref, gb_ref, ng_ref, ws_ref, bias_ref, yb_ref)
    _out_ple_tail(ya_ref, yb_ref, *tail, final)


def out_ple(ya, yb, h, p, layer, w_out, ple_g, gate_w, ple_up, final_g, final, gmlp=None):
    m, d = h.shape
    ca = ya.shape[1]
    tm = min(1024, m)
    rowblk = lambda a: pl.BlockSpec((tm, a.shape[1]), lambda i: (i, 0))
    full = lambda a: _resident(a.shape)
    wa = w_out[:ca].astype(BF16)
    wb = w_out[ca:].astype(BF16)
    tail = (h, p, wa, wb, ple_g.reshape(1, d), gate_w.astype(BF16), ple_up.astype(BF16), final_g.reshape(1, d))
    pspec = pl.BlockSpec((None, tm, p.shape[2]), lambda i: (layer, i, 0))
    tail_specs = [rowblk(h), pspec] + [full(a) for a in tail[2:]]
    if gmlp is None:
        kern, args, specs, scratch = _out_ple_kernel, (ya, yb), [rowblk(ya), rowblk(yb)], []
    else:
        proj, col0, norm_g, ws, bs = gmlp
        c = (proj.shape[1] - col0) // 3
        off = col0 // c
        bias = jnp.repeat(bs.T, c // GM_GROUPS, axis=1)
        blk = lambda o: pl.BlockSpec((tm, c), lambda i: (i, off + o))
        extra = (norm_g.reshape(1, c), ws.astype(BF16), bias)
        kern, args = _out_ple_gmlp_kernel, (ya, proj, proj, proj) + extra
        specs = [rowblk(ya), blk(0), blk(1), blk(2)] + [full(a) for a in extra]
        scratch = [pltpu.VMEM((tm, c), BF16)]
    return pl.pallas_call(
        functools.partial(kern, final=final),
        grid=(m // tm,),
        in_specs=specs + tail_specs,
        out_specs=pl.BlockSpec((tm, d), lambda i: (i, 0)),
        out_shape=jax.ShapeDtypeStruct((m, d), F32),
        scratch_shapes=scratch,
        compiler_params=_cparams("parallel"),
        name="out_ple",
    )(*args, *tail)


POOL_HALO = 16


def _pool_kernel(x_ref, prev_ref, next_ref, gc_ref, band_ref, top_ref, bot_ref, w_ref, b_ref, sc_ref, o_ref,
                 *, seq):
    tr, c = x_ref.shape
    hl = POOL_HALO
    i = pl.program_id(0)
    t0 = (i * tr) % seq
    zeros = jnp.zeros((hl, c), BF16)
    prev = jnp.where(t0 == 0, zeros, prev_ref[...])
    nxt = jnp.where(t0 + tr == seq, zeros, next_ref[...])
    t = t0 + lax.broadcasted_iota(jnp.int32, (tr, 1), 0)
    gcw = c // len(POOL_WINDOWS)
    for g, w in enumerate(POOL_WINDOWS):
        cols = slice(g * gcw, (g + 1) * gcw)
        acc = jnp.dot(band_ref[g], x_ref[:, cols], preferred_element_type=F32)
        top = acc[:hl] + jnp.dot(top_ref[g], prev[:, cols], preferred_element_type=F32)
        bot = acc[tr - hl:] + jnp.dot(bot_ref[g], nxt[:, cols], preferred_element_type=F32)
        acc = jnp.concatenate([top, acc[hl:tr - hl], bot], axis=0)
        cnt = (jnp.minimum(t + w // 2, seq) - jnp.maximum(t - w // 2, 0)).astype(F32)
        dlt = acc * (1.0 / cnt) - x_ref[:, cols].astype(F32)
        y = jnp.dot(dlt.astype(BF16), w_ref[g], preferred_element_type=F32) + b_ref[:, cols]
        o_ref[:, cols] = (y * sc_ref[:, cols] * _silu(gc_ref[:, cols].astype(F32))).astype(o_ref.dtype)


def _pool_bands(tr):
    hl = POOL_HALO
    ng = len(POOL_WINDOWS)
    band = np.zeros((ng, tr, tr), np.float32)
    top = np.zeros((ng, hl, hl), np.float32)
    bot = np.zeros((ng, hl, hl), np.float32)
    t = np.arange(tr)[:, None]
    s = np.arange(tr)[None, :]
    a = np.arange(hl)[:, None]
    j = np.arange(hl)[None, :]
    for g, w in enumerate(POOL_WINDOWS):
        band[g] = (s - t >= -(w // 2)) & (s - t < w // 2)
        top[g] = (j - hl) - a >= -(w // 2)
        bot[g] = (hl + j) - a < w // 2
    return band, top, bot


def pool_branch(proj, pool_w, pool_b, pool_scale, seq, c):
    m = proj.shape[0]
    tr = min(512, seq)
    hb = tr // POOL_HALO
    nblk = m // POOL_HALO
    band, top, bot = (jnp.asarray(a, dtype=BF16) for a in _pool_bands(tr))
    return pl.pallas_call(
        functools.partial(_pool_kernel, seq=seq),
        grid=(m // tr,),
        in_specs=[pl.BlockSpec((tr, c), lambda i: (i, 0)),
                  pl.BlockSpec((POOL_HALO, c), lambda i: (jnp.maximum(i * hb - 1, 0), 0)),
                  pl.BlockSpec((POOL_HALO, c), lambda i: (jnp.minimum((i + 1) * hb, nblk - 1), 0)),
                  pl.BlockSpec((tr, c), lambda i: (i, 1)),
                  _resident(band.shape), _resident(top.shape), _resident(bot.shape),
                  _resident(pool_w.shape), _resident((1, c)), _resident((1, c))],
        out_specs=pl.BlockSpec((tr, c), lambda i: (i, 0)),
        out_shape=jax.ShapeDtypeStruct((m, c), BF16),
        compiler_params=_cparams("parallel"),
        name="pool_branch",
    )(proj, proj, proj, proj, band, top, bot, pool_w.astype(BF16), pool_b.reshape(1, c),
      pool_scale.reshape(1, c))


NA_KBLK = 16
NA_BAND = 2
NA_ROWS_PER_STEP = 32
NA_LOOKAHEAD = 6
LOG2E = math.log2(math.e)


def _natten_bias_table(rpb):
    nh = rpb.shape[0]
    nd = 2 * NA_KW - 1
    first_blk = np.zeros(GRID_W, np.int64)
    for qa, qb, b0 in _natten_rects():
        first_blk[qa:qb] = b0
    q = np.arange(GRID_W)[:, None, None]
    kc = (first_blk[:, None, None] + np.arange(NA_BAND)[None, :, None]) * NA_KBLK + np.arange(NA_KBLK)[None, None, :]
    start = np.clip(q - NA_KW // 2, 0, GRID_W - NA_KW)
    inside = (kc >= start) & (kc < start + NA_KW)
    onehot = ((kc - q + (NA_KW - 1))[..., None] == np.arange(nd)) & inside[..., None]
    o = np.arange(NA_KH)[:, None]
    j = np.arange(NA_KH)[None, :]
    rows = rpb.astype(F32)[:, j - o + (NA_KH - 1), :].reshape(nh // 2, 2, NA_KH, NA_KH, nd)
    t = jnp.einsum('phojd,qbkd->pohqbjk', rows, jnp.asarray(onehot.astype(np.float32)),
                   precision=lax.Precision.HIGHEST)
    mask = np.where(inside, 0.0, NEG_BIG).astype(np.float32)[None, None, None, :, :, None, :]
    t = t * LOG2E + jnp.asarray(mask)
    return t.reshape(nh // 2, NA_KH, 2 * GRID_W, NA_BAND * NA_KH * NA_KBLK)


def _natten_rects():
    nblk = GRID_W // NA_KBLK
    rects = []
    for q0 in range(0, GRID_W, 8):
        lo = min(max(q0 - NA_KW // 2, 0), GRID_W - NA_KW)
        hi = min(max(q0 + 7 - NA_KW // 2, 0), GRID_W - NA_KW) + NA_KW
        b0 = min(lo // NA_KBLK, nblk - NA_BAND)
        assert (hi - 1) // NA_KBLK < b0 + NA_BAND
        if rects and rects[-1][2] == b0:
            rects[-1] = (rects[-1][0], q0 + 8, b0)
        else:
            rects.append((q0, q0 + 8, b0))
    return rects


def _natten_kernel(q_ref, k_ref, v_ref, gd_ref, t_ref, o_ref, *, rows):
    rg = pl.program_id(2)
    hd = NA_HEAD_DIM
    lane = lax.broadcasted_iota(jnp.int32, (GRID_W, 2 * hd), 1)
    first = lane < hd
    scale = hd ** -0.5 * LOG2E
    nblk = GRID_W // NA_KBLK
    lanes_per_blk = NA_KH * NA_KBLK
    boxes = [(hh * GRID_W + qa, hh * GRID_W + qb, b0) for hh in range(2) for qa, qb, b0 in _natten_rects()]

    def key_start(rr):
        r = rg * NA_ROWS_PER_STEP + rr
        rs = jnp.clip(r - NA_KH // 2, 0, rows - NA_KH)
        return r - rs, pl.multiple_of(rs * GRID_W, GRID_W)

    def key_block(ref, k0):
        blk = ref[pl.ds(k0, NA_KH * GRID_W), :]
        return jnp.concatenate([blk[j * GRID_W + b * NA_KBLK:j * GRID_W + (b + 1) * NA_KBLK]
                                for b in range(nblk) for j in range(NA_KH)], axis=0)

    def scores(rr):
        _, k0 = key_start(rr)
        q2 = q_ref[rr * GRID_W:(rr + 1) * GRID_W, :].astype(F32) * scale
        qs = jnp.concatenate([jnp.where(first, q2, 0.0), jnp.where(first, 0.0, q2)], axis=0).astype(BF16)
        return lax.dot_general(qs, key_block(k_ref, k0), (((1,), (1,)), ((), ())), preferred_element_type=F32)

    pending = [scores(rr) for rr in range(NA_LOOKAHEAD)]
    for rr in range(NA_ROWS_PER_STEP):
        if rr + NA_LOOKAHEAD < NA_ROWS_PER_STEP:
            pending.append(scores(rr + NA_LOOKAHEAD))
        s = pending.pop(0)
        off, k0 = key_start(rr)
        qrows = slice(rr * GRID_W, (rr + 1) * GRID_W)
        sb = jnp.concatenate([s[ra:rb, b0 * lanes_per_blk:(b0 + NA_BAND) * lanes_per_blk]
                              for ra, rb, b0 in boxes], axis=0) + t_ref[off]
        p = jnp.exp2(sb - jnp.max(sb, axis=-1, keepdims=True))
        den = jnp.sum(p, axis=-1, keepdims=True)
        p_rows = []
        for ra, rb, b0 in boxes:
            parts = [p[ra:rb]]
            if b0 > 0:
                parts.insert(0, jnp.zeros((rb - ra, b0 * lanes_per_blk), F32))
            if b0 + NA_BAND < nblk:
                parts.append(jnp.zeros((rb - ra, (nblk - b0 - NA_BAND) * lanes_per_blk), F32))
            p_rows.append(jnp.concatenate(parts, axis=1))
        pm = jnp.concatenate(p_rows, axis=0).astype(BF16)
        o2 = jnp.dot(pm, key_block(v_ref, k0), preferred_element_type=F32) / den
        att = jnp.where(first, o2[:GRID_W], o2[GRID_W:])
        o_ref[qrows, :] = (att * _silu(gd_ref[qrows, :].astype(F32))).astype(o_ref.dtype)


def natten_branch(proj, rpb, bsz, seq, c, col0):
    rows = seq // GRID_W
    assert rows >= NA_KH and rows % NA_ROWS_PER_STEP == 0
    table = _natten_bias_table(rpb)
    pw = 2 * NA_HEAD_DIM
    npair = c // pw
    off = col0 // pw
    tq = NA_ROWS_PER_STEP * GRID_W
    nrg = rows // NA_ROWS_PER_STEP
    qblk = lambda o: pl.BlockSpec((tq, pw), lambda b, hp, rg: (b * nrg + rg, off + o * npair + hp))
    kvblk = lambda o: pl.BlockSpec((seq, pw), lambda b, hp, rg: (b, off + o * npair + hp))
    return pl.pallas_call(
        functools.partial(_natten_kernel, rows=rows),
        grid=(bsz, npair, nrg),
        in_specs=[qblk(0), kvblk(1), kvblk(2), qblk(3),
                  pl.BlockSpec((None,) + table.shape[1:], lambda b, hp, rg: (hp, 0, 0, 0))],
        out_specs=pl.BlockSpec((tq, pw), lambda b, hp, rg: (b * nrg + rg, hp)),
        out_shape=jax.ShapeDtypeStruct((bsz * seq, c), BF16),
        compiler_params=_cparams("parallel", "parallel", "parallel"),
        name="natten_branch",
    )(proj, proj, proj, proj, table)


def even_layer_mix(h, norm_g, w_in, j, conv_w, conv_b, hy_w0, hy_b0, hy_w1, hy_b1, hy_w2, hy_b2, hy_wout,
                   hy_freq, hy_d, bsz, seq):
    c = h.shape[1]
    proj = norm_matmul(h, norm_g, w_in, j)
    kc_raw, ss = hyena_filter(seq, c, hy_w0, hy_b0, hy_w1, hy_b1, hy_w2, hy_b2, hy_wout, hy_freq)
    ya = hyena_branch(proj, conv_w, conv_b, hy_d, kc_raw, ss, bsz, seq, c)
    return ya, proj


def odd_layer_mix(h, norm_g, w_in, j, pool_w, pool_b, pool_scale, rpb, bsz, seq):
    c = h.shape[1]
    proj = norm_matmul(h, norm_g, w_in, j)
    yc = pool_branch(proj, pool_w, pool_b, pool_scale, seq, c)
    yd = natten_branch(proj, rpb, bsz, seq, c, 2 * c)
    return yc, yd


def kernel(x, p, norm_g, final_g, ev_w_in, ev_conv_w, ev_conv_b, hy_w0, hy_b0, hy_w1, hy_b1, hy_w2, hy_b2, hy_wout, hy_freq, hy_d, gm_norm_g, gm_ws, gm_bs, ev_w_out, od_w_in, pool_w, pool_b, pool_scale, na_rpb, od_w_out, ple_up, ple_gate_w, ple_g):
    bsz, seq, d = x.shape
    depth = p.shape[0]
    h = x.reshape(bsz * seq, d)
    ev_w_in16, od_w_in16 = ev_w_in.astype(BF16), od_w_in.astype(BF16)
    for i in range(depth):
        j = i // 2
        if i % 2 == 0:
            ya, proj = even_layer_mix(h, norm_g[i], ev_w_in16, j, ev_conv_w[j], ev_conv_b[j], hy_w0[j], hy_b0[j],
                                      hy_w1[j], hy_b1[j], hy_w2[j], hy_b2[j], hy_wout[j], hy_freq[j], hy_d[j],
                                      bsz, seq)
            yb, gmlp, w_out = None, (proj, 4 * d, gm_norm_g[j], gm_ws[j], gm_bs[j]), ev_w_out[j]
        else:
            ya, yb = odd_layer_mix(h, norm_g[i], od_w_in16, j, pool_w[j], pool_b[j], pool_scale[j], na_rpb[j],
                                   bsz, seq)
            gmlp, w_out = None, od_w_out[j]
        h = out_ple(ya, yb, h, p.reshape(depth, bsz * seq, -1), i, w_out, ple_g[i], ple_gate_w[i], ple_up[i],
                    final_g, final=(i == depth - 1), gmlp=gmlp)
    return h.reshape(bsz, seq, d)
```

```python
import functools
import math

import jax
import jax.numpy as jnp
import numpy as np
from jax import lax
from jax.experimental import pallas as pl
from jax.experimental.pallas import tpu as pltpu

F32 = jnp.float32
BF16 = jnp.bfloat16

EPS = 1e-6
GRID_W = 64
PLE_DIM = 256
HY_BANDS = 16
HY_FAST_DECAY = 0.3
HY_SLOW_DECAY = 1.5
HY_TARGET = 1e-2
GM_GROUPS = 8
CHUNK = 128
POOL_WINDOWS = (2, 4, 8, 16)
NA_HEADS = 16
NA_HEAD_DIM = 64
NA_KH = 8
NA_KW = 16
FFT_N2 = 64
NEG_BIG = -1e30
V7X_VMEM_BYTES = 64 * 1024 * 1024
VMEM_LIMIT = V7X_VMEM_BYTES - 8 * 1024 * 1024


def _cparams(*sem):
    return pltpu.CompilerParams(dimension_semantics=sem, vmem_limit_bytes=VMEM_LIMIT)


def _silu(x):
    return x * jax.nn.sigmoid(x)


def _rms(x, g):
    return x * lax.rsqrt(jnp.mean(x * x, axis=-1, keepdims=True) + EPS) * g


def _resident(shape):
    return pl.BlockSpec(shape, lambda *_: (0,) * len(shape), pipeline_mode=pl.Buffered(1))


def _norm_matmul_kernel(h_ref, g_ref, w_ref, o_ref, *, tn):
    hn = _rms(h_ref[...], g_ref[...]).astype(BF16)
    for j in range(o_ref.shape[1] // tn):
        cols = slice(j * tn, (j + 1) * tn)
        o_ref[:, cols] = jnp.dot(hn, w_ref[:, cols], preferred_element_type=F32).astype(o_ref.dtype)


def norm_matmul(h, g, w, layer):
    m, d = h.shape
    n = w.shape[2]
    tm = min(512, m)
    tn = min(512, n)
    return pl.pallas_call(
        functools.partial(_norm_matmul_kernel, tn=tn),
        grid=(m // tm,),
        in_specs=[pl.BlockSpec((tm, d), lambda i: (i, 0)),
                  _resident((1, d)),
                  pl.BlockSpec((None, d, n), lambda i: (layer, 0, 0), pipeline_mode=pl.Buffered(1))],
        out_specs=pl.BlockSpec((tm, n), lambda i: (i, 0)),
        out_shape=jax.ShapeDtypeStruct((m, n), BF16),
        compiler_params=_cparams("parallel"),
        name="norm_matmul",
    )(h, g.reshape(1, d), w)


def _short_conv(x, w, b):
    n = x.shape[0]
    row = lax.broadcasted_iota(jnp.int32, x.shape, 0)
    xm = jnp.where(row == 0, 0.0, pltpu.roll(x, 1, 0))
    xp = jnp.where(row == n - 1, 0.0, pltpu.roll(x, n - 1, 0))
    return xm * w[0:1] + x * w[1:2] + xp * w[2:3] + b


def _short_conv_block(x_ref, w_ref, b_ref):
    n1h, n2, ct = x_ref.shape
    return _short_conv(x_ref[...].astype(F32).reshape(n1h * n2, ct), w_ref[...], b_ref[...])


def _positional_features(seq):
    f32 = np.float32
    t = np.linspace(0.0, 1.0, seq, dtype=f32)[:, None]
    ang = (f32(2.0 * math.pi) * np.arange(seq, dtype=f32)[:, None] / f32(seq)).astype(f32)
    bands = np.linspace(1e-4, HY_BANDS - 1, HY_BANDS, dtype=f32)[None, :]
    ba = (bands * ang).astype(f32)
    feats = np.concatenate([t, np.cos(ba), -np.sin(ba)], axis=-1).astype(f32)
    feats2 = np.concatenate([feats, feats[:1], feats[1:][::-1]], axis=0)
    pad = np.zeros((2 * seq, 128 - feats2.shape[1]), f32)
    return np.concatenate([feats2, pad], axis=1)


def _filter_kernel(feat_ref, w0_ref, b0_ref, w1_ref, b1_ref, w2_ref, b2_ref, wo_ref, fr_ref, dl_ref,
                   kc_ref, ss_ref, *, seq, tr):
    i = pl.program_id(0)
    fr = fr_ref[...]

    def dot3(x, w3_ref):
        hi = x.astype(BF16)
        lo = (x - hi.astype(F32)).astype(BF16)
        return jnp.dot(jnp.concatenate([hi, lo, hi], axis=1), w3_ref[...], preferred_element_type=F32)

    h = jnp.sin(fr * (dot3(feat_ref[...], w0_ref) + b0_ref[...]))
    h = jnp.sin(fr * (dot3(h, w1_ref) + b1_ref[...]))
    h = jnp.sin(fr * (dot3(h, w2_ref) + b2_ref[...]))
    k = dot3(h, wo_ref)
    s = i * tr + lax.broadcasted_iota(jnp.int32, (tr, 1), 0)
    lag = jnp.where(s < seq, s, 2 * seq - s).astype(F32)
    t = lag * (1.0 / (seq - 1))
    k = jnp.where(s == seq, 0.0, k * jnp.exp(-t * dl_ref[...]))
    kc_ref[...] = k

    @pl.when(i == 0)
    def _():
        ss_ref[...] = jnp.zeros_like(ss_ref)

    ss_ref[...] += jnp.sum(k * k, axis=0, keepdims=True)


def hyena_filter(seq, c, w0, b0, w1, b1, w2, b2, wout, freq):
    feats = jnp.asarray(_positional_features(seq))
    hid = w0.shape[1]
    w0p = jnp.pad(w0.astype(F32), ((0, 128 - w0.shape[0]), (0, 0)))
    max_decay = math.log(HY_TARGET) / HY_FAST_DECAY
    min_decay = math.log(HY_TARGET) / HY_SLOW_DECAY
    deltas = jnp.asarray(np.abs(np.linspace(min_decay, max_decay, c, dtype=np.float32)))[None, :]
    tr = min(512, seq)
    half = seq // tr
    full = lambda a: pl.BlockSpec(a.shape, lambda i: (0,) * a.ndim)
    row = lambda a: a.reshape(1, -1)

    def split3(w):
        hi = w.astype(BF16)
        lo = (w - hi.astype(F32)).astype(BF16)
        return jnp.concatenate([hi, hi, lo], axis=0)

    args = (feats, split3(w0p), row(b0), split3(w1), row(b1), split3(w2), row(b2), split3(wout), row(freq),
            deltas)
    specs = [full(a) for a in args[1:]]
    specs[6] = pl.BlockSpec((3 * hid, c), lambda i: (0, i // half))
    return pl.pallas_call(
        functools.partial(_filter_kernel, seq=seq, tr=tr),
        grid=(2 * half,),
        in_specs=[pl.BlockSpec((tr, 128), lambda i: (i, 0))] + specs,
        out_specs=[pl.BlockSpec((tr, c), lambda i: (i, 0)), pl.BlockSpec((1, c), lambda i: (0, 0))],
        out_shape=[jax.ShapeDtypeStruct((2 * seq, c), F32), jax.ShapeDtypeStruct((1, c), F32)],
        compiler_params=_cparams("arbitrary"),
        name="hyena_filter",
    )(*args)


def _dft_tables(seq):
    n = 2 * seq
    n2 = FFT_N2
    n1 = n // n2
    f1 = np.arange(n1)[:, None]
    s1 = np.arange(n1)[None, :]
    a = 2.0 * np.pi * (f1 * s1 % n1) / n1
    fa = np.empty((2 * n1, n1), np.float64)
    fa[0::2] = np.cos(a)
    fa[1::2] = -np.sin(a)
    ia = np.empty((n1 // 2, 2 * n1), np.float64)
    at = a.T[: n1 // 2]
    ia[:, 0::2] = np.cos(at) / n
    ia[:, 1::2] = -np.sin(at) / n
    f1b = np.arange(n1)[:, None, None]
    f2 = np.arange(n2)[None, :, None]
    s2 = np.arange(n2)[None, None, :]
    ph = 2.0 * np.pi * ((s2 * f2 * n1 + s2 * f1b) % n) / n
    gr, gi = np.cos(ph), -np.sin(ph)
    gf = np.concatenate([np.concatenate([gr, -gi], axis=2),
                         np.concatenate([gi, gr], axis=2)], axis=1)
    er, ei = np.transpose(gr, (0, 2, 1)), -np.transpose(gi, (0, 2, 1))
    gb = np.concatenate([np.concatenate([er, -ei], axis=2),
                         np.concatenate([ei, er], axis=2)], axis=1)
    return fa.astype(np.float32), ia.astype(np.float32), gf.astype(np.float32), gb.astype(np.float32)


def _filter_stage_a_kernel(wh_ref, wl_ref, x_ref, o_ref):
    n1, n2, ct = x_ref.shape
    nf = o_ref.shape[0]
    for oc in range(n2 // KRON):
        x8 = x_ref[:, oc * KRON:(oc + 1) * KRON, :].reshape(n1 * KRON, ct)
        hi = x8.astype(BF16)
        lo = (x8 - hi.astype(F32)).astype(BF16)
        y = (jnp.dot(wh_ref[...], hi, preferred_element_type=F32)
             + jnp.dot(wh_ref[...], lo, preferred_element_type=F32)
             + jnp.dot(wl_ref[...], hi, preferred_element_type=F32))
        o_ref[:, oc * 2 * KRON:(oc + 1) * 2 * KRON, :] = y.reshape(nf, 2 * KRON, ct)


def filter_stage_a(w, x3):
    n1, n2, c = x3.shape
    nf = w.shape[0] // (2 * KRON)
    ct = min(256, c)
    wh = w.astype(BF16)
    wl = (w - wh.astype(F32)).astype(BF16)
    return pl.pallas_call(
        _filter_stage_a_kernel,
        grid=(c // ct,),
        in_specs=[_resident(wh.shape), _resident(wl.shape),
                  pl.BlockSpec((n1, n2, ct), lambda j: (0, 0, j))],
        out_specs=pl.BlockSpec((nf, 2 * n2, ct), lambda j: (0, 0, j)),
        out_shape=jax.ShapeDtypeStruct((nf, 2 * n2, c), F32),
        compiler_params=_cparams("parallel"),
        name="filter_stage_a",
    )(wh, wl, x3)


def _bmm(g, x, precise):
    dn = (((2,), (1,)), ((0,), (0,)))
    if precise:
        return lax.dot_general(g, x, dn, precision=lax.Precision.HIGHEST, preferred_element_type=F32)
    return lax.dot_general(g, x.astype(BF16), dn, preferred_element_type=F32)


def _filter_spectrum_kernel(g_ref, x_ref, ss_ref, d_ref, o_ref):
    x = x_ref[...]
    hi = x.astype(BF16)
    lo = (x - hi.astype(F32)).astype(BF16)
    spec = _bmm(g_ref[...], jnp.concatenate([hi, lo, hi], axis=1), False) * lax.rsqrt(ss_ref[...] + EPS)
    is_real = lax.broadcasted_iota(jnp.int32, spec.shape, 1) < spec.shape[1] // 2
    o_ref[...] = spec + jnp.where(is_real, d_ref[...], 0.0)


def _largest_divisor(n, cap):
    return max(d for d in range(1, cap + 1) if n % d == 0)


def filter_spectrum(gf, x, ss, d):
    nf = gf.shape[0]
    _, k2, c = x.shape
    ft = _largest_divisor(nf, STAGE_B_CHUNK)
    ct = min(512, c)
    g_hi = gf.astype(BF16)
    g_lo = (gf - g_hi.astype(F32)).astype(BF16)
    g3 = jnp.concatenate([g_hi, g_hi, g_lo], axis=2)
    return pl.pallas_call(
        _filter_spectrum_kernel,
        grid=(nf // ft, c // ct),
        in_specs=[pl.BlockSpec((ft, k2, 3 * k2), lambda i, j: (i, 0, 0)),
                  pl.BlockSpec((ft, k2, ct), lambda i, j: (i, 0, j)),
                  pl.BlockSpec((1, ct), lambda i, j: (0, j)),
                  pl.BlockSpec((1, ct), lambda i, j: (0, j))],
        out_specs=pl.BlockSpec((ft, k2, ct), lambda i, j: (i, 0, j)),
        out_shape=jax.ShapeDtypeStruct((nf, k2, c), F32),
        compiler_params=_cparams("parallel", "parallel"),
        name="filter_spectrum",
    )(g3, x, ss, d.reshape(1, c))


KRON = 8
STAGE_B_CHUNK = 16


def _hyena_kernel(fk_ref, ik_ref, gf_ref, gb_ref, h_ref, x0_ref, x1_ref, v_ref, ga_ref,
                  w0_ref, b0_ref, w1_ref, b1_ref, wv_ref, bv_ref, o_ref, seq_ref, xg_ref, za_ref, zb_ref):
    n1h, n2, ct = x1_ref.shape
    nf = za_ref.shape[0]
    z = _short_conv_block(x1_ref, w1_ref, b1_ref) * _short_conv_block(v_ref, wv_ref, bv_ref)
    seq_ref[...] = z.reshape(n1h, n2, ct)
    gate = _silu(ga_ref[...].astype(F32).reshape(n1h * n2, ct))
    xg_ref[...] = (_short_conv_block(x0_ref, w0_ref, b0_ref) * gate).reshape(n1h, n2, ct)
    for oc in range(n2 // KRON):
        x8 = seq_ref[:, oc * KRON:(oc + 1) * KRON, :].reshape(n1h * KRON, ct).astype(BF16)
        y = jnp.dot(fk_ref[...], x8, preferred_element_type=F32)
        za_ref[:, oc * 2 * KRON:(oc + 1) * 2 * KRON, :] = y.reshape(nf, 2 * KRON, ct).astype(BF16)
    ft = _largest_divisor(nf, STAGE_B_CHUNK)
    for f0 in range(0, nf, ft):
        fs = slice(f0, f0 + ft)
        spec = _bmm(gf_ref[fs], za_ref[fs], False)
        sr, si = spec[:, :n2], spec[:, n2:]
        hr, hi = h_ref[fs, :n2], h_ref[fs, n2:]
        prod = jnp.concatenate([sr * hr - si * hi, sr * hi + si * hr], axis=1)
        zb_ref[fs] = _bmm(gb_ref[fs], prod, False).astype(BF16)
    for oc in range(n2 // KRON):
        x8 = zb_ref[:, oc * 2 * KRON:(oc + 1) * 2 * KRON, :].reshape(nf * 2 * KRON, ct)
        y = jnp.dot(ik_ref[...], x8, preferred_element_type=F32).reshape(n1h, KRON, ct)
        sl = slice(oc * KRON, (oc + 1) * KRON)
        o_ref[:, sl, :] = (y * xg_ref[:, sl, :]).astype(o_ref.dtype)


def hyena_branch(proj, conv_w, conv_b, d, kc_raw, ss, bsz, seq, c):
    n2 = FFT_N2
    n1 = 2 * seq // n2
    n1h = n1 // 2
    nf = n1h + 1
    fa, ia, gf, gb = _dft_tables(seq)
    k = np.arange(2 * n2)
    perm = (k // (2 * KRON)) * KRON + (k % KRON) + ((k // KRON) % 2) * n2
    eye = np.eye(KRON, dtype=np.float32)
    kspec = filter_stage_a(jnp.asarray(np.kron(fa[:2 * nf], eye)), kc_raw.reshape(n1, n2, c))
    hspec = filter_spectrum(jnp.asarray(gf[:nf][:, :, perm]), kspec, ss, d)
    fk = jnp.asarray(np.kron(fa[:2 * nf, :n1h], eye), dtype=BF16)
    weight = np.repeat(np.where((np.arange(nf) == 0) | (np.arange(nf) == n1h), 1.0, 2.0), 2)
    ik = jnp.asarray(np.kron(ia[:, :2 * nf] * weight[None, :], eye), dtype=BF16)
    gfs = jnp.asarray(gf[:nf][:, :, perm], dtype=BF16)
    gbs = jnp.asarray(gb[:nf][:, perm, :], dtype=BF16)
    proj4 = proj.reshape(bsz, n1h, n2, proj.shape[1])
    cb = conv_b.reshape(1, -1)
    ct = min(256, c)
    nc = c // ct
    seq_blk = lambda group: pl.BlockSpec((None, n1h, n2, ct), lambda j, b: (b, 0, 0, group * nc + j))
    conv_blk = lambda group: [pl.BlockSpec((3, ct), lambda j, b: (0, group * nc + j)),
                              pl.BlockSpec((1, ct), lambda j, b: (0, group * nc + j))]
    ya = pl.pallas_call(
        _hyena_kernel,
        grid=(nc, bsz),
        in_specs=[_resident(fk.shape), _resident(ik.shape), _resident(gfs.shape), _resident(gbs.shape),
                  pl.BlockSpec((nf, 2 * n2, ct), lambda j, b: (0, 0, j), pipeline_mode=pl.Buffered(1)),
                  seq_blk(0), seq_blk(1), seq_blk(2), seq_blk(3)]
        + conv_blk(0) + conv_blk(1) + conv_blk(2),
        out_specs=pl.BlockSpec((None, n1h, n2, ct), lambda j, b: (b, 0, 0, j)),
        out_shape=jax.ShapeDtypeStruct((bsz, n1h, n2, c), BF16),
        scratch_shapes=[pltpu.VMEM((n1h, n2, ct), F32), pltpu.VMEM((n1h, n2, ct), F32),
                        pltpu.VMEM((nf, 2 * n2, ct), BF16), pltpu.VMEM((nf, 2 * n2, ct), BF16)],
        compiler_params=_cparams("parallel", "arbitrary"),
        name="hyena_branch",
    )(fk, ik, gfs, gbs, hspec, proj4, proj4, proj4, proj4, conv_w, cb, conv_w, cb, conv_w, cb)
    return ya.reshape(bsz * seq, c)


def _gmlp_kernel(u_ref, v_ref, gb_ref, ng_ref, ws_ref, bias_ref, o_ref):
    vn = _rms(v_ref[...].astype(F32), ng_ref[...]).astype(BF16)
    tr, c = vn.shape
    gc = c // GM_GROUPS
    for n in range(tr // CHUNK):
        rows = slice(n * CHUNK, (n + 1) * CHUNK)
        for g in range(GM_GROUPS):
            cols = slice(g * gc, (g + 1) * gc)
            s = jnp.dot(ws_ref[g], vn[rows, cols], preferred_element_type=F32) + bias_ref[:, cols]
            y = u_ref[rows, cols].astype(F32) * s * _silu(gb_ref[rows, cols].astype(F32))
            o_ref[rows, cols] = y.astype(o_ref.dtype)


def _out_ple_tail(ya_ref, yb_ref, h_ref, p_ref, wa_ref, wb_ref, pg_ref, gw_ref, up_ref, fg_ref, o_ref, final):
    mix = (jnp.dot(ya_ref[...].astype(BF16), wa_ref[...], preferred_element_type=F32)
           + jnp.dot(yb_ref[...].astype(BF16), wb_ref[...], preferred_element_type=F32))
    h1 = h_ref[...] + mix
    r = _rms(h1, pg_ref[...]).astype(BF16)
    gate = jax.nn.sigmoid(jnp.dot(r, gw_ref[...], preferred_element_type=F32))
    up = jnp.dot(p_ref[...].astype(BF16), up_ref[...], preferred_element_type=F32)
    h2 = h1 + up * gate
    if final:
        h2 = _rms(h2, fg_ref[...])
    o_ref[...] = h2


def _out_ple_kernel(ya_ref, yb_ref, *rest, final):
    _out_ple_tail(ya_ref, yb_ref, *rest, final)


def _out_ple_gmlp_kernel(ya_ref, u_ref, v_ref, gb_ref, ng_ref, ws_ref, bias_ref, *rest, final):
    *tail, yb_ref = rest
    _gmlp_kernel(u_ref, v_ref, gb_ref, ng_ref, ws_ref, bias_ref, yb_ref)
    _out_ple_tail(ya_ref, yb_ref, *tail, final)


def out_ple(ya, yb, h, p, layer, w_out, ple_g, gate_w, ple_up, final_g, final, gmlp=None):
    m, d = h.shape
    ca = ya.shape[1]
    tm = min(1024, m)
    rowblk = lambda a: pl.BlockSpec((tm, a.shape[1]), lambda i: (i, 0))
    full = lambda a: _resident(a.shape)
    wa = w_out[:ca].astype(BF16)
    wb = w_out[ca:].astype(BF16)
    tail = (h, p, wa, wb, ple_g.reshape(1, d), gate_w.astype(BF16), ple_up.astype(BF16), final_g.reshape(1, d))
    pspec = pl.BlockSpec((None, tm, p.shape[2]), lambda i: (layer, i, 0))
    tail_specs = [rowblk(h), pspec] + [full(a) for a in tail[2:]]
    if gmlp is None:
        kern, args, specs, scratch = _out_ple_kernel, (ya, yb), [rowblk(ya), rowblk(yb)], []
    else:
        proj, col0, norm_g, ws, bs = gmlp
        c = (proj.shape[1] - col0) // 3
        off = col0 // c
        bias = jnp.repeat(bs.T, c // GM_GROUPS, axis=1)
        blk = lambda o: pl.BlockSpec((tm, c), lambda i: (i, off + o))
        extra = (norm_g.reshape(1, c), ws.astype(BF16), bias)
        kern, args = _out_ple_gmlp_kernel, (ya, proj, proj, proj) + extra
        specs = [rowblk(ya), blk(0), blk(1), blk(2)] + [full(a) for a in extra]
        scratch = [pltpu.VMEM((tm, c), BF16)]
    return pl.pallas_call(
        functools.partial(kern, final=final),
        grid=(m // tm,),
        in_specs=specs + tail_specs,
        out_specs=pl.BlockSpec((tm, d), lambda i: (i, 0)),
        out_shape=jax.ShapeDtypeStruct((m, d), F32),
        scratch_shapes=scratch,
        compiler_params=_cparams("parallel"),
        name="out_ple",
    )(*args, *tail)


POOL_HALO = 16


def _pool_kernel(x_ref, prev_ref, next_ref, gc_ref, band_ref, top_ref, bot_ref, w_ref, b_ref, sc_ref, o_ref,
                 *, seq):
    tr, c = x_ref.shape
    hl = POOL_HALO
    i = pl.program_id(0)
    t0 = (i * tr) % seq
    zeros = jnp.zeros((hl, c), BF16)
    prev = jnp.where(t0 == 0, zeros, prev_ref[...])
    nxt = jnp.where(t0 + tr == seq, zeros, next_ref[...])
    t = t0 + lax.broadcasted_iota(jnp.int32, (tr, 1), 0)
    gcw = c // len(POOL_WINDOWS)
    for g, w in enumerate(POOL_WINDOWS):
        cols = slice(g * gcw, (g + 1) * gcw)
        acc = jnp.dot(band_ref[g], x_ref[:, cols], preferred_element_type=F32)
        top = acc[:hl] + jnp.dot(top_ref[g], prev[:, cols], preferred_element_type=F32)
        bot = acc[tr - hl:] + jnp.dot(bot_ref[g], nxt[:, cols], preferred_element_type=F32)
        acc = jnp.concatenate([top, acc[hl:tr - hl], bot], axis=0)
        cnt = (jnp.minimum(t + w // 2, seq) - jnp.maximum(t - w // 2, 0)).astype(F32)
        dlt = acc * (1.0 / cnt) - x_ref[:, cols].astype(F32)
        y = jnp.dot(dlt.astype(BF16), w_ref[g], preferred_element_type=F32) + b_ref[:, cols]
        o_ref[:, cols] = (y * sc_ref[:, cols] * _silu(gc_ref[:, cols].astype(F32))).astype(o_ref.dtype)


def _pool_bands(tr):
    hl = POOL_HALO
    ng = len(POOL_WINDOWS)
    band = np.zeros((ng, tr, tr), np.float32)
    top = np.zeros((ng, hl, hl), np.float32)
    bot = np.zeros((ng, hl, hl), np.float32)
    t = np.arange(tr)[:, None]
    s = np.arange(tr)[None, :]
    a = np.arange(hl)[:, None]
    j = np.arange(hl)[None, :]
    for g, w in enumerate(POOL_WINDOWS):
        band[g] = (s - t >= -(w // 2)) & (s - t < w // 2)
        top[g] = (j - hl) - a >= -(w // 2)
        bot[g] = (hl + j) - a < w // 2
    return band, top, bot


def pool_branch(proj, pool_w, pool_b, pool_scale, seq, c):
    m = proj.shape[0]
    tr = min(512, seq)
    hb = tr // POOL_HALO
    nblk = m // POOL_HALO
    band, top, bot = (jnp.asarray(a, dtype=BF16) for a in _pool_bands(tr))
    return pl.pallas_call(
        functools.partial(_pool_kernel, seq=seq),
        grid=(m // tr,),
        in_specs=[pl.BlockSpec((tr, c), lambda i: (i, 0)),
                  pl.BlockSpec((POOL_HALO, c), lambda i: (jnp.maximum(i * hb - 1, 0), 0)),
                  pl.BlockSpec((POOL_HALO, c), lambda i: (jnp.minimum((i + 1) * hb, nblk - 1), 0)),
                  pl.BlockSpec((tr, c), lambda i: (i, 1)),
                  _resident(band.shape), _resident(top.shape), _resident(bot.shape),
                  _resident(pool_w.shape), _resident((1, c)), _resident((1, c))],
        out_specs=pl.BlockSpec((tr, c), lambda i: (i, 0)),
        out_shape=jax.ShapeDtypeStruct((m, c), BF16),
        compiler_params=_cparams("parallel"),
        name="pool_branch",
    )(proj, proj, proj, proj, band, top, bot, pool_w.astype(BF16), pool_b.reshape(1, c),
      pool_scale.reshape(1, c))


NA_KBLK = 16
NA_BAND = 2
NA_ROWS_PER_STEP = 32
NA_LOOKAHEAD = 6
LOG2E = math.log2(math.e)


def _natten_bias_table(rpb):
    nh = rpb.shape[0]
    nd = 2 * NA_KW - 1
    first_blk = np.zeros(GRID_W, np.int64)
    for qa, qb, b0 in _natten_rects():
        first_blk[qa:qb] = b0
    q = np.arange(GRID_W)[:, None, None]
    kc = (first_blk[:, None, None] + np.arange(NA_BAND)[None, :, None]) * NA_KBLK + np.arange(NA_KBLK)[None, None, :]
    start = np.clip(q - NA_KW // 2, 0, GRID_W - NA_KW)
    inside = (kc >= start) & (kc < start + NA_KW)
    onehot = ((kc - q + (NA_KW - 1))[..., None] == np.arange(nd)) & inside[..., None]
    o = np.arange(NA_KH)[:, None]
    j = np.arange(NA_KH)[None, :]
    rows = rpb.astype(F32)[:, j - o + (NA_KH - 1), :].reshape(nh // 2, 2, NA_KH, NA_KH, nd)
    t = jnp.einsum('phojd,qbkd->pohqbjk', rows, jnp.asarray(onehot.astype(np.float32)),
                   precision=lax.Precision.HIGHEST)
    mask = np.where(inside, 0.0, NEG_BIG).astype(np.float32)[None, None, None, :, :, None, :]
    t = t * LOG2E + jnp.asarray(mask)
    return t.reshape(nh // 2, NA_KH, 2 * GRID_W, NA_BAND * NA_KH * NA_KBLK)


def _natten_rects():
    nblk = GRID_W // NA_KBLK
    rects = []
    for q0 in range(0, GRID_W, 8):
        lo = min(max(q0 - NA_KW // 2, 0), GRID_W - NA_KW)
        hi = min(max(q0 + 7 - NA_KW // 2, 0), GRID_W - NA_KW) + NA_KW
        b0 = min(lo // NA_KBLK, nblk - NA_BAND)
        assert (hi - 1) // NA_KBLK < b0 + NA_BAND
        if rects and rects[-1][2] == b0:
            rects[-1] = (rects[-1][0], q0 + 8, b0)
        else:
            rects.append((q0, q0 + 8, b0))
    return rects


def _natten_kernel(q_ref, k_ref, v_ref, gd_ref, t_ref, o_ref, *, rows):
    rg = pl.program_id(2)
    hd = NA_HEAD_DIM
    lane = lax.broadcasted_iota(jnp.int32, (GRID_W, 2 * hd), 1)
    first = lane < hd
    scale = hd ** -0.5 * LOG2E
    nblk = GRID_W // NA_KBLK
    lanes_per_blk = NA_KH * NA_KBLK
    boxes = [(hh * GRID_W + qa, hh * GRID_W + qb, b0) for hh in range(2) for qa, qb, b0 in _natten_rects()]

    def key_start(rr):
        r = rg * NA_ROWS_PER_STEP + rr
        rs = jnp.clip(r - NA_KH // 2, 0, rows - NA_KH)
        return r - rs, pl.multiple_of(rs * GRID_W, GRID_W)

    def key_block(ref, k0):
        blk = ref[pl.ds(k0, NA_KH * GRID_W), :]
        return jnp.concatenate([blk[j * GRID_W + b * NA_KBLK:j * GRID_W + (b + 1) * NA_KBLK]
                                for b in range(nblk) for j in range(NA_KH)], axis=0)

    def scores(rr):
        _, k0 = key_start(rr)
        q2 = q_ref[rr * GRID_W:(rr + 1) * GRID_W, :].astype(F32) * scale
        qs = jnp.concatenate([jnp.where(first, q2, 0.0), jnp.where(first, 0.0, q2)], axis=0).astype(BF16)
        return lax.dot_general(qs, key_block(k_ref, k0), (((1,), (1,)), ((), ())), preferred_element_type=F32)

    pending = [scores(rr) for rr in range(NA_LOOKAHEAD)]
    for rr in range(NA_ROWS_PER_STEP):
        if rr + NA_LOOKAHEAD < NA_ROWS_PER_STEP:
            pending.append(scores(rr + NA_LOOKAHEAD))
        s = pending.pop(0)
        off, k0 = key_start(rr)
        qrows = slice(rr * GRID_W, (rr + 1) * GRID_W)
        sb = jnp.concatenate([s[ra:rb, b0 * lanes_per_blk:(b0 + NA_BAND) * lanes_per_blk]
                              for ra, rb, b0 in boxes], axis=0) + t_ref[off]
        p = jnp.exp2(sb - jnp.max(sb, axis=-1, keepdims=True))
        den = jnp.sum(p, axis=-1, keepdims=True)
        p_rows = []
        for ra, rb, b0 in boxes:
            parts = [p[ra:rb]]
            if b0 > 0:
                parts.insert(0, jnp.zeros((rb - ra, b0 * lanes_per_blk), F32))
            if b0 + NA_BAND < nblk:
                parts.append(jnp.zeros((rb - ra, (nblk - b0 - NA_BAND) * lanes_per_blk), F32))
            p_rows.append(jnp.concatenate(parts, axis=1))
        pm = jnp.concatenate(p_rows, axis=0).astype(BF16)
        o2 = jnp.dot(pm, key_block(v_ref, k0), preferred_element_type=F32) / den
        att = jnp.where(first, o2[:GRID_W], o2[GRID_W:])
        o_ref[qrows, :] = (att * _silu(gd_ref[qrows, :].astype(F32))).astype(o_ref.dtype)


def natten_branch(proj, rpb, bsz, seq, c, col0):
    rows = seq // GRID_W
    assert rows >= NA_KH and rows % NA_ROWS_PER_STEP == 0
    table = _natten_bias_table(rpb)
    pw = 2 * NA_HEAD_DIM
    npair = c // pw
    off = col0 // pw
    tq = NA_ROWS_PER_STEP * GRID_W
    nrg = rows // NA_ROWS_PER_STEP
    qblk = lambda o: pl.BlockSpec((tq, pw), lambda b, hp, rg: (b * nrg + rg, off + o * npair + hp))
    kvblk = lambda o: pl.BlockSpec((seq, pw), lambda b, hp, rg: (b, off + o * npair + hp))
    return pl.pallas_call(
        functools.partial(_natten_kernel, rows=rows),
        grid=(bsz, npair, nrg),
        in_specs=[qblk(0), kvblk(1), kvblk(2), qblk(3),
                  pl.BlockSpec((None,) + table.shape[1:], lambda b, hp, rg: (hp, 0, 0, 0))],
        out_specs=pl.BlockSpec((tq, pw), lambda b, hp, rg: (b * nrg + rg, hp)),
        out_shape=jax.ShapeDtypeStruct((bsz * seq, c), BF16),
        compiler_params=_cparams("parallel", "parallel", "parallel"),
        name="natten_branch",
    )(proj, proj, proj, proj, table)


def even_layer_mix(h, norm_g, w_in, j, conv_w, conv_b, hy_w0, hy_b0, hy_w1, hy_b1, hy_w2, hy_b2, hy_wout,
                   hy_freq, hy_d, bsz, seq):
    c = h.shape[1]
    proj = norm_matmul(h, norm_g, w_in, j)
    kc_raw, ss = hyena_filter(seq, c, hy_w0, hy_b0, hy_w1, hy_b1, hy_w2, hy_b2, hy_wout, hy_freq)
    ya = hyena_branch(proj, conv_w, conv_b, hy_d, kc_raw, ss, bsz, seq, c)
    return ya, proj


def odd_layer_mix(h, norm_g, w_in, j, pool_w, pool_b, pool_scale, rpb, bsz, seq):
    c = h.shape[1]
    proj = norm_matmul(h, norm_g, w_in, j)
    yc = pool_branch(proj, pool_w, pool_b, pool_scale, seq, c)
    yd = natten_branch(proj, rpb, bsz, seq, c, 2 * c)
    return yc, yd


def kernel(x, p, norm_g, final_g, ev_w_in, ev_conv_w, ev_conv_b, hy_w0, hy_b0, hy_w1, hy_b1, hy_w2, hy_b2, hy_wout, hy_freq, hy_d, gm_norm_g, gm_ws, gm_bs, ev_w_out, od_w_in, pool_w, pool_b, pool_scale, na_rpb, od_w_out, ple_up, ple_gate_w, ple_g):
    bsz, seq, d = x.shape
    depth = p.shape[0]
    h = x.reshape(bsz * seq, d)
    ev_w_in16, od_w_in16 = ev_w_in.astype(BF16), od_w_in.astype(BF16)
    for i in range(depth):
        j = i // 2
        if i % 2 == 0:
            ya, proj = even_layer_mix(h, norm_g[i], ev_w_in16, j, ev_conv_w[j], ev_conv_b[j], hy_w0[j], hy_b0[j],
                                      hy_w1[j], hy_b1[j], hy_w2[j], hy_b2[j], hy_wout[j], hy_freq[j], hy_d[j],
                                      bsz, seq)
            yb, gmlp, w_out = None, (proj, 4 * d, gm_norm_g[j], gm_ws[j], gm_bs[j]), ev_w_out[j]
        else:
            ya, yb = odd_layer_mix(h, norm_g[i], od_w_in16, j, pool_w[j], pool_b[j], pool_scale[j], na_rpb[j],
                                   bsz, seq)
            gmlp, w_out = None, od_w_out[j]
        h = out_ple(ya, yb, h, p.reshape(depth, bsz * seq, -1), i, w_out, ple_g[i], ple_gate_w[i], ple_up[i],
                    final_g, final=(i == depth - 1), gmlp=gmlp)
    return h.reshape(bsz, seq, d)
```

```python
import functools
import math

import jax
import jax.numpy as jnp
import numpy as np
from jax import lax
from jax.experimental import pallas as pl
from jax.experimental.pallas import tpu as pltpu

F32 = jnp.float32
BF16 = jnp.bfloat16

EPS = 1e-6
GRID_W = 64
PLE_DIM = 256
HY_BANDS = 16
HY_FAST_DECAY = 0.3
HY_SLOW_DECAY = 1.5
HY_TARGET = 1e-2
GM_GROUPS = 8
CHUNK = 128
POOL_WINDOWS = (2, 4, 8, 16)
NA_HEADS = 16
NA_HEAD_DIM = 64
NA_KH = 8
NA_KW = 16
FFT_N2 = 64
NEG_BIG = -1e30
V7X_VMEM_BYTES = 64 * 1024 * 1024
VMEM_LIMIT = V7X_VMEM_BYTES - 8 * 1024 * 1024


def _cparams(*sem):
    return pltpu.CompilerParams(dimension_semantics=sem, vmem_limit_bytes=VMEM_LIMIT)


def _silu(x):
    return x * jax.nn.sigmoid(x)


def _rms(x, g):
    return x * lax.rsqrt(jnp.mean(x * x, axis=-1, keepdims=True) + EPS) * g


def _resident(shape):
    return pl.BlockSpec(shape, lambda *_: (0,) * len(shape), pipeline_mode=pl.Buffered(1))


def _norm_matmul_kernel(h_ref, g_ref, w_ref, o_ref, *, tn):
    hn = _rms(h_ref[...], g_ref[...]).astype(BF16)
    for j in range(o_ref.shape[1] // tn):
        cols = slice(j * tn, (j + 1) * tn)
        o_ref[:, cols] = jnp.dot(hn, w_ref[:, cols], preferred_element_type=F32).astype(o_ref.dtype)


def norm_matmul(h, g, w, layer):
    m, d = h.shape
    n = w.shape[2]
    tm = min(512, m)
    tn = min(512, n)
    return pl.pallas_call(
        functools.partial(_norm_matmul_kernel, tn=tn),
        grid=(m // tm,),
        in_specs=[pl.BlockSpec((tm, d), lambda i: (i, 0)),
                  _resident((1, d)),
                  pl.BlockSpec((None, d, n), lambda i: (layer, 0, 0), pipeline_mode=pl.Buffered(1))],
        out_specs=pl.BlockSpec((tm, n), lambda i: (i, 0)),
        out_shape=jax.ShapeDtypeStruct((m, n), BF16),
        compiler_params=_cparams("parallel"),
        name="norm_matmul",
    )(h, g.reshape(1, d), w)


def _short_conv(x, w, b):
    n = x.shape[0]
    row = lax.broadcasted_iota(jnp.int32, x.shape, 0)
    xm = jnp.where(row == 0, 0.0, pltpu.roll(x, 1, 0))
    xp = jnp.where(row == n - 1, 0.0, pltpu.roll(x, n - 1, 0))
    return xm * w[0:1] + x * w[1:2] + xp * w[2:3] + b


def _short_conv_block(x_ref, w_ref, b_ref):
    n1h, n2, ct = x_ref.shape
    return _short_conv(x_ref[...].astype(F32).reshape(n1h * n2, ct), w_ref[...], b_ref[...])


def _positional_features(seq):
    f32 = np.float32
    t = np.linspace(0.0, 1.0, seq, dtype=f32)[:, None]
    ang = (f32(2.0 * math.pi) * np.arange(seq, dtype=f32)[:, None] / f32(seq)).astype(f32)
    bands = np.linspace(1e-4, HY_BANDS - 1, HY_BANDS, dtype=f32)[None, :]
    ba = (bands * ang).astype(f32)
    feats = np.concatenate([t, np.cos(ba), -np.sin(ba)], axis=-1).astype(f32)
    feats2 = np.concatenate([feats, feats[:1], feats[1:][::-1]], axis=0)
    pad = np.zeros((2 * seq, 128 - feats2.shape[1]), f32)
    return np.concatenate([feats2, pad], axis=1)


def _filter_kernel(feat_ref, w0_ref, b0_ref, w1_ref, b1_ref, w2_ref, b2_ref, wo_ref, fr_ref, dl_ref,
                   kc_ref, ss_ref, *, seq, tr):
    i = pl.program_id(0)
    fr = fr_ref[...]

    def dot3(x, w3_ref):
        hi = x.astype(BF16)
        lo = (x - hi.astype(F32)).astype(BF16)
        return jnp.dot(jnp.concatenate([hi, lo, hi], axis=1), w3_ref[...], preferred_element_type=F32)

    h = jnp.sin(fr * (dot3(feat_ref[...], w0_ref) + b0_ref[...]))
    h = jnp.sin(fr * (dot3(h, w1_ref) + b1_ref[...]))
    h = jnp.sin(fr * (dot3(h, w2_ref) + b2_ref[...]))
    k = dot3(h, wo_ref)
    s = i * tr + lax.broadcasted_iota(jnp.int32, (tr, 1), 0)
    lag = jnp.where(s < seq, s, 2 * seq - s).astype(F32)
    t = lag * (1.0 / (seq - 1))
    k = jnp.where(s == seq, 0.0, k * jnp.exp(-t * dl_ref[...]))
    kc_ref[...] = k

    @pl.when(i == 0)
    def _():
        ss_ref[...] = jnp.zeros_like(ss_ref)

    ss_ref[...] += jnp.sum(k * k, axis=0, keepdims=True)


def hyena_filter(seq, c, w0, b0, w1, b1, w2, b2, wout, freq):
    feats = jnp.asarray(_positional_features(seq))
    hid = w0.shape[1]
    w0p = jnp.pad(w0.astype(F32), ((0, 128 - w0.shape[0]), (0, 0)))
    max_decay = math.log(HY_TARGET) / HY_FAST_DECAY
    min_decay = math.log(HY_TARGET) / HY_SLOW_DECAY
    deltas = jnp.asarray(np.abs(np.linspace(min_decay, max_decay, c, dtype=np.float32)))[None, :]
    tr = min(512, seq)
    half = seq // tr
    full = lambda a: pl.BlockSpec(a.shape, lambda i: (0,) * a.ndim)
    row = lambda a: a.reshape(1, -1)

    def split3(w):
        hi = w.astype(BF16)
        lo = (w - hi.astype(F32)).astype(BF16)
        return jnp.concatenate([hi, hi, lo], axis=0)

    args = (feats, split3(w0p), row(b0), split3(w1), row(b1), split3(w2), row(b2), split3(wout), row(freq),
            deltas)
    specs = [full(a) for a in args[1:]]
    specs[6] = pl.BlockSpec((3 * hid, c), lambda i: (0, i // half))
    return pl.pallas_call(
        functools.partial(_filter_kernel, seq=seq, tr=tr),
        grid=(2 * half,),
        in_specs=[pl.BlockSpec((tr, 128), lambda i: (i, 0))] + specs,
        out_specs=[pl.BlockSpec((tr, c), lambda i: (i, 0)), pl.BlockSpec((1, c), lambda i: (0, 0))],
        out_shape=[jax.ShapeDtypeStruct((2 * seq, c), F32), jax.ShapeDtypeStruct((1, c), F32)],
        compiler_params=_cparams("arbitrary"),
        name="hyena_filter",
    )(*args)


def _dft_tables(seq):
    n = 2 * seq
    n2 = FFT_N2
    n1 = n // n2
    f1 = np.arange(n1)[:, None]
    s1 = np.arange(n1)[None, :]
    a = 2.0 * np.pi * (f1 * s1 % n1) / n1
    fa = np.empty((2 * n1, n1), np.float64)
    fa[0::2] = np.cos(a)
    fa[1::2] = -np.sin(a)
    ia = np.empty((n1 // 2, 2 * n1), np.float64)
    at = a.T[: n1 // 2]
    ia[:, 0::2] = np.cos(at) / n
    ia[:, 1::2] = -np.sin(at) / n
    f1b = np.arange(n1)[:, None, None]
    f2 = np.arange(n2)[None, :, None]
    s2 = np.arange(n2)[None, None, :]
    ph = 2.0 * np.pi * ((s2 * f2 * n1 + s2 * f1b) % n) / n
    gr, gi = np.cos(ph), -np.sin(ph)
    gf = np.concatenate([np.concatenate([gr, -gi], axis=2),
                         np.concatenate([gi, gr], axis=2)], axis=1)
    er, ei = np.transpose(gr, (0, 2, 1)), -np.transpose(gi, (0, 2, 1))
    gb = np.concatenate([np.concatenate([er, -ei], axis=2),
                         np.concatenate([ei, er], axis=2)], axis=1)
    return fa.astype(np.float32), ia.astype(np.float32), gf.astype(np.float32), gb.astype(np.float32)


def _filter_stage_a_kernel(wh_ref, wl_ref, x_ref, o_ref):
    n1, n2, ct = x_ref.shape
    nf = o_ref.shape[0]
    for oc in range(n2 // KRON):
        x8 = x_ref[:, oc * KRON:(oc + 1) * KRON, :].reshape(n1 * KRON, ct)
        hi = x8.astype(BF16)
        lo = (x8 - hi.astype(F32)).astype(BF16)
        y = (jnp.dot(wh_ref[...], hi, preferred_element_type=F32)
             + jnp.dot(wh_ref[...], lo, preferred_element_type=F32)
             + jnp.dot(wl_ref[...], hi, preferred_element_type=F32))
        o_ref[:, oc * 2 * KRON:(oc + 1) * 2 * KRON, :] = y.reshape(nf, 2 * KRON, ct)


def filter_stage_a(w, x3):
    n1, n2, c = x3.shape
    nf = w.shape[0] // (2 * KRON)
    ct = min(256, c)
    wh = w.astype(BF16)
    wl = (w - wh.astype(F32)).astype(BF16)
    return pl.pallas_call(
        _filter_stage_a_kernel,
        grid=(c // ct,),
        in_specs=[_resident(wh.shape), _resident(wl.shape),
                  pl.BlockSpec((n1, n2, ct), lambda j: (0, 0, j))],
        out_specs=pl.BlockSpec((nf, 2 * n2, ct), lambda j: (0, 0, j)),
        out_shape=jax.ShapeDtypeStruct((nf, 2 * n2, c), F32),
        compiler_params=_cparams("parallel"),
        name="filter_stage_a",
    )(wh, wl, x3)


def _bmm(g, x, precise):
    dn = (((2,), (1,)), ((0,), (0,)))
    if precise:
        return lax.dot_general(g, x, dn, precision=lax.Precision.HIGHEST, preferred_element_type=F32)
    return lax.dot_general(g, x.astype(BF16), dn, preferred_element_type=F32)


def _filter_spectrum_kernel(g_ref, x_ref, ss_ref, d_ref, o_ref):
    x = x_ref[...]
    hi = x.astype(BF16)
    lo = (x - hi.astype(F32)).astype(BF16)
    spec = _bmm(g_ref[...], jnp.concatenate([hi, lo, hi], axis=1), False) * lax.rsqrt(ss_ref[...] + EPS)
    is_real = lax.broadcasted_iota(jnp.int32, spec.shape, 1) < spec.shape[1] // 2
    o_ref[...] = spec + jnp.where(is_real, d_ref[...], 0.0)


def _largest_divisor(n, cap):
    return max(d for d in range(1, cap + 1) if n % d == 0)


def filter_spectrum(gf, x, ss, d):
    nf = gf.shape[0]
    _, k2, c = x.shape
    ft = _largest_divisor(nf, STAGE_B_CHUNK)
    ct = min(512, c)
    g_hi = gf.astype(BF16)
    g_lo = (gf - g_hi.astype(F32)).astype(BF16)
    g3 = jnp.concatenate([g_hi, g_hi, g_lo], axis=2)
    return pl.pallas_call(
        _filter_spectrum_kernel,
        grid=(nf // ft, c // ct),
        in_specs=[pl.BlockSpec((ft, k2, 3 * k2), lambda i, j: (i, 0, 0)),
                  pl.BlockSpec((ft, k2, ct), lambda i, j: (i, 0, j)),
                  pl.BlockSpec((1, ct), lambda i, j: (0, j)),
                  pl.BlockSpec((1, ct), lambda i, j: (0, j))],
        out_specs=pl.BlockSpec((ft, k2, ct), lambda i, j: (i, 0, j)),
        out_shape=jax.ShapeDtypeStruct((nf, k2, c), F32),
        compiler_params=_cparams("parallel", "parallel"),
        name="filter_spectrum",
    )(g3, x, ss, d.reshape(1, c))


KRON = 8
STAGE_B_CHUNK = 16


def _hyena_kernel(fk_ref, ik_ref, gf_ref, gb_ref, h_ref, x0_ref, x1_ref, v_ref, ga_ref,
                  w0_ref, b0_ref, w1_ref, b1_ref, wv_ref, bv_ref, o_ref, seq_ref, xg_ref, za_ref, zb_ref):
    n1h, n2, ct = x1_ref.shape
    nf = za_ref.shape[0]
    z = _short_conv_block(x1_ref, w1_ref, b1_ref) * _short_conv_block(v_ref, wv_ref, bv_ref)
    seq_ref[...] = z.reshape(n1h, n2, ct)
    gate = _silu(ga_ref[...].astype(F32).reshape(n1h * n2, ct))
    xg_ref[...] = (_short_conv_block(x0_ref, w0_ref, b0_ref) * gate).reshape(n1h, n2, ct)
    for oc in range(n2 // KRON):
        x8 = seq_ref[:, oc * KRON:(oc + 1) * KRON, :].reshape(n1h * KRON, ct).astype(BF16)
        y = jnp.dot(fk_ref[...], x8, preferred_element_type=F32)
        za_ref[:, oc * 2 * KRON:(oc + 1) * 2 * KRON, :] = y.reshape(nf, 2 * KRON, ct).astype(BF16)
    ft = _largest_divisor(nf, STAGE_B_CHUNK)
    for f0 in range(0, nf, ft):
        fs = slice(f0, f0 + ft)
        spec = _bmm(gf_ref[fs], za_ref[fs], False)
        sr, si = spec[:, :n2], spec[:, n2:]
        hr, hi = h_ref[fs, :n2], h_ref[fs, n2:]
        prod = jnp.concatenate([sr * hr - si * hi, sr * hi + si * hr], axis=1)
        zb_ref[fs] = _bmm(gb_ref[fs], prod, False).astype(BF16)
    for oc in range(n2 // KRON):
        x8 = zb_ref[:, oc * 2 * KRON:(oc + 1) * 2 * KRON, :].reshape(nf * 2 * KRON, ct)
        y = jnp.dot(ik_ref[...], x8, preferred_element_type=F32).reshape(n1h, KRON, ct)
        sl = slice(oc * KRON, (oc + 1) * KRON)
        o_ref[:, sl, :] = (y * xg_ref[:, sl, :]).astype(o_ref.dtype)


def hyena_branch(proj, conv_w, conv_b, d, kc_raw, ss, bsz, seq, c):
    n2 = FFT_N2
    n1 = 2 * seq // n2
    n1h = n1 // 2
    nf = n1h + 1
    fa, ia, gf, gb = _dft_tables(seq)
    k = np.arange(2 * n2)
    perm = (k // (2 * KRON)) * KRON + (k % KRON) + ((k // KRON) % 2) * n2
    eye = np.eye(KRON, dtype=np.float32)
    kspec = filter_stage_a(jnp.asarray(np.kron(fa[:2 * nf], eye)), kc_raw.reshape(n1, n2, c))
    hspec = filter_spectrum(jnp.asarray(gf[:nf][:, :, perm]), kspec, ss, d)
    fk = jnp.asarray(np.kron(fa[:2 * nf, :n1h], eye), dtype=BF16)
    weight = np.repeat(np.where((np.arange(nf) == 0) | (np.arange(nf) == n1h), 1.0, 2.0), 2)
    ik = jnp.asarray(np.kron(ia[:, :2 * nf] * weight[None, :], eye), dtype=BF16)
    gfs = jnp.asarray(gf[:nf][:, :, perm], dtype=BF16)
    gbs = jnp.asarray(gb[:nf][:, perm, :], dtype=BF16)
    proj4 = proj.reshape(bsz, n1h, n2, proj.shape[1])
    cb = conv_b.reshape(1, -1)
    ct = min(256, c)
    nc = c // ct
    seq_blk = lambda group: pl.BlockSpec((None, n1h, n2, ct), lambda j, b: (b, 0, 0, group * nc + j))
    conv_blk = lambda group: [pl.BlockSpec((3, ct), lambda j, b: (0, group * nc + j)),
                              pl.BlockSpec((1, ct), lambda j, b: (0, group * nc + j))]
    ya = pl.pallas_call(
        _hyena_kernel,
        grid=(nc, bsz),
        in_specs=[_resident(fk.shape), _resident(ik.shape), _resident(gfs.shape), _resident(gbs.shape),
                  pl.BlockSpec((nf, 2 * n2, ct), lambda j, b: (0, 0, j), pipeline_mode=pl.Buffered(1)),
                  seq_blk(0), seq_blk(1), seq_blk(2), seq_blk(3)]
        + conv_blk(0) + conv_blk(1) + conv_blk(2),
        out_specs=pl.BlockSpec((None, n1h, n2, ct), lambda j, b: (b, 0, 0, j)),
        out_shape=jax.ShapeDtypeStruct((bsz, n1h, n2, c), BF16),
        scratch_shapes=[pltpu.VMEM((n1h, n2, ct), F32), pltpu.VMEM((n1h, n2, ct), F32),
                        pltpu.VMEM((nf, 2 * n2, ct), BF16), pltpu.VMEM((nf, 2 * n2, ct), BF16)],
        compiler_params=_cparams("parallel", "arbitrary"),
        name="hyena_branch",
    )(fk, ik, gfs, gbs, hspec, proj4, proj4, proj4, proj4, conv_w, cb, conv_w, cb, conv_w, cb)
    return ya.reshape(bsz * seq, c)


def _gmlp_kernel(u_ref, v_ref, gb_ref, ng_ref, ws_ref, bias_ref, o_ref):
    vn = _rms(v_ref[...].astype(F32), ng_ref[...]).astype(BF16)
    tr, c = vn.shape
    gc = c // GM_GROUPS
    for n in range(tr // CHUNK):
        rows = slice(n * CHUNK, (n + 1) * CHUNK)
        for g in range(GM_GROUPS):
            cols = slice(g * gc, (g + 1) * gc)
            s = jnp.dot(ws_ref[g], vn[rows, cols], preferred_element_type=F32) + bias_ref[:, cols]
            y = u_ref[rows, cols].astype(F32) * s * _silu(gb_ref[rows, cols].astype(F32))
            o_ref[rows, cols] = y.astype(o_ref.dtype)


def _out_ple_tail(ya_ref, yb_ref, h_ref, p_ref, wa_ref, wb_ref, pg_ref, gw_ref, up_ref, fg_ref, o_ref, final):
    mix = (jnp.dot(ya_ref[...].astype(BF16), wa_ref[...], preferred_element_type=F32)
           + jnp.dot(yb_ref[...].astype(BF16), wb_ref[...], preferred_element_type=F32))
    h1 = h_ref[...] + mix
    r = _rms(h1, pg_ref[...]).astype(BF16)
    gate = jax.nn.sigmoid(jnp.dot(r, gw_ref[...], preferred_element_type=F32))
    up = jnp.dot(p_ref[...].astype(BF16), up_ref[...], preferred_element_type=F32)
    h2 = h1 + up * gate
    if final:
        h2 = _rms(h2, fg_ref[...])
    o_ref[...] = h2


def _out_ple_kernel(ya_ref, yb_ref, *rest, final):
    _out_ple_tail(ya_ref, yb_ref, *rest, final)


def _out_ple_gmlp_kernel(ya_ref, u_ref, v_ref, gb_ref, ng_ref, ws_ref, bias_ref, *rest, final):
    *tail, yb_ref = rest
    _gmlp_kernel(u_ref, v_ref, gb_ref, ng_ref, ws_ref, bias_ref, yb_ref)
    _out_ple_tail(ya_ref, yb_ref, *tail, final)


def out_ple(ya, yb, h, p, layer, w_out, ple_g, gate_w, ple_up, final_g, final, gmlp=None):
    m, d = h.shape
    ca = ya.shape[1]
    tm = min(1024, m)
    rowblk = lambda a: pl.BlockSpec((tm, a.shape[1]), lambda i: (i, 0))
    full = lambda a: _resident(a.shape)
    wa = w_out[:ca].astype(BF16)
    wb = w_out[ca:].astype(BF16)
    tail = (h, p, wa, wb, ple_g.reshape(1, d), gate_w.astype(BF16), ple_up.astype(BF16), final_g.reshape(1, d))
    pspec = pl.BlockSpec((None, tm, p.shape[2]), lambda i: (layer, i, 0))
    tail_specs = [rowblk(h), pspec] + [full(a) for a in tail[2:]]
    if gmlp is None:
        kern, args, specs, scratch = _out_ple_kernel, (ya, yb), [rowblk(ya), rowblk(yb)], []
    else:
        proj, col0, norm_g, ws, bs = gmlp
        c = (proj.shape[1] - col0) // 3
        off = col0 // c
        bias = jnp.repeat(bs.T, c // GM_GROUPS, axis=1)
        blk = lambda o: pl.BlockSpec((tm, c), lambda i: (i, off + o))
        extra = (norm_g.reshape(1, c), ws.astype(BF16), bias)
        kern, args = _out_ple_gmlp_kernel, (ya, proj, proj, proj) + extra
        specs = [rowblk(ya), blk(0), blk(1), blk(2)] + [full(a) for a in extra]
        scratch = [pltpu.VMEM((tm, c), BF16)]
    return pl.pallas_call(
        functools.partial(kern, final=final),
        grid=(m // tm,),
        in_specs=specs + tail_specs,
        out_specs=pl.BlockSpec((tm, d), lambda i: (i, 0)),
        out_shape=jax.ShapeDtypeStruct((m, d), F32),
        scratch_shapes=scratch,
        compiler_params=_cparams("parallel"),
        name="out_ple",
    )(*args, *tail)


POOL_HALO = 16


def _pool_kernel(x_ref, prev_ref, next_ref, gc_ref, band_ref, top_ref, bot_ref, w_ref, b_ref, sc_ref, o_ref,
                 *, seq):
    tr, c = x_ref.shape
    hl = POOL_HALO
    i = pl.program_id(0)
    t0 = (i * tr) % seq
    zeros = jnp.zeros((hl, c), BF16)
    prev = jnp.where(t0 == 0, zeros, prev_ref[...])
    nxt = jnp.where(t0 + tr == seq, zeros, next_ref[...])
    t = t0 + lax.broadcasted_iota(jnp.int32, (tr, 1), 0)
    gcw = c // len(POOL_WINDOWS)
    for g, w in enumerate(POOL_WINDOWS):
        cols = slice(g * gcw, (g + 1) * gcw)
        acc = jnp.dot(band_ref[g], x_ref[:, cols], preferred_element_type=F32)
        top = acc[:hl] + jnp.dot(top_ref[g], prev[:, cols], preferred_element_type=F32)
        bot = acc[tr - hl:] + jnp.dot(bot_ref[g], nxt[:, cols], preferred_element_type=F32)
        acc = jnp.concatenate([top, acc[hl:tr - hl], bot], axis=0)
        cnt = (jnp.minimum(t + w // 2, seq) - jnp.maximum(t - w // 2, 0)).astype(F32)
        dlt = acc * (1.0 / cnt) - x_ref[:, cols].astype(F32)
        y = jnp.dot(dlt.astype(BF16), w_ref[g], preferred_element_type=F32) + b_ref[:, cols]
        o_ref[:, cols] = (y * sc_ref[:, cols] * _silu(gc_ref[:, cols].astype(F32))).astype(o_ref.dtype)


def _pool_bands(tr):
    hl = POOL_HALO
    ng = len(POOL_WINDOWS)
    band = np.zeros((ng, tr, tr), np.float32)
    top = np.zeros((ng, hl, hl), np.float32)
    bot = np.zeros((ng, hl, hl), np.float32)
    t = np.arange(tr)[:, None]
    s = np.arange(tr)[None, :]
    a = np.arange(hl)[:, None]
    j = np.arange(hl)[None, :]
    for g, w in enumerate(POOL_WINDOWS):
        band[g] = (s - t >= -(w // 2)) & (s - t < w // 2)
        top[g] = (j - hl) - a >= -(w // 2)
        bot[g] = (hl + j) - a < w // 2
    return band, top, bot


def pool_branch(proj, pool_w, pool_b, pool_scale, seq, c):
    m = proj.shape[0]
    tr = min(512, seq)
    hb = tr // POOL_HALO
    nblk = m // POOL_HALO
    band, top, bot = (jnp.asarray(a, dtype=BF16) for a in _pool_bands(tr))
    return pl.pallas_call(
        functools.partial(_pool_kernel, seq=seq),
        grid=(m // tr,),
        in_specs=[pl.BlockSpec((tr, c), lambda i: (i, 0)),
                  pl.BlockSpec((POOL_HALO, c), lambda i: (jnp.maximum(i * hb - 1, 0), 0)),
                  pl.BlockSpec((POOL_HALO, c), lambda i: (jnp.minimum((i + 1) * hb, nblk - 1), 0)),
                  pl.BlockSpec((tr, c), lambda i: (i, 1)),
                  _resident(band.shape), _resident(top.shape), _resident(bot.shape),
                  _resident(pool_w.shape), _resident((1, c)), _resident((1, c))],
        out_specs=pl.BlockSpec((tr, c), lambda i: (i, 0)),
        out_shape=jax.ShapeDtypeStruct((m, c), BF16),
        compiler_params=_cparams("parallel"),
        name="pool_branch",
    )(proj, proj, proj, proj, band, top, bot, pool_w.astype(BF16), pool_b.reshape(1, c),
      pool_scale.reshape(1, c))


NA_KBLK = 16
NA_BAND = 2
NA_ROWS_PER_STEP = 64
NA_LOOKAHEAD = 6
LOG2E = math.log2(math.e)


def _natten_bias_table(rpb):
    nh = rpb.shape[0]
    nd = 2 * NA_KW - 1
    first_blk = np.zeros(GRID_W, np.int64)
    for qa, qb, b0 in _natten_rects():
        first_blk[qa:qb] = b0
    q = np.arange(GRID_W)[:, None, None]
    kc = (first_blk[:, None, None] + np.arange(NA_BAND)[None, :, None]) * NA_KBLK + np.arange(NA_KBLK)[None, None, :]
    start = np.clip(q - NA_KW // 2, 0, GRID_W - NA_KW)
    inside = (kc >= start) & (kc < start + NA_KW)
    onehot = ((kc - q + (NA_KW - 1))[..., None] == np.arange(nd)) & inside[..., None]
    o = np.arange(NA_KH)[:, None]
    j = np.arange(NA_KH)[None, :]
    rows = rpb.astype(F32)[:, j - o + (NA_KH - 1), :].reshape(nh // 2, 2, NA_KH, NA_KH, nd)
    t = jnp.einsum('phojd,qbkd->pohqbjk', rows, jnp.asarray(onehot.astype(np.float32)),
                   precision=lax.Precision.HIGHEST)
    mask = np.where(inside, 0.0, NEG_BIG).astype(np.float32)[None, None, None, :, :, None, :]
    t = t * LOG2E + jnp.asarray(mask)
    return t.reshape(nh // 2, NA_KH, 2 * GRID_W, NA_BAND * NA_KH * NA_KBLK)


def _natten_rects():
    nblk = GRID_W // NA_KBLK
    rects = []
    for q0 in range(0, GRID_W, 8):
        lo = min(max(q0 - NA_KW // 2, 0), GRID_W - NA_KW)
        hi = min(max(q0 + 7 - NA_KW // 2, 0), GRID_W - NA_KW) + NA_KW
        b0 = min(lo // NA_KBLK, nblk - NA_BAND)
        assert (hi - 1) // NA_KBLK < b0 + NA_BAND
        if rects and rects[-1][2] == b0:
            rects[-1] = (rects[-1][0], q0 + 8, b0)
        else:
            rects.append((q0, q0 + 8, b0))
    return rects


def _natten_kernel(q_ref, k_ref, v_ref, gd_ref, t_ref, o_ref, *, rows):
    rg = pl.program_id(2)
    hd = NA_HEAD_DIM
    lane = lax.broadcasted_iota(jnp.int32, (GRID_W, 2 * hd), 1)
    first = lane < hd
    scale = hd ** -0.5 * LOG2E
    nblk = GRID_W // NA_KBLK
    lanes_per_blk = NA_KH * NA_KBLK
    boxes = [(hh * GRID_W + qa, hh * GRID_W + qb, b0) for hh in range(2) for qa, qb, b0 in _natten_rects()]

    def key_start(rr):
        r = rg * NA_ROWS_PER_STEP + rr
        rs = jnp.clip(r - NA_KH // 2, 0, rows - NA_KH)
        return r - rs, pl.multiple_of(rs * GRID_W, GRID_W)

    def key_block(ref, k0):
        blk = ref[pl.ds(k0, NA_KH * GRID_W), :]
        return jnp.concatenate([blk[j * GRID_W + b * NA_KBLK:j * GRID_W + (b + 1) * NA_KBLK]
                                for b in range(nblk) for j in range(NA_KH)], axis=0)

    def scores(rr):
        _, k0 = key_start(rr)
        q2 = q_ref[rr * GRID_W:(rr + 1) * GRID_W, :].astype(F32) * scale
        qs = jnp.concatenate([jnp.where(first, q2, 0.0), jnp.where(first, 0.0, q2)], axis=0).astype(BF16)
        return lax.dot_general(qs, key_block(k_ref, k0), (((1,), (1,)), ((), ())), preferred_element_type=F32)

    pending = [scores(rr) for rr in range(NA_LOOKAHEAD)]
    for rr in range(NA_ROWS_PER_STEP):
        if rr + NA_LOOKAHEAD < NA_ROWS_PER_STEP:
            pending.append(scores(rr + NA_LOOKAHEAD))
        s = pending.pop(0)
        off, k0 = key_start(rr)
        qrows = slice(rr * GRID_W, (rr + 1) * GRID_W)
        sb = jnp.concatenate([s[ra:rb, b0 * lanes_per_blk:(b0 + NA_BAND) * lanes_per_blk]
                              for ra, rb, b0 in boxes], axis=0) + t_ref[off]
        p = jnp.exp2(sb - jnp.max(sb, axis=-1, keepdims=True))
        den = jnp.sum(p, axis=-1, keepdims=True)
        p_rows = []
        for ra, rb, b0 in boxes:
            parts = [p[ra:rb]]
            if b0 > 0:
                parts.insert(0, jnp.zeros((rb - ra, b0 * lanes_per_blk), F32))
            if b0 + NA_BAND < nblk:
                parts.append(jnp.zeros((rb - ra, (nblk - b0 - NA_BAND) * lanes_per_blk), F32))
            p_rows.append(jnp.concatenate(parts, axis=1))
        pm = jnp.concatenate(p_rows, axis=0).astype(BF16)
        o2 = jnp.dot(pm, key_block(v_ref, k0), preferred_element_type=F32) / den
        att = jnp.where(first, o2[:GRID_W], o2[GRID_W:])
        o_ref[qrows, :] = (att * _silu(gd_ref[qrows, :].astype(F32))).astype(o_ref.dtype)


def natten_branch(proj, rpb, bsz, seq, c, col0):
    rows = seq // GRID_W
    assert rows >= NA_KH and rows % NA_ROWS_PER_STEP == 0
    table = _natten_bias_table(rpb)
    pw = 2 * NA_HEAD_DIM
    npair = c // pw
    off = col0 // pw
    tq = NA_ROWS_PER_STEP * GRID_W
    nrg = rows // NA_ROWS_PER_STEP
    qblk = lambda o: pl.BlockSpec((tq, pw), lambda b, hp, rg: (b * nrg + rg, off + o * npair + hp))
    kvblk = lambda o: pl.BlockSpec((seq, pw), lambda b, hp, rg: (b, off + o * npair + hp))
    return pl.pallas_call(
        functools.partial(_natten_kernel, rows=rows),
        grid=(bsz, npair, nrg),
        in_specs=[qblk(0), kvblk(1), kvblk(2), qblk(3),
                  pl.BlockSpec((None,) + table.shape[1:], lambda b, hp, rg: (hp, 0, 0, 0))],
        out_specs=pl.BlockSpec((tq, pw), lambda b, hp, rg: (b * nrg + rg, hp)),
        out_shape=jax.ShapeDtypeStruct((bsz * seq, c), BF16),
        compiler_params=_cparams("parallel", "parallel", "parallel"),
        name="natten_branch",
    )(proj, proj, proj, proj, table)


def even_layer_mix(h, norm_g, w_in, j, conv_w, conv_b, hy_w0, hy_b0, hy_w1, hy_b1, hy_w2, hy_b2, hy_wout,
                   hy_freq, hy_d, bsz, seq):
    c = h.shape[1]
    proj = norm_matmul(h, norm_g, w_in, j)
    kc_raw, ss = hyena_filter(seq, c, hy_w0, hy_b0, hy_w1, hy_b1, hy_w2, hy_b2, hy_wout, hy_freq)
    ya = hyena_branch(proj, conv_w, conv_b, hy_d, kc_raw, ss, bsz, seq, c)
    return ya, proj


def odd_layer_mix(h, norm_g, w_in, j, pool_w, pool_b, pool_scale, rpb, bsz, seq):
    c = h.shape[1]
    proj = norm_matmul(h, norm_g, w_in, j)
    yc = pool_branch(proj, pool_w, pool_b, pool_scale, seq, c)
    yd = natten_branch(proj, rpb, bsz, seq, c, 2 * c)
    return yc, yd


def kernel(x, p, norm_g, final_g, ev_w_in, ev_conv_w, ev_conv_b, hy_w0, hy_b0, hy_w1, hy_b1, hy_w2, hy_b2, hy_wout, hy_freq, hy_d, gm_norm_g, gm_ws, gm_bs, ev_w_out, od_w_in, pool_w, pool_b, pool_scale, na_rpb, od_w_out, ple_up, ple_gate_w, ple_g):
    bsz, seq, d = x.shape
    depth = p.shape[0]
    h = x.reshape(bsz * seq, d)
    ev_w_in16, od_w_in16 = ev_w_in.astype(BF16), od_w_in.astype(BF16)
    for i in range(depth):
        j = i // 2
        if i % 2 == 0:
            ya, proj = even_layer_mix(h, norm_g[i], ev_w_in16, j, ev_conv_w[j], ev_conv_b[j], hy_w0[j], hy_b0[j],
                                      hy_w1[j], hy_b1[j], hy_w2[j], hy_b2[j], hy_wout[j], hy_freq[j], hy_d[j],
                                      bsz, seq)
            yb, gmlp, w_out = None, (proj, 4 * d, gm_norm_g[j], gm_ws[j], gm_bs[j]), ev_w_out[j]
        else:
            ya, yb = odd_layer_mix(h, norm_g[i], od_w_in16, j, pool_w[j], pool_b[j], pool_scale[j], na_rpb[j],
                                   bsz, seq)
            gmlp, w_out = None, od_w_out[j]
        h = out_ple(ya, yb, h, p.reshape(depth, bsz * seq, -1), i, w_out, ple_g[i], ple_gate_w[i], ple_up[i],
                    final_g, final=(i == depth - 1), gmlp=gmlp)
    return h.reshape(bsz, seq, d)
```
